```python
import math
import jax
import jax.numpy as jnp
from jax import lax
import numpy as np

D_MODEL = 2048
BATCH = 8
SEQ = 2048
DEPTH = 2

GRID_W = 64
CTX_LEN = 256
N_MOD = 6
EPS = 1e-6
NEG_INF = -1e9
NA_HEADS = 16
NA_HEAD_DIM = 128
NA_WIDTH = NA_HEADS * NA_HEAD_DIM
NA_KH = 8
NA_KW = 16
NA_QB = 16
NA_KB = 32
SSM_D_INNER = 2048
SSM_HEAD_DIM = 64
SSM_HEADS = SSM_D_INNER // SSM_HEAD_DIM
SSM_GROUPS = 8
SSM_STATE = 128
SSM_CONV = 3
SSM_CHUNK = 128
SSM_XBC = SSM_D_INNER + 2 * SSM_GROUPS * SSM_STATE
EV_IN_COLS = NA_WIDTH + SSM_D_INNER + 2 * NA_WIDTH + SSM_XBC + 2 * SSM_HEADS
EV_KV_START = NA_WIDTH + SSM_D_INNER
EV_MIX_WIDTH = NA_WIDTH + SSM_D_INNER
HY_SHORT_CONV = 3
HY_EMB = 33
HY_FILTER_WIDTH = 64
HY_TARGET = 1e-2
HY_FAST_PCT = 0.3
HY_SLOW_PCT = 1.5
N_EXPERTS = 32
TOP_K = 4
MOE_FF = 2048
SWIGLU_LIMIT = 7.0
SWIGLU_ALPHA = 1.702
MOE_BLOCK = 256
N_EVEN = (DEPTH + 1) // 2
N_ODD = DEPTH // 2

kernel_name = 'hybrid_natten_ssd_hyena_moe_dit'


def rmsnorm(x, g):
    xf = x.astype(jnp.float32)
    xf = xf * lax.rsqrt(jnp.mean(xf * xf, axis=-1, keepdims=True) + EPS)
    return xf.astype(x.dtype) * g


def modulate(h, shift, scale):
    return h * (1 + scale) + shift


def chunk(m, k):
    return m[..., k * D_MODEL:(k + 1) * D_MODEL]


def split_cols(t, sizes):
    return jnp.split(t, np.cumsum(sizes)[:-1].tolist(), axis=-1)


def dwconv_centred(x, w, b):
    k = w.shape[0]
    pad = k // 2
    length = x.shape[1]
    xp = jnp.pad(x, ((0, 0), (pad, pad), (0, 0)))
    out = b
    for i in range(k):
        out = out + xp[:, i:i + length] * w[i]
    return out


def to_heads(t):
    return t.reshape(t.shape[0], t.shape[1], NA_HEADS, NA_HEAD_DIM)


def neighbourhood_attention(q, k, v, k_ctx, v_ctx, rpb):
    bsz, seq, heads, hd = q.shape
    rows = seq // GRID_W
    kh = min(NA_KH, rows)
    nj = GRID_W // NA_QB
    nk = kh * NA_KB
    r = np.arange(rows)
    row_start = np.clip(r - kh // 2, 0, rows - kh)
    key_rows = row_start[:, None] + np.arange(kh)[None, :]
    col_start = np.clip(np.arange(nj) * NA_QB - NA_KW // 2, 0, GRID_W - NA_KB)
    key_cols = col_start[:, None] + np.arange(NA_KB)[None, :]
    kidx = (key_rows[:, None, :, None] * GRID_W + key_cols[None, :, None, :]).reshape(rows, nj, nk)
    q_cols = np.arange(nj)[:, None] * NA_QB + np.arange(NA_QB)[None, :]
    win_start = np.clip(q_cols - NA_KW // 2, 0, GRID_W - NA_KW)
    kc = key_cols[:, None, :]
    valid = (kc >= win_start[:, :, None]) & (kc < win_start[:, :, None] + NA_KW)
    row_off = key_rows - r[:, None] + NA_KH - 1
    col_off = np.clip(kc - q_cols[:, :, None], -(NA_KW - 1), NA_KW - 1) + NA_KW - 1
    bias = rpb[:, row_off[:, None, None, :, None], col_off[None, :, :, None, :]]
    bias = jnp.where(valid[None, None, :, :, None, :], bias, NEG_INF)
    bias = bias.reshape(heads, rows, nj, NA_QB, nk).transpose(1, 0, 2, 3, 4)
    q_rows = q.reshape(bsz, rows, nj, NA_QB, heads, hd).transpose(1, 0, 2, 3, 4, 5)
    scale = hd ** -0.5

    def row_block(args):
        q_r, idx_r, bias_r = args
        k_r = k[:, idx_r]
        v_r = v[:, idx_r]
        s_lat = jnp.einsum('bjqhd,bjkhd->bhjqk', q_r, k_r).astype(jnp.float32) * scale
        s_lat = s_lat + bias_r[None].astype(jnp.float32)
        s_ctx = jnp.einsum('bjqhd,bchd->bhjqc', q_r, k_ctx).astype(jnp.float32) * scale
        p = jax.nn.softmax(jnp.concatenate([s_lat, s_ctx], axis=-1), axis=-1).astype(v.dtype)
        out = jnp.einsum('bhjqk,bjkhd->bjqhd', p[..., :nk], v_r)
        return out + jnp.einsum('bhjqc,bchd->bjqhd', p[..., nk:], v_ctx)

    o = lax.map(row_block, (q_rows, jnp.asarray(kidx, jnp.int32), bias))
    return o.transpose(1, 0, 2, 3, 4, 5).reshape(bsz, seq, heads * hd)


def context_attention(q_c, k_c, v_c):
    s = jnp.einsum('bqhd,bkhd->bhqk', q_c, k_c).astype(jnp.float32) * NA_HEAD_DIM ** -0.5
    p = jax.nn.softmax(s, axis=-1).astype(v_c.dtype)
    o = jnp.einsum('bhqk,bkhd->bqhd', p, v_c)
    return o.reshape(o.shape[0], o.shape[1], NA_WIDTH)


def segsum(a):
    t = a.shape[-1]
    rep = jnp.broadcast_to(a[..., :, None], a.shape + (t,))
    rep = jnp.where(np.tril(np.ones((t, t), bool), -1), rep, 0.0)
    out = jnp.cumsum(rep, axis=-2)
    return jnp.where(np.tril(np.ones((t, t), bool)), out, -jnp.inf)


def ssd_chunked(xdt, a, b, c, init, with_y):
    bsz, length, heads, hp = xdt.shape
    n = b.shape[-1]
    nc = length // SSM_CHUNK
    f32 = jnp.float32
    x = xdt.astype(f32).reshape(bsz, nc, SSM_CHUNK, heads, hp)
    b = b.astype(f32).reshape(bsz, nc, SSM_CHUNK, heads, n)
    c = c.astype(f32).reshape(bsz, nc, SSM_CHUNK, heads, n)
    a = a.astype(f32).reshape(bsz, nc, SSM_CHUNK, heads).transpose(0, 3, 1, 2)
    a_cum = jnp.cumsum(a, axis=-1)
    decay_to_end = jnp.exp(a_cum[..., -1:] - a_cum).transpose(0, 2, 3, 1)
    states = jnp.einsum('bclhn,bclhp->bchpn', b * decay_to_end[..., None], x)
    states = jnp.concatenate([init[:, None].astype(f32), states], axis=1)
    chunk_decay = jnp.exp(segsum(jnp.pad(a_cum[..., -1], ((0, 0), (0, 0), (1, 0)))))
    new_states = jnp.einsum('bhzc,bchpn->bzhpn', chunk_decay, states)
    final = new_states[:, -1]
    if not with_y:
        return None, final
    scores = jnp.einsum('bclhn,bcshn->bhcls', c, b) * jnp.exp(segsum(a))
    y_diag = jnp.einsum('bhcls,bcshp->bclhp', scores, x)
    y_off = jnp.einsum('bclhn,bchpn->bclhp', c, new_states[:, :-1])
    y_off = y_off * jnp.exp(a_cum).transpose(0, 2, 3, 1)[..., None]
    return (y_diag + y_off).reshape(bsz, length, heads, hp), final


def ssm_inputs(xbc, dt_raw, conv_w, conv_b, dt_bias):
    xbc = jax.nn.silu(dwconv_centred(xbc, conv_w, conv_b))
    xs, bs, cs = split_cols(xbc, [SSM_D_INNER, SSM_GROUPS * SSM_STATE, SSM_GROUPS * SSM_STATE])
    bsz, length = xs.shape[:2]
    rep = SSM_HEADS // SSM_GROUPS
    xh = xs.reshape(bsz, length, SSM_HEADS, SSM_HEAD_DIM)
    bh = jnp.repeat(bs.reshape(bsz, length, SSM_GROUPS, SSM_STATE), rep, axis=2)
    ch = jnp.repeat(cs.reshape(bsz, length, SSM_GROUPS, SSM_STATE), rep, axis=2)
    dt = jax.nn.softplus((dt_raw.reshape(bsz, length, 2, SSM_HEADS) + dt_bias).astype(jnp.float32))
    return xh, bh, ch, dt


def bidir_ssd(xh, bh, ch, dt, a_log, init_f, init_b, with_y):
    a = -jnp.exp(a_log.astype(jnp.float32))
    y_f, s_f = ssd_chunked(xh * dt[:, :, 0, :, None], dt[:, :, 0] * a[0], bh, ch, init_f, with_y)
    rev = lambda t: t[:, ::-1]
    dt_b = rev(dt[:, :, 1])
    y_b, s_b = ssd_chunked(rev(xh) * dt_b[..., None], dt_b * a[1], rev(bh), rev(ch), init_b, with_y)
    y = y_f + rev(y_b) if with_y else None
    return y, s_f, s_b


def ssm_output(y, xh, z, d_skip, gate_norm):
    bsz, length = z.shape[:2]
    y = y + xh.astype(jnp.float32) * d_skip[:, None]
    y = y.reshape(bsz, length, SSM_D_INNER) * jax.nn.silu(z.astype(jnp.float32))
    yg = y.reshape(bsz, length, SSM_GROUPS, SSM_D_INNER // SSM_GROUPS)
    yg = yg * lax.rsqrt(jnp.mean(yg * yg, axis=-1, keepdims=True) + EPS)
    return (yg.reshape(bsz, length, SSM_D_INNER) * gate_norm).astype(z.dtype)


def even_mixer(h, hc, w_in, conv_w, conv_b, q_norm, k_norm, rpb, a_log, dt_bias, d_skip, gate_norm, w_out, ctx_out):
    tail = [NA_WIDTH, NA_WIDTH, SSM_XBC, 2 * SSM_HEADS]
    q, z, k, v, xbc, dt_raw = split_cols(h @ w_in, [NA_WIDTH, SSM_D_INNER] + tail)
    if ctx_out:
        q_c, z_c, k_c, v_c, xbc_c, dt_raw_c = split_cols(hc @ w_in, [NA_WIDTH, SSM_D_INNER] + tail)
    else:
        k_c, v_c, xbc_c, dt_raw_c = split_cols(hc @ w_in[:, EV_KV_START:], tail)
    k_c = rmsnorm(to_heads(k_c), k_norm)
    v_c = to_heads(v_c)
    attn = neighbourhood_attention(rmsnorm(to_heads(q), q_norm), rmsnorm(to_heads(k), k_norm), to_heads(v), k_c, v_c, rpb)
    xh, bh, ch, dt = ssm_inputs(xbc, dt_raw, conv_w, conv_b, dt_bias)
    xh_c, bh_c, ch_c, dt_c = ssm_inputs(xbc_c, dt_raw_c, conv_w, conv_b, dt_bias)
    zero = jnp.zeros((h.shape[0], SSM_HEADS, SSM_HEAD_DIM, SSM_STATE), jnp.float32)
    y_ssm_c, s_f, s_b = bidir_ssd(xh_c, bh_c, ch_c, dt_c, a_log, zero, zero, ctx_out)
    y_ssm, _, _ = bidir_ssd(xh, bh, ch, dt, a_log, s_f, s_b, True)
    y = jnp.concatenate([attn, ssm_output(y_ssm, xh, z, d_skip, gate_norm)], axis=-1) @ w_out
    if not ctx_out:
        return y, None
    attn_c = context_attention(rmsnorm(to_heads(q_c), q_norm), k_c, v_c)
    y_c = jnp.concatenate([attn_c, ssm_output(y_ssm_c, xh_c, z_c, d_skip, gate_norm)], axis=-1) @ w_out
    return y, y_c


def hyena_filters(length, w1, b1, w2, b2, w3, b3, w4, freq):
    f32 = jnp.float32
    t = jnp.linspace(0.0, 1.0, length, dtype=f32)[:, None]
    bands = (HY_EMB - 1) // 2
    f = jnp.linspace(1e-4, bands - 1, bands, dtype=f32)[None, :]
    w = 2 * math.pi * jnp.arange(length, dtype=f32)[:, None] / length
    z = jnp.concatenate([t, jnp.cos(f * w), -jnp.sin(f * w)], axis=-1)
    hdn = jnp.sin(freq * (z @ w1 + b1))
    hdn = jnp.sin(freq * (hdn @ w2 + b2))
    hdn = jnp.sin(freq * (hdn @ w3 + b3))
    k = (hdn @ w4).astype(f32).reshape(length, 2, D_MODEL)
    min_decay = math.log(HY_TARGET) / HY_SLOW_PCT
    max_decay = math.log(HY_TARGET) / HY_FAST_PCT
    deltas = jnp.abs(jnp.linspace(min_decay, max_decay, D_MODEL, dtype=f32))
    k = k * jnp.exp(-t * deltas)[:, None, :]
    return k[:, 0], k[:, 1]


def long_conv_bidir(u, k_f, k_b, bias):
    length = u.shape[1]
    kern = jnp.concatenate([k_f[:1] + k_b[:1], k_f[1:], jnp.zeros((1, D_MODEL), jnp.float32), k_b[:0:-1]], axis=0)
    kf = jnp.fft.rfft(kern, n=2 * length, axis=0)
    uf = jnp.fft.rfft(u.astype(jnp.float32), n=2 * length, axis=1)
    y = jnp.fft.irfft(uf * kf[None], n=2 * length, axis=1)[:, :length]
    return y + u.astype(jnp.float32) * bias


def hyena_mixer(h, w_in, b_in, conv_w, conv_b, w1, b1, w2, b2, w3, b3, w4, freq, filt_bias, w_out, b_out):
    u = dwconv_centred(h @ w_in + b_in, conv_w, conv_b)
    x0, x1, v = split_cols(u, [D_MODEL, D_MODEL, D_MODEL])
    k_f, k_b = hyena_filters(h.shape[1], w1, b1, w2, b2, w3, b3, w4, freq)
    y = x0.astype(jnp.float32) * long_conv_bidir(v * x1, k_f, k_b, filt_bias)
    return y.astype(h.dtype) @ w_out + b_out


def clamped_swiglu(x, w1, b1, w2, b2):
    gu = x @ w1 + b1
    g = jnp.minimum(gu[..., :MOE_FF], SWIGLU_LIMIT)
    u = jnp.clip(gu[..., MOE_FF:], -SWIGLU_LIMIT, SWIGLU_LIMIT)
    return ((u + 1) * g * jax.nn.sigmoid(SWIGLU_ALPHA * g)) @ w2 + b2


def moe_ffn(h, w_r, b_r, w1, b1, w2, b2):
    bsz, length, d = h.shape
    xt = h.reshape(-1, d)
    n = xt.shape[0]
    logits = (xt @ w_r + b_r).astype(jnp.float32)
    top_logit, top_idx = lax.top_k(logits, TOP_K)
    gate = jax.nn.softmax(top_logit, axis=-1)
    n_slot = n * TOP_K
    flat_e = top_idx.reshape(-1)
    order = jnp.argsort(flat_e)
    e_sorted = flat_e[order]
    counts = jnp.bincount(flat_e, length=N_EXPERTS)
    padded = (counts + MOE_BLOCK - 1) // MOE_BLOCK * MOE_BLOCK
    pad_end = jnp.cumsum(padded)
    grp_begin = jnp.cumsum(counts) - counts
    dest = (pad_end - padded)[e_sorted] + jnp.arange(n_slot) - grp_begin[e_sorted]
    n_blocks = -(-n_slot // MOE_BLOCK) + N_EXPERTS
    slot_tok = jnp.full((n_blocks * MOE_BLOCK,), n, jnp.int32).at[dest].set((order // TOP_K).astype(jnp.int32))
    slot_gate = jnp.zeros((n_blocks * MOE_BLOCK,), jnp.float32).at[dest].set(gate.reshape(-1)[order])
    block_e = jnp.minimum(jnp.searchsorted(pad_end, jnp.arange(n_blocks) * MOE_BLOCK, side='right'), N_EXPERTS - 1)
    x_pad = jnp.concatenate([xt, jnp.zeros((1, d), xt.dtype)], axis=0)

    def body(acc, blk):
        tok, g, e = blk
        yb = clamped_swiglu(x_pad[tok], w1[e], b1[e], w2[e], b2[e])
        return acc.at[tok].add(yb.astype(jnp.float32) * g[:, None]), None

    acc, _ = lax.scan(body, jnp.zeros((n + 1, d), jnp.float32),
                      (slot_tok.reshape(n_blocks, MOE_BLOCK), slot_gate.reshape(n_blocks, MOE_BLOCK), block_e))
    return acc[:n].astype(h.dtype).reshape(bsz, length, d)


def setup_inputs(seed: int = 0) -> dict:
    key = jax.random.key(seed)
    ks = iter(jax.random.split(key, 48))
    D = D_MODEL

    def nrm(shape, std):
        return jax.random.normal(next(ks), shape, jnp.float32) * std

    a_log = jnp.log(jax.random.uniform(next(ks), (N_EVEN, 2, SSM_HEADS), jnp.float32, 1.0, 16.0))
    dt0 = jnp.exp(jax.random.uniform(next(ks), (N_EVEN, 2, SSM_HEADS), jnp.float32, math.log(1e-3), math.log(1e-1)))
    return {
        'x': nrm((BATCH, SEQ, D), 1.0),
        'c': nrm((BATCH, D), 1.0),
        'ctx': nrm((BATCH, CTX_LEN, D), 1.0),
        'c_ctx': nrm((D,), 1.0),
        'ada_w': nrm((DEPTH, D, N_MOD * D), 0.5 * D ** -0.5),
        'ada_b': nrm((DEPTH, N_MOD * D), 0.02),
        'norm_mix': 1.0 + nrm((DEPTH, D), 0.02),
        'norm_ffn': 1.0 + nrm((DEPTH, D), 0.02),
        'ev_w_in': nrm((N_EVEN, D, EV_IN_COLS), D ** -0.5),
        'ev_conv_w': nrm((N_EVEN, SSM_CONV, SSM_XBC), 0.5),
        'ev_conv_b': nrm((N_EVEN, SSM_XBC), 0.02),
        'ev_q_norm': 1.0 + nrm((N_EVEN, NA_HEAD_DIM), 0.02),
        'ev_k_norm': 1.0 + nrm((N_EVEN, NA_HEAD_DIM), 0.02),
        'ev_rpb': nrm((N_EVEN, NA_HEADS, 2 * NA_KH - 1, 2 * NA_KW - 1), 0.5),
        'ev_a_log': a_log,
        'ev_dt_bias': dt0 + jnp.log(-jnp.expm1(-dt0)),
        'ev_d_skip': 1.0 + nrm((N_EVEN, SSM_HEADS), 0.1),
        'ev_gate_norm': 1.0 + nrm((N_EVEN, SSM_D_INNER), 0.02),
        'ev_w_out': nrm((N_EVEN, EV_MIX_WIDTH, D), EV_MIX_WIDTH ** -0.5),
        'od_w_in': nrm((N_ODD, D, 3 * D), D ** -0.5),
        'od_b_in': nrm((N_ODD, 3 * D), 0.02),
        'od_conv_w': nrm((N_ODD, HY_SHORT_CONV, 3 * D), 0.5),
        'od_conv_b': nrm((N_ODD, 3 * D), 0.02),
        'od_filt_w1': nrm((N_ODD, HY_EMB, HY_FILTER_WIDTH), HY_EMB ** -0.5),
        'od_filt_b1': nrm((N_ODD, HY_FILTER_WIDTH), 0.02),
        'od_filt_w2': nrm((N_ODD, HY_FILTER_WIDTH, HY_FILTER_WIDTH), HY_FILTER_WIDTH ** -0.5),
        'od_filt_b2': nrm((N_ODD, HY_FILTER_WIDTH), 0.02),
        'od_filt_w3': nrm((N_ODD, HY_FILTER_WIDTH, HY_FILTER_WIDTH), HY_FILTER_WIDTH ** -0.5),
        'od_filt_b3': nrm((N_ODD, HY_FILTER_WIDTH), 0.02),
        'od_filt_w4': nrm((N_ODD, HY_FILTER_WIDTH, 2 * D), 0.1 * HY_FILTER_WIDTH ** -0.5),
        'od_filt_freq': 1.0 + nrm((N_ODD, HY_FILTER_WIDTH), 0.1),
        'od_filt_bias': nrm((N_ODD, D), 0.5),
        'od_w_out': nrm((N_ODD, D, D), D ** -0.5),
        'od_b_out': nrm((N_ODD, D), 0.02),
        'moe_router_w': nrm((DEPTH, D, N_EXPERTS), D ** -0.5),
        'moe_router_b': nrm((DEPTH, N_EXPERTS), 0.01),
        'moe_w1': nrm((DEPTH, N_EXPERTS, D, 2 * MOE_FF), D ** -0.5),
        'moe_b1': nrm((DEPTH, N_EXPERTS, 2 * MOE_FF), 0.02),
        'moe_w2': nrm((DEPTH, N_EXPERTS, MOE_FF, D), MOE_FF ** -0.5),
        'moe_b2': nrm((DEPTH, N_EXPERTS, D), 0.02),
    }


def reference(x, c, ctx, c_ctx, ada_w, ada_b, norm_mix, norm_ffn, ev_w_in, ev_conv_w, ev_conv_b, ev_q_norm,
              ev_k_norm, ev_rpb, ev_a_log, ev_dt_bias, ev_d_skip, ev_gate_norm, ev_w_out, od_w_in, od_b_in,
              od_conv_w, od_conv_b, od_filt_w1, od_filt_b1, od_filt_w2, od_filt_b2, od_filt_w3, od_filt_b3,
              od_filt_w4, od_filt_freq, od_filt_bias, od_w_out, od_b_out, moe_router_w, moe_router_b, moe_w1,
              moe_b1, moe_w2, moe_b2):
    last_reader = ((DEPTH - 1) // 2) * 2
    sc = jax.nn.silu(c)
    scc = jax.nn.silu(c_ctx)
    for i in range(DEPTH):
        ctx_out = i < last_reader
        j = i // 2
        mod = (sc @ ada_w[i] + ada_b[i])[:, None, :]
        h = modulate(rmsnorm(x, norm_mix[i]), chunk(mod, 0), chunk(mod, 1))
        if i % 2 == 0 or ctx_out:
            mod_c = scc @ ada_w[i] + ada_b[i]
            hc = modulate(rmsnorm(ctx, norm_mix[i]), chunk(mod_c, 0), chunk(mod_c, 1))
        if i % 2 == 0:
            y, y_c = even_mixer(h, hc, ev_w_in[j], ev_conv_w[j], ev_conv_b[j], ev_q_norm[j], ev_k_norm[j],
                                ev_rpb[j], ev_a_log[j], ev_dt_bias[j], ev_d_skip[j], ev_gate_norm[j],
                                ev_w_out[j], ctx_out)
        else:
            hy = (od_w_in[j], od_b_in[j], od_conv_w[j], od_conv_b[j], od_filt_w1[j], od_filt_b1[j],
                  od_filt_w2[j], od_filt_b2[j], od_filt_w3[j], od_filt_b3[j], od_filt_w4[j],
                  od_filt_freq[j], od_filt_bias[j], od_w_out[j], od_b_out[j])
            y = hyena_mixer(h, *hy)
            y_c = hyena_mixer(hc, *hy) if ctx_out else None
        x = x + chunk(mod, 2) * y
        moe_p = (moe_router_w[i], moe_router_b[i], moe_w1[i], moe_b1[i], moe_w2[i], moe_b2[i])
        x = x + chunk(mod, 5) * moe_ffn(modulate(rmsnorm(x, norm_ffn[i]), chunk(mod, 3), chunk(mod, 4)), *moe_p)
        if ctx_out:
            ctx = ctx + chunk(mod_c, 2) * y_c
            ctx = ctx + chunk(mod_c, 5) * moe_ffn(modulate(rmsnorm(ctx, norm_ffn[i]), chunk(mod_c, 3), chunk(mod_c, 4)), *moe_p)
    return x
```

```python
import functools
import math

import jax
import jax.numpy as jnp
import numpy as np
from jax import lax
from jax.experimental import pallas as pl
from jax.experimental.pallas import tpu as pltpu

F32 = jnp.float32
BF16 = jnp.bfloat16
HI = lax.Precision.HIGHEST

D_MODEL = 2048
N_MOD = 6
EPS = 1e-6
NEG_INF = -1e9
GRID_W = 64
NA_HEADS = 16
NA_HEAD_DIM = 128
NA_WIDTH = NA_HEADS * NA_HEAD_DIM
NA_KH = 8
NA_KW = 16
SSM_D_INNER = 2048
SSM_HEAD_DIM = 64
SSM_HEADS = SSM_D_INNER // SSM_HEAD_DIM
SSM_GROUPS = 8
SSM_STATE = 128
SSM_CHUNK = 128
SSM_XBC = SSM_D_INNER + 2 * SSM_GROUPS * SSM_STATE
HEADS_PER_GROUP = SSM_HEADS // SSM_GROUPS
GROUP_WIDTH = HEADS_PER_GROUP * SSM_HEAD_DIM
HY_EMB = 33
HY_FILTER_WIDTH = 64
HY_TARGET = 1e-2
HY_FAST_PCT = 0.3
HY_SLOW_PCT = 1.5
N_EXPERTS = 32
TOP_K = 4
MOE_FF = 2048
SWIGLU_LIMIT = 7.0
SWIGLU_ALPHA = 1.702
MOE_ROWS = 512
MOE_FF_TILE = 512
LANES = 128
VMEM_LIMIT = 56 * 1024 * 1024


def _cparams(sem, vmem=VMEM_LIMIT):
    return pltpu.CompilerParams(dimension_semantics=sem, vmem_limit_bytes=vmem)


def _dot(a, b):
    return jnp.dot(a, b, preferred_element_type=F32)


def _dot_nt(a, b):
    return lax.dot_general(a, b, (((1,), (1,)), ((), ())), preferred_element_type=F32)


def _dot_tn(a, b):
    return lax.dot_general(a, b, (((0,), (0,)), ((), ())), preferred_element_type=F32)


def _silu(x):
    return x / (1.0 + jnp.exp(-x))


def _softplus(x):
    return jnp.maximum(x, 0.0) + jnp.log1p(jnp.exp(-jnp.abs(x)))


def _ada_kernel(c_ref, w_ref, b_ref, o_ref):
    sc = _silu(c_ref[...])
    hi = sc.astype(BF16)
    lo = (sc - hi.astype(F32)).astype(BF16)
    w = w_ref[0].astype(BF16)
    o_ref[0] = _dot(hi, w) + _dot(lo, w) + b_ref[0]


def _ada_mod(cc, ada_w, ada_b):
    depth, d, n = ada_w.shape
    tn = 1024
    return pl.pallas_call(
        _ada_kernel,
        grid=(depth, n // tn),
        in_specs=[
            pl.BlockSpec((cc.shape[0], d), lambda i, j: (0, 0)),
            pl.BlockSpec((1, d, tn), lambda i, j: (i, 0, j)),
            pl.BlockSpec((1, 1, tn), lambda i, j: (i, 0, j)),
        ],
        out_specs=pl.BlockSpec((1, cc.shape[0], tn), lambda i, j: (i, 0, j)),
        out_shape=jax.ShapeDtypeStruct((depth, cc.shape[0], n), F32),
        compiler_params=_cparams(("parallel", "parallel")),
        name="ada_mod",
    )(cc, ada_w, ada_b.reshape(depth, 1, n))


def _nmm_kernel(x_ref, g_ref, sh_ref, sc_ref, w_ref, b_ref, o_ref, h_scr):
    @pl.when(pl.program_id(2) == 0)
    def _():
        x = x_ref[0]
        ms = jnp.mean(x * x, axis=-1, keepdims=True)
        xn = x * lax.rsqrt(ms + EPS) * g_ref[...]
        h_scr[...] = (xn * (1.0 + sc_ref[0]) + sh_ref[0]).astype(BF16)

    o_ref[0] = (_dot(h_scr[...], w_ref[...]) + b_ref[...]).astype(o_ref.dtype)


def _norm_mod_matmul(x, g, shift, scale, w, bias, out_dtype, tm, tn):
    b, s, d = x.shape
    n = w.shape[1]
    tm = min(tm, s)
    return pl.pallas_call(
        _nmm_kernel,
        grid=(b, s // tm, n // tn),
        in_specs=[
            pl.BlockSpec((1, tm, d), lambda bi, i, j: (bi, i, 0)),
            pl.BlockSpec((1, d), lambda bi, i, j: (0, 0)),
            pl.BlockSpec((1, 1, d), lambda bi, i, j: (bi, 0, 0)),
            pl.BlockSpec((1, 1, d), lambda bi, i, j: (bi, 0, 0)),
            pl.BlockSpec((d, tn), lambda bi, i, j: (0, j)),
            pl.BlockSpec((1, tn), lambda bi, i, j: (0, j)),
        ],
        out_specs=pl.BlockSpec((1, tm, tn), lambda bi, i, j: (bi, i, j)),
        out_shape=jax.ShapeDtypeStruct((b, s, n), out_dtype),
        scratch_shapes=[pltpu.VMEM((tm, d), BF16)],
        compiler_params=_cparams(("parallel", "parallel", "arbitrary")),
        name="norm_mod_matmul",
    )(x, g.reshape(1, d), shift.reshape(b, 1, d), scale.reshape(b, 1, d), w, bias.reshape(1, n))


def _attn_bias_table(rpb):
    kh = NA_KH
    v = np.arange(kh)[:, None, None, None]
    qc = np.arange(GRID_W)[None, :, None, None]
    i = np.arange(kh)[None, None, :, None]
    kc = np.arange(GRID_W)[None, None, None, :]
    row_off = np.broadcast_to(i + v, (kh, GRID_W, kh, GRID_W))
    col_off = np.broadcast_to(np.clip(kc - qc, -(NA_KW - 1), NA_KW - 1) + NA_KW - 1, (kh, GRID_W, kh, GRID_W))
    ws = np.clip(qc - NA_KW // 2, 0, GRID_W - NA_KW)
    valid = np.broadcast_to((kc >= ws) & (kc < ws + NA_KW), (kh, GRID_W, kh, GRID_W))
    bias = rpb[:, row_off, col_off]
    bias = jnp.where(valid[None], bias, NEG_INF)
    return bias.reshape(rpb.shape[0], kh, GRID_W, kh * GRID_W).astype(F32)


def _head_rmsnorm(t, g):
    ms = jnp.mean(t * t, axis=-1, keepdims=True)
    return t * lax.rsqrt(ms + EPS) * g


def _attn_kernel(q_ref, k_ref, v_ref, kc_ref, vc_ref, qg_ref, kg_ref, bias_ref, o_ref, qs, ks, kcs):
    rows = q_ref.shape[1] // GRID_W
    win = NA_KH * GRID_W
    qs[...] = (_head_rmsnorm(q_ref[0].astype(F32), qg_ref[...]) * NA_HEAD_DIM ** -0.5).astype(BF16)
    ks[...] = _head_rmsnorm(k_ref[0].astype(F32), kg_ref[...]).astype(BF16)
    kcs[...] = _head_rmsnorm(kc_ref[0].astype(F32), kg_ref[...]).astype(BF16)

    def body(r, carry):
        row_start = jnp.clip(r - NA_KH // 2, 0, rows - NA_KH)
        var = row_start - r + NA_KH - 1
        q0 = pl.multiple_of(r * GRID_W, GRID_W)
        k0 = pl.multiple_of(row_start * GRID_W, GRID_W)
        q_r = qs[pl.ds(q0, GRID_W), :]
        s_lat = _dot_nt(q_r, ks[pl.ds(k0, win), :]) + bias_ref[0, var]
        s_ctx = _dot_nt(q_r, kcs[...])
        m = jnp.maximum(jnp.max(s_lat, axis=-1, keepdims=True), jnp.max(s_ctx, axis=-1, keepdims=True))
        p_lat = jnp.exp(s_lat - m)
        p_ctx = jnp.exp(s_ctx - m)
        den = jnp.sum(p_lat, axis=-1, keepdims=True) + jnp.sum(p_ctx, axis=-1, keepdims=True)
        o = _dot(p_lat.astype(BF16), v_ref[0, pl.ds(k0, win), :]) + _dot(p_ctx.astype(BF16), vc_ref[0])
        o_ref[0, pl.ds(q0, GRID_W), :] = (o / den).astype(o_ref.dtype)
        return carry

    lax.fori_loop(0, rows, body, 0)


def _attention(proj, proj_c, q_norm, k_norm, bias_tab):
    b, s, _ = proj.shape
    ctx = proj_c.shape[1]
    hd = NA_HEAD_DIM
    nh = NA_HEADS
    return pl.pallas_call(
        _attn_kernel,
        grid=(b, nh),
        in_specs=[
            pl.BlockSpec((1, s, hd), lambda bi, h: (bi, 0, h)),
            pl.BlockSpec((1, s, hd), lambda bi, h: (bi, 0, 2 * nh + h)),
            pl.BlockSpec((1, s, hd), lambda bi, h: (bi, 0, 3 * nh + h)),
            pl.BlockSpec((1, ctx, hd), lambda bi, h: (bi, 0, h)),
            pl.BlockSpec((1, ctx, hd), lambda bi, h: (bi, 0, nh + h)),
            pl.BlockSpec((1, hd), lambda bi, h: (0, 0)),
            pl.BlockSpec((1, hd), lambda bi, h: (0, 0)),
            pl.BlockSpec((1, NA_KH, GRID_W, NA_KH * GRID_W), lambda bi, h: (h, 0, 0, 0)),
        ],
        out_specs=pl.BlockSpec((1, s, hd), lambda bi, h: (bi, 0, h)),
        out_shape=jax.ShapeDtypeStruct((b, s, nh * hd), BF16),
        scratch_shapes=[pltpu.VMEM((s, hd), BF16), pltpu.VMEM((s, hd), BF16), pltpu.VMEM((ctx, hd), BF16)],
        compiler_params=_cparams(("parallel", "parallel")),
        name="nbr_attention",
    )(proj, proj, proj, proj_c, proj_c, q_norm.reshape(1, hd), k_norm.reshape(1, hd), bias_tab)


def _expand(v, e):
    hi = v.astype(BF16)
    r1 = v - hi.astype(F32)
    mid = r1.astype(BF16)
    lo = (r1 - mid.astype(F32)).astype(BF16)
    return _dot(hi, e) + _dot(mid, e) + _dot(lo, e)


def _ssd_kernel(*refs, with_y, nc):
    (xr_ref, br_ref, cr_ref, cwx, cbx, cwb, cbb, cwc, cbc, dtc_ref, dtr_ref, biasc, biasr, ac_ref, ar_ref), rest = refs[:15], refs[15:]
    if with_y:
        z_ref, dsk_ref, gn_ref, sf0_ref, sb0_ref, y_ref, xs, bs, cs, xf, sfa, sba = rest
    else:
        sf_out, sb_out, xs, bs, cs = rest
    ck = SSM_CHUNK
    length = nc * ck
    nh = HEADS_PER_GROUP
    gw = GROUP_WIDTH

    def conv_silu(raw_ref, w_ref, b_ref):
        x = raw_ref[0].astype(F32)
        rows = lax.broadcasted_iota(jnp.int32, x.shape, 0)
        xm1 = jnp.where(rows == 0, 0.0, pltpu.roll(x, 1, 0))
        xp1 = jnp.where(rows == length - 1, 0.0, pltpu.roll(x, length - 1, 0))
        return _silu(b_ref[...] + xm1 * w_ref[0:1, :] + x * w_ref[1:2, :] + xp1 * w_ref[2:3, :])

    xv = conv_silu(xr_ref, cwx, cbx)
    xs[...] = xv.astype(BF16)
    if with_y:
        xf[...] = xv
    bs[...] = conv_silu(br_ref, cwb, cbb).astype(BF16)
    cs[...] = conv_silu(cr_ref, cwc, cbc).astype(BF16)

    li = lax.broadcasted_iota(jnp.int32, (ck, ck), 0)
    ki = lax.broadcasted_iota(jnp.int32, (ck, ck), 1)
    lower = ki <= li
    upper = ki >= li
    tril = lower.astype(F32)
    triu = upper.astype(F32)
    erow = lax.broadcasted_iota(jnp.int32, (2 * nh, gw), 0)
    ecol = lax.broadcasted_iota(jnp.int32, (2 * nh, gw), 1) // SSM_HEAD_DIM
    e_f = (erow == ecol).astype(BF16)
    e_b = (erow == ecol + nh).astype(BF16)
    srow = lax.broadcasted_iota(jnp.int32, (2 * nh, ck), 0)
    colid = lax.broadcasted_iota(jnp.int32, (ck, 2 * nh), 1)
    rowid = lax.broadcasted_iota(jnp.int32, (2 * nh, ck), 0)
    lane_head = lax.broadcasted_iota(jnp.int32, (ck, gw), 1) // SSM_HEAD_DIM

    def col_vals(c):
        l0 = pl.multiple_of(c * ck, ck)
        dtc = _softplus(dtc_ref[0, 0, pl.ds(l0, ck), :] + biasc[0])
        a_c = dtc * ac_ref[0]
        cf = jnp.dot(tril, a_c, precision=HI, preferred_element_type=F32)
        rb = jnp.dot(triu, a_c, precision=HI, preferred_element_type=F32)
        return dtc, jnp.where(colid < nh, cf, rb), cf[ck - 1:ck, :]

    def row_vals(c):
        dtr = _softplus(dtr_ref[0, 0, c] + biasr[0])
        a_r = dtr * ar_ref[0]
        cfr = jnp.dot(a_r, triu, precision=HI, preferred_element_type=F32)
        rbr = jnp.dot(a_r, tril, precision=HI, preferred_element_type=F32)
        return dtr, jnp.where(rowid < nh, cfr, rbr)

    def state_step(c, s, e_dir, store):
        l0 = pl.multiple_of(c * ck, ck)
        if store is not None:
            store[c] = s
        dtc, dc, tot = col_vals(c)
        wx = _expand(jnp.exp(tot - dc) * dtc, e_dir)
        xw = (xs[pl.ds(l0, ck), :].astype(F32) * wx).astype(BF16)
        contrib = _dot_tn(bs[pl.ds(l0, ck), :], xw)
        dec = _expand(jnp.broadcast_to(jnp.exp(tot), (2 * nh, 2 * nh)), e_dir)[0:1, :]
        return s * dec + contrib

    if with_y:
        s_f0 = sf0_ref[0, 0]
        s_b0 = sb0_ref[0, 0]
    else:
        s_f0 = jnp.zeros((SSM_STATE, gw), F32)
        s_b0 = s_f0
    s_f = lax.fori_loop(0, nc, lambda c, s: state_step(c, s, e_f, sfa if with_y else None), s_f0)
    s_b = lax.fori_loop(0, nc, lambda i, s: state_step(nc - 1 - i, s, e_b, sba if with_y else None), s_b0)
    if not with_y:
        sf_out[0, 0] = s_f
        sb_out[0, 0] = s_b
        return

    def out_step(c, carry):
        l0 = pl.multiple_of(c * ck, ck)
        dtc, dc, _ = col_vals(c)
        dtr, dr = row_vals(c)
        b_c = bs[pl.ds(l0, ck), :]
        c_c = cs[pl.ds(l0, ck), :]
        x_c = xs[pl.ds(l0, ck), :]
        g = _dot_nt(c_c, b_c)
        acc = jnp.zeros((ck, gw), F32)
        for j in range(nh):
            cfb = _expand(dc, (srow == j).astype(BF16))
            rbb = _expand(dc, (srow == nh + j).astype(BF16))
            lf = jnp.exp(jnp.where(lower, cfb - dr[j:j + 1, :], -1e30)) * dtr[j:j + 1, :]
            lb = jnp.exp(jnp.where(upper, rbb - dr[nh + j:nh + j + 1, :], -1e30)) * dtr[nh + j:nh + j + 1, :]
            m = (g * (lf + lb)).astype(BF16)
            acc = acc + _dot(m, jnp.where(lane_head == j, x_c, jnp.zeros_like(x_c)))
        edc = jnp.exp(dc)
        acc = acc + _dot(c_c, sfa[c].astype(BF16)) * _expand(edc, e_f)
        acc = acc + _dot(c_c, sba[c].astype(BF16)) * _expand(edc, e_b)
        y = acc + xf[pl.ds(l0, ck), :] * dsk_ref[...]
        y = y * _silu(z_ref[0, pl.ds(l0, ck), :].astype(F32))
        ms = jnp.mean(y * y, axis=-1, keepdims=True)
        y_ref[0, pl.ds(l0, ck), :] = (y * lax.rsqrt(ms + EPS) * gn_ref[...]).astype(y_ref.dtype)
        return carry

    lax.fori_loop(0, nc, out_step, 0)


def _ssd(proj, x_blk0, z_blk0, dt_raw, conv_w, conv_b, dt_bias, a_log, d_skip, gate_norm, init):
    b, length, _ = proj.shape
    nc = length // SSM_CHUNK
    ng, nh, gw, st = SSM_GROUPS, HEADS_PER_GROUP, GROUP_WIDTH, SSM_STATE
    with_y = init is not None
    b_blk0 = x_blk0 * (gw // st) + SSM_D_INNER // st
    c_blk0 = b_blk0 + ng
    arr = lambda t: t.reshape(2, ng, nh).transpose(1, 0, 2).reshape(ng, 2 * nh)
    dtg = dt_raw.reshape(b, length, 2, ng, nh).transpose(0, 3, 1, 2, 4).reshape(b, ng, length, 2 * nh)
    dtg_row = dtg.reshape(b, ng, nc, SSM_CHUNK, 2 * nh).transpose(0, 1, 2, 4, 3)
    bias_g = arr(dt_bias)
    a_g = arr(-jnp.exp(a_log.astype(F32)))
    cw_x, cw_b, cw_c = conv_w[:, :SSM_D_INNER], conv_w[:, SSM_D_INNER:SSM_D_INNER + ng * st], conv_w[:, SSM_D_INNER + ng * st:]
    cb = conv_b.reshape(1, -1)
    cb_x, cb_b, cb_c = cb[:, :SSM_D_INNER], cb[:, SSM_D_INNER:SSM_D_INNER + ng * st], cb[:, SSM_D_INNER + ng * st:]
    in_specs = [
        pl.BlockSpec((1, length, gw), lambda bi, g: (bi, 0, x_blk0 + g)),
        pl.BlockSpec((1, length, st), lambda bi, g: (bi, 0, b_blk0 + g)),
        pl.BlockSpec((1, length, st), lambda bi, g: (bi, 0, c_blk0 + g)),
        pl.BlockSpec((3, gw), lambda bi, g: (0, g)),
        pl.BlockSpec((1, gw), lambda bi, g: (0, g)),
        pl.BlockSpec((3, st), lambda bi, g: (0, g)),
        pl.BlockSpec((1, st), lambda bi, g: (0, g)),
        pl.BlockSpec((3, st), lambda bi, g: (0, g)),
        pl.BlockSpec((1, st), lambda bi, g: (0, g)),
        pl.BlockSpec((1, 1, length, 2 * nh), lambda bi, g: (bi, g, 0, 0)),
        pl.BlockSpec((1, 1, nc, 2 * nh, SSM_CHUNK), lambda bi, g: (bi, g, 0, 0, 0)),
        pl.BlockSpec((1, 1, 2 * nh), lambda bi, g: (g, 0, 0)),
        pl.BlockSpec((1, 2 * nh, 1), lambda bi, g: (g, 0, 0)),
        pl.BlockSpec((1, 1, 2 * nh), lambda bi, g: (g, 0, 0)),
        pl.BlockSpec((1, 2 * nh, 1), lambda bi, g: (g, 0, 0)),
    ]
    args = [proj, proj, proj, cw_x, cb_x, cw_b, cb_b, cw_c, cb_c, dtg, dtg_row,
            bias_g.reshape(ng, 1, 2 * nh), bias_g.reshape(ng, 2 * nh, 1), a_g.reshape(ng, 1, 2 * nh), a_g.reshape(ng, 2 * nh, 1)]
    scratch = [pltpu.VMEM((length, gw), BF16), pltpu.VMEM((length, st), BF16), pltpu.VMEM((length, st), BF16)]
    state_spec = pl.BlockSpec((1, 1, st, gw), lambda bi, g: (bi, g, 0, 0))
    state_shape = jax.ShapeDtypeStruct((b, ng, st, gw), F32)
    if with_y:
        in_specs += [
            pl.BlockSpec((1, length, gw), lambda bi, g: (bi, 0, z_blk0 + g)),
            pl.BlockSpec((1, gw), lambda bi, g: (0, g)),
            pl.BlockSpec((1, gw), lambda bi, g: (0, g)),
            state_spec, state_spec,
        ]
        args += [proj, jnp.repeat(d_skip, SSM_HEAD_DIM).reshape(1, -1), gate_norm.reshape(1, -1), init[0], init[1]]
        out_specs = pl.BlockSpec((1, length, gw), lambda bi, g: (bi, 0, g))
        out_shape = jax.ShapeDtypeStruct((b, length, SSM_D_INNER), BF16)
        scratch += [pltpu.VMEM((length, gw), F32), pltpu.VMEM((nc, st, gw), F32), pltpu.VMEM((nc, st, gw), F32)]
    else:
        out_specs = [state_spec, state_spec]
        out_shape = [state_shape, state_shape]
    return pl.pallas_call(
        functools.partial(_ssd_kernel, with_y=with_y, nc=nc),
        grid=(b, ng),
        in_specs=in_specs,
        out_specs=out_specs,
        out_shape=out_shape,
        scratch_shapes=scratch,
        compiler_params=_cparams(("parallel", "parallel")),
        name="ssd_main" if with_y else "ssd_ctx",
    )(*args)


def _proj_res_kernel(*refs, n_lhs):
    a_refs, w_refs = refs[:n_lhs], refs[n_lhs:2 * n_lhs]
    b_ref, gate_ref, res_ref, o_ref = refs[2 * n_lhs:]
    acc = _dot(a_refs[0][0], w_refs[0][...])
    for a_ref, w_ref in zip(a_refs[1:], w_refs[1:]):
        acc = acc + _dot(a_ref[0], w_ref[...])
    o_ref[0] = res_ref[0] + gate_ref[0] * (acc + b_ref[...])


def _proj_residual(lhs, w, bias, gate, resid, tm=1024, tn=1024):
    b, s, k = lhs[0].shape
    n = w.shape[1]
    n_lhs = len(lhs)
    in_specs = [pl.BlockSpec((1, tm, k), lambda bi, i, j: (bi, i, 0)) for _ in lhs]
    in_specs += [pl.BlockSpec((k, tn), functools.partial(lambda bi, i, j, t: (t, j), t=t)) for t in range(n_lhs)]
    in_specs += [
        pl.BlockSpec((1, tn), lambda bi, i, j: (0, j)),
        pl.BlockSpec((1, 1, tn), lambda bi, i, j: (bi, 0, j)),
        pl.BlockSpec((1, tm, tn), lambda bi, i, j: (bi, i, j)),
    ]
    return pl.pallas_call(
        functools.partial(_proj_res_kernel, n_lhs=n_lhs),
        grid=(b, s // tm, n // tn),
        in_specs=in_specs,
        out_specs=pl.BlockSpec((1, tm, tn), lambda bi, i, j: (bi, i, j)),
        out_shape=jax.ShapeDtypeStruct((b, s, n), F32),
        compiler_params=_cparams(("parallel", "parallel", "parallel")),
        name="proj_residual",
    )(*lhs, *([w] * n_lhs), bias.reshape(1, n), gate.reshape(b, 1, n), resid)


def _hy_filter_kernel(tw_ref, f_ref, w1t, w1c, w1s, b1, w2, b2, w3, b3, fr, w4_ref, dl_ref, o_ref, h_scr):
    dot_hi = lambda a, b: jnp.dot(a, b, precision=HI, preferred_element_type=F32)

    @pl.when(pl.program_id(0) == 0)
    def _():
        t = tw_ref[:, 0:1]
        ang = tw_ref[:, 1:2] * f_ref[...]
        pre = t * w1t[...] + dot_hi(jnp.cos(ang), w1c[...]) + dot_hi(-jnp.sin(ang), w1s[...]) + b1[...]
        h = jnp.sin(fr[...] * pre)
        h = jnp.sin(fr[...] * (dot_hi(h, w2[...]) + b2[...]))
        h_scr[...] = jnp.sin(fr[...] * (dot_hi(h, w3[...]) + b3[...]))

    o_ref[...] = dot_hi(h_scr[...], w4_ref[...]) * jnp.exp(-tw_ref[:, 0:1] * dl_ref[...])


def _hyena_filters(length, w1, b1, w2, b2, w3, b3, w4, freq):
    bands = (HY_EMB - 1) // 2
    fw = HY_FILTER_WIDTH
    t = jnp.linspace(0.0, 1.0, length, dtype=F32)
    w = 2 * math.pi * jnp.arange(length, dtype=F32) / length
    tw = jnp.stack([t, w], axis=1)
    f = jnp.linspace(1e-4, bands - 1, bands, dtype=F32).reshape(1, bands)
    min_decay = math.log(HY_TARGET) / HY_SLOW_PCT
    max_decay = math.log(HY_TARGET) / HY_FAST_PCT
    deltas = jnp.abs(jnp.linspace(min_decay, max_decay, D_MODEL, dtype=F32))
    dl = jnp.concatenate([deltas, deltas]).reshape(1, 2 * D_MODEL)
    tn = 1024
    small = lambda shape: pl.BlockSpec(shape, lambda j: (0, 0))
    row = lambda v: v.reshape(1, -1)
    return pl.pallas_call(
        _hy_filter_kernel,
        grid=(2 * D_MODEL // tn,),
        in_specs=[small((length, 2)), small((1, bands)), small((1, fw)), small((bands, fw)), small((bands, fw)), small((1, fw)),
                  small((fw, fw)), small((1, fw)), small((fw, fw)), small((1, fw)), small((1, fw)),
                  pl.BlockSpec((fw, tn), lambda j: (0, j)), pl.BlockSpec((1, tn), lambda j: (0, j))],
        out_specs=pl.BlockSpec((length, tn), lambda j: (0, j)),
        out_shape=jax.ShapeDtypeStruct((length, 2 * D_MODEL), F32),
        scratch_shapes=[pltpu.VMEM((length, fw), F32)],
        compiler_params=_cparams(("arbitrary",)),
        name="hyena_filters",
    )(tw, f, w1[0:1], w1[1:1 + bands], w1[1 + bands:], row(b1), w2, row(b2), w3, row(b3), row(freq), w4, dl)


def _hy_conv_kernel(x0_ref, x1_ref, v_ref, w0, b0, w1, b1, w2, b2, x0_out, u_out):
    length = x0_ref.shape[1]

    def conv(raw_ref, w_ref, b_ref):
        x = raw_ref[0].astype(F32)
        rows = lax.broadcasted_iota(jnp.int32, x.shape, 0)
        xm1 = jnp.where(rows == 0, 0.0, pltpu.roll(x, 1, 0))
        xp1 = jnp.where(rows == length - 1, 0.0, pltpu.roll(x, length - 1, 0))
        return b_ref[...] + xm1 * w_ref[0:1, :] + x * w_ref[1:2, :] + xp1 * w_ref[2:3, :]

    x0_out[0] = conv(x0_ref, w0, b0).astype(x0_out.dtype)
    u_out[0] = (conv(v_ref, w2, b2) * conv(x1_ref, w1, b1)).astype(u_out.dtype)


def _hyena_conv(proj, conv_w, conv_b):
    b, length, _ = proj.shape
    d = D_MODEL
    tn = 512
    nb = d // tn
    cb = conv_b.reshape(1, -1)
    seg = lambda k: pl.BlockSpec((1, length, tn), lambda bi, j: (bi, 0, k * nb + j))
    wseg = lambda k: pl.BlockSpec((3, tn), lambda bi, j: (0, k * nb + j))
    bseg = lambda k: pl.BlockSpec((1, tn), lambda bi, j: (0, k * nb + j))
    out = pl.BlockSpec((1, length, tn), lambda bi, j: (bi, 0, j))
    return pl.pallas_call(
        _hy_conv_kernel,
        grid=(b, nb),
        in_specs=[seg(0), seg(1), seg(2), wseg(0), bseg(0), wseg(1), bseg(1), wseg(2), bseg(2)],
        out_specs=[out, out],
        out_shape=[jax.ShapeDtypeStruct((b, length, d), BF16)] * 2,
        compiler_params=_cparams(("parallel", "parallel")),
        name="hyena_conv",
    )(proj, proj, proj, conv_w, cb, conv_w, cb, conv_w, cb)


def _dft_matrices(length):
    n = 2 * length
    f = lax.broadcasted_iota(jnp.int32, (length, length), 0)
    t = lax.broadcasted_iota(jnp.int32, (length, length), 1)
    ang = ((f * t) % n).astype(F32) * (2 * math.pi / n)
    sign = (1 - 2 * (t % 2)).astype(F32)
    fwd_c = jnp.cos(ang)
    fwd_s = jnp.where(f == 0, sign, -jnp.sin(ang))
    fwd = jnp.concatenate([fwd_c, fwd_s], axis=0).astype(BF16)
    wgt = jnp.where(f == 0, 1.0, 2.0) / n
    inv = jnp.concatenate([(fwd_c * wgt).T, (fwd_s * wgt).T], axis=1).astype(BF16)
    return fwd, inv


def _mm_kernel(a_ref, b_ref, o_ref):
    o_ref[...] = _dot(a_ref[...], b_ref[...]).astype(o_ref.dtype)


def _matmul(a, b, out_dtype=F32, tm=1024, tn=1024):
    m, k = a.shape
    n = b.shape[1]
    return pl.pallas_call(
        _mm_kernel,
        grid=(n // tn, m // tm),
        in_specs=[pl.BlockSpec((tm, k), lambda j, i: (i, 0)), pl.BlockSpec((k, tn), lambda j, i: (0, j))],
        out_specs=pl.BlockSpec((tm, tn), lambda j, i: (i, j)),
        out_shape=jax.ShapeDtypeStruct((m, n), out_dtype),
        compiler_params=_cparams(("parallel", "parallel")),
        name="matmul",
    )(a, b)


def _filter_spectrum(kk, fwd):
    length = kk.shape[0]
    d = D_MODEL
    k_f, k_b = kk[:, :d], kk[:, d:]
    k_fp = k_f.at[0].add(k_b[0])
    k_bp = k_b.at[0].set(0.0)
    spec = _matmul(fwd, jnp.concatenate([k_fp, k_bp], axis=1).astype(BF16))
    a_re, a_im, b_re, b_im = spec[:length, :d], spec[length:, :d], spec[:length, d:], spec[length:, d:]
    nz = (jnp.arange(length) != 0)[:, None]
    k_re = a_re + b_re
    k_im = jnp.where(nz, a_im - b_im, a_im + b_im)
    return jnp.stack([k_re, jnp.where(nz, k_im, 0.0), jnp.where(nz, k_re, k_im)])


def _dft_fwd_kernel(fc_ref, fs_ref, u_ref, k_ref, y_ref):
    u = u_ref[0]
    re = _dot(fc_ref[...], u)
    im = _dot(fs_ref[...], u)
    y_ref[0, 0] = (re * k_ref[0] - im * k_ref[1]).astype(y_ref.dtype)
    y_ref[0, 1] = (re * k_ref[1] + im * k_ref[2]).astype(y_ref.dtype)


def _dft_forward(u, fwd, coef, tm=1024, tn=512):
    b, length, d = u.shape
    ni = length // tm
    return pl.pallas_call(
        _dft_fwd_kernel,
        grid=(d // tn, ni, b),
        in_specs=[
            pl.BlockSpec((tm, length), lambda j, i, bi: (i, 0)),
            pl.BlockSpec((tm, length), lambda j, i, bi: (ni + i, 0)),
            pl.BlockSpec((1, length, tn), lambda j, i, bi: (bi, 0, j)),
            pl.BlockSpec((3, tm, tn), lambda j, i, bi: (0, i, j)),
        ],
        out_specs=pl.BlockSpec((1, 2, tm, tn), lambda j, i, bi: (bi, 0, i, j)),
        out_shape=jax.ShapeDtypeStruct((b, 2, length, d), BF16),
        compiler_params=_cparams(("parallel", "parallel", "parallel")),
        name="dft_forward",
    )(fwd, fwd, u, coef)


def _dft_inv_kernel(g_ref, y_ref, x0_ref, u_ref, fb_ref, o_ref):
    conv = _dot(g_ref[...], y_ref[0])
    o_ref[0] = (x0_ref[0].astype(F32) * (conv + u_ref[0].astype(F32) * fb_ref[...])).astype(o_ref.dtype)


def _dft_inverse(y, inv, x0, u, filt_bias, tm=512, tn=512):
    b, n2, d = y.shape
    length = n2 // 2
    return pl.pallas_call(
        _dft_inv_kernel,
        grid=(d // tn, length // tm, b),
        in_specs=[
            pl.BlockSpec((tm, n2), lambda j, i, bi: (i, 0)),
            pl.BlockSpec((1, n2, tn), lambda j, i, bi: (bi, 0, j)),
            pl.BlockSpec((1, tm, tn), lambda j, i, bi: (bi, i, j)),
            pl.BlockSpec((1, tm, tn), lambda j, i, bi: (bi, i, j)),
            pl.BlockSpec((1, tn), lambda j, i, bi: (0, j)),
        ],
        out_specs=pl.BlockSpec((1, tm, tn), lambda j, i, bi: (bi, i, j)),
        out_shape=jax.ShapeDtypeStruct((b, length, d), BF16),
        compiler_params=_cparams(("parallel", "parallel", "parallel")),
        name="dft_inverse",
    )(inv, y, x0, u, filt_bias.reshape(1, d))


def _router_kernel(x_ref, g_ref, sh_ref, sc_ref, wr_ref, br_ref, hp_ref, idx_ref, gate_ref, rank_ref, cnt_ref, run):
    tm = x_ref.shape[1]
    half = x_ref.shape[2] // 2
    first = (pl.program_id(0) == 0) & (pl.program_id(1) == 0)

    @pl.when(first)
    def _():
        run[...] = jnp.zeros_like(run)

    x = x_ref[0]
    ms = jnp.mean(x * x, axis=-1, keepdims=True)
    h = x * lax.rsqrt(ms + EPS) * g_ref[...] * (1.0 + sc_ref[0]) + sh_ref[0]
    hb = pltpu.bitcast(h.astype(BF16).astype(F32), jnp.uint32)
    hp_ref[0] = (hb[:, :half] >> 16) | (hb[:, half:] & jnp.uint32(0xFFFF0000))

    logits = jnp.dot(h, wr_ref[...], precision=HI, preferred_element_type=F32) + br_ref[...]
    lane_i = lax.broadcasted_iota(jnp.int32, logits.shape, 1)
    lane = lane_i.astype(F32)
    work = jnp.where(lane_i < N_EXPERTS, logits, -jnp.inf)
    li = lax.broadcasted_iota(jnp.int32, (tm, tm), 0)
    ki = lax.broadcasted_iota(jnp.int32, (tm, tm), 1)
    strict_lower = (ki < li).astype(BF16)
    sels, tops = [], []
    for _ in range(TOP_K):
        m = jnp.max(work, axis=-1, keepdims=True)
        first_idx = jnp.min(jnp.where(work == m, lane, float(LANES)), axis=-1, keepdims=True)
        sel = lane == first_idx
        sels.append(sel)
        tops.append((m, first_idx))
        work = jnp.where(sel, -jnp.inf, work)
    chosen = sels[0] | sels[1] | sels[2] | sels[3]
    before = _dot(strict_lower, chosen.astype(BF16)) + run[...]
    run[...] = run[...] + jnp.sum(chosen.astype(F32), axis=0, keepdims=True)
    den = sum(jnp.exp(m - tops[0][0]) for m, _ in tops)
    idx_o = jnp.zeros(logits.shape, jnp.int32)
    gate_o = jnp.zeros(logits.shape, F32)
    rank_o = jnp.zeros(logits.shape, jnp.int32)
    for k, (sel, (m, first_idx)) in enumerate(zip(sels, tops)):
        rank = jnp.sum(jnp.where(sel, before, 0.0), axis=-1, keepdims=True).astype(jnp.int32)
        idx_o = jnp.where(lane_i == k, first_idx.astype(jnp.int32), idx_o)
        gate_o = jnp.where(lane_i == k, jnp.exp(m - tops[0][0]) / den, gate_o)
        rank_o = jnp.where(lane_i == k, rank, rank_o)
    idx_ref[0] = idx_o
    gate_ref[0] = gate_o
    rank_ref[0] = rank_o
    cnt_ref[...] = run[...]


def _moe_route(x, g, shift, scale, w_r, b_r, tm=512):
    b, s, d = x.shape
    wr = jnp.zeros((d, LANES), F32).at[:, :N_EXPERTS].set(w_r)
    br = jnp.zeros((1, LANES), F32).at[0, :N_EXPERTS].set(b_r)
    tok = lambda n: pl.BlockSpec((1, tm, n), lambda bi, i: (bi, i, 0))
    return pl.pallas_call(
        _router_kernel,
        grid=(b, s // tm),
        in_specs=[
            tok(d),
            pl.BlockSpec((1, d), lambda bi, i: (0, 0)),
            pl.BlockSpec((1, 1, d), lambda bi, i: (bi, 0, 0)),
            pl.BlockSpec((1, 1, d), lambda bi, i: (bi, 0, 0)),
            pl.BlockSpec((d, LANES), lambda bi, i: (0, 0)),
            pl.BlockSpec((1, LANES), lambda bi, i: (0, 0)),
        ],
        out_specs=[tok(d // 2), tok(LANES), tok(LANES), tok(LANES), pl.BlockSpec((1, LANES), lambda bi, i: (0, 0))],
        out_shape=[
            jax.ShapeDtypeStruct((b, s, d // 2), jnp.uint32),
            jax.ShapeDtypeStruct((b, s, LANES), jnp.int32),
            jax.ShapeDtypeStruct((b, s, LANES), F32),
            jax.ShapeDtypeStruct((b, s, LANES), jnp.int32),
            jax.ShapeDtypeStruct((1, LANES), F32),
        ],
        scratch_shapes=[pltpu.VMEM((1, LANES), F32)],
        compiler_params=_cparams(("arbitrary", "arbitrary")),
        name="moe_route",
    )(x, g.reshape(1, d), shift.reshape(b, 1, d), scale.reshape(b, 1, d), wr, br)


def _gather_kernel(idx_ref, src_ref, o_ref, sem):
    rows = o_ref.shape[0]

    def issue(r, carry):
        pltpu.make_async_copy(src_ref.at[pl.ds(idx_ref[0, 0, r], 1), :], o_ref.at[pl.ds(r, 1), :], sem).start()
        return carry

    lax.fori_loop(0, rows, issue, 0)
    pltpu.make_async_copy(src_ref.at[pl.ds(0, rows), :], o_ref, sem).wait()


def _gather_rows(src, idx, rows):
    n_out = idx.shape[0]
    w = src.shape[1]
    nb = n_out // rows
    return pl.pallas_call(
        _gather_kernel,
        grid=(nb,),
        in_specs=[
            pl.BlockSpec((1, 1, rows), lambda i: (i, 0, 0), memory_space=pltpu.SMEM),
            pl.BlockSpec(memory_space=pl.ANY),
        ],
        out_specs=pl.BlockSpec((rows, w), lambda i: (i, 0)),
        out_shape=jax.ShapeDtypeStruct((n_out, w), src.dtype),
        scratch_shapes=[pltpu.SemaphoreType.DMA(())],
        compiler_params=_cparams(("arbitrary",)),
        name="gather_rows",
    )(idx.reshape(nb, 1, rows), src)


def _ffn_kernel(be_ref, na_ref, x_ref, w1g_ref, w1u_ref, b1g_ref, b1u_ref, w2_ref, b2_ref, o_ref, xb):
    i = pl.program_id(0)
    f = pl.program_id(1)
    active = i < na_ref[0]
    half = x_ref.shape[1]

    @pl.when(f == 0)
    def _():
        xp = x_ref[...]
        xb[:, :half] = pltpu.bitcast(xp << 16, F32).astype(BF16)
        xb[:, half:] = pltpu.bitcast(xp & jnp.uint32(0xFFFF0000), F32).astype(BF16)
        o_ref[...] = jnp.broadcast_to(b2_ref[0, 0], o_ref.shape)

    @pl.when(active)
    def _():
        x = xb[...]
        g = _dot(x, w1g_ref[0, 0].astype(BF16)) + b1g_ref[0, 0]
        u = _dot(x, w1u_ref[0, 0].astype(BF16)) + b1u_ref[0, 0]
        g = jnp.minimum(g, SWIGLU_LIMIT)
        u = jnp.clip(u, -SWIGLU_LIMIT, SWIGLU_LIMIT)
        a = (u + 1.0) * g / (1.0 + jnp.exp(-SWIGLU_ALPHA * g))
        o_ref[...] += _dot(a.astype(BF16), w2_ref[0, 0].astype(BF16))


def _moe_experts(layer, xs_packed, block_e, n_active, w1, b1, w2, b2):
    n_pad, half = xs_packed.shape
    d = 2 * half
    tm, tf = MOE_ROWS, MOE_FF_TILE
    nf = MOE_FF // tf
    nb = n_pad // tm

    def blk(i, na):
        return jnp.minimum(i, na[0] - 1)

    def ff(i, f, na):
        return jnp.where(i < na[0], f, nf - 1)

    grid_spec = pltpu.PrefetchScalarGridSpec(
        num_scalar_prefetch=2,
        grid=(nb, nf),
        in_specs=[
            pl.BlockSpec((tm, half), lambda i, f, be, na: (blk(i, na), 0)),
            pl.BlockSpec((1, 1, d, tf), lambda i, f, be, na: (layer, be[blk(i, na)], 0, ff(i, f, na))),
            pl.BlockSpec((1, 1, d, tf), lambda i, f, be, na: (layer, be[blk(i, na)], 0, nf + ff(i, f, na))),
            pl.BlockSpec((1, 1, 1, tf), lambda i, f, be, na: (layer, be[blk(i, na)], 0, ff(i, f, na))),
            pl.BlockSpec((1, 1, 1, tf), lambda i, f, be, na: (layer, be[blk(i, na)], 0, nf + ff(i, f, na))),
            pl.BlockSpec((1, 1, tf, d), lambda i, f, be, na: (layer, be[blk(i, na)], ff(i, f, na), 0)),
            pl.BlockSpec((1, 1, 1, d), lambda i, f, be, na: (layer, be[blk(i, na)], 0, 0)),
        ],
        out_specs=pl.BlockSpec((tm, d), lambda i, f, be, na: (i, 0)),
        scratch_shapes=[pltpu.VMEM((tm, d), BF16)],
    )
    depth, ne = w1.shape[:2]
    return pl.pallas_call(
        _ffn_kernel,
        grid_spec=grid_spec,
        out_shape=jax.ShapeDtypeStruct((n_pad, d), F32),
        compiler_params=_cparams(("arbitrary", "arbitrary")),
        name="moe_experts",
    )(block_e, n_active, xs_packed, w1, w1, b1.reshape(depth, ne, 1, -1), b1.reshape(depth, ne, 1, -1), w2, b2.reshape(depth, ne, 1, -1))


def _combine_kernel(pos_ref, y_ref, gate_ref, x_ref, gm_ref, o_ref, buf, sem):
    tc = x_ref.shape[1]

    def issue(r, carry):
        pltpu.make_async_copy(y_ref.at[pl.ds(pos_ref[0, 0, r], 1), :], buf.at[pl.ds(r, 1), :], sem).start()
        return carry

    lax.fori_loop(0, TOP_K * tc, issue, 0)
    pltpu.make_async_copy(y_ref.at[pl.ds(0, TOP_K * tc), :], buf, sem).wait()
    gates = gate_ref[0]
    acc = jnp.zeros((tc, x_ref.shape[2]), F32)
    for k in range(TOP_K):
        acc = acc + gates[:, k:k + 1] * buf[k * tc:(k + 1) * tc, :]
    o_ref[0] = x_ref[0] + gm_ref[0] * acc


def _moe_combine(y_sorted, pos, gates, x, gmod, tc=128):
    b, s, d = x.shape
    nt = s // tc
    pos_blk = pos.reshape(b * nt, tc, TOP_K).transpose(0, 2, 1).reshape(b * nt, 1, TOP_K * tc)
    return pl.pallas_call(
        _combine_kernel,
        grid=(b, nt),
        in_specs=[
            pl.BlockSpec((1, 1, TOP_K * tc), lambda bi, i: (bi * nt + i, 0, 0), memory_space=pltpu.SMEM),
            pl.BlockSpec(memory_space=pl.ANY),
            pl.BlockSpec((1, tc, LANES), lambda bi, i: (bi, i, 0)),
            pl.BlockSpec((1, tc, d), lambda bi, i: (bi, i, 0)),
            pl.BlockSpec((1, 1, d), lambda bi, i: (bi, 0, 0)),
        ],
        out_specs=pl.BlockSpec((1, tc, d), lambda bi, i: (bi, i, 0)),
        out_shape=jax.ShapeDtypeStruct((b, s, d), F32),
        scratch_shapes=[pltpu.VMEM((TOP_K * tc, d), F32), pltpu.SemaphoreType.DMA(())],
        compiler_params=_cparams(("arbitrary", "arbitrary")),
        name="moe_combine",
    )(pos_blk, y_sorted, gates, x, gmod.reshape(b, 1, d))


def _moe_ffn(layer, x, g, shift, scale, gmod, w_r, b_r, w1, b1, w2, b2):
    b, s, d = x.shape
    n = b * s
    hp, idx, gates, rank, counts = _moe_route(x, g, shift, scale, w_r, b_r)
    counts = counts[0, :N_EXPERTS].astype(jnp.int32)
    padded = (counts + MOE_ROWS - 1) // MOE_ROWS * MOE_ROWS
    pad_end = jnp.cumsum(padded)
    pad_start = pad_end - padded
    idx4 = idx.reshape(n, LANES)[:, :TOP_K]
    pos = pad_start[idx4] + rank.reshape(n, LANES)[:, :TOP_K]
    n_blocks = n * TOP_K // MOE_ROWS + N_EXPERTS
    n_pad = n_blocks * MOE_ROWS
    tok_ids = jnp.broadcast_to(jnp.arange(n, dtype=jnp.int32)[:, None], (n, TOP_K))
    slot_tok = jnp.zeros((n_pad,), jnp.int32).at[pos.reshape(-1)].set(tok_ids.reshape(-1))
    block_e = jnp.minimum(jnp.searchsorted(pad_end, jnp.arange(n_blocks, dtype=jnp.int32) * MOE_ROWS, side='right'),
                          N_EXPERTS - 1).astype(jnp.int32)
    n_active = (pad_end[-1:] // MOE_ROWS).astype(jnp.int32)
    xs = _gather_rows(hp.reshape(n, d // 2), slot_tok, MOE_ROWS)
    ys = _moe_experts(layer, xs, block_e, n_active, w1, b1, w2, b2)
    return _moe_combine(ys, pos.astype(jnp.int32), gates, x, gmod)


def kernel(x, c, ctx, c_ctx, ada_w, ada_b, norm_mix, norm_ffn, ev_w_in, ev_conv_w, ev_conv_b, ev_q_norm, ev_k_norm, ev_rpb, ev_a_log, ev_dt_bias, ev_d_skip, ev_gate_norm, ev_w_out, od_w_in, od_b_in, od_conv_w, od_conv_b, od_filt_w1, od_filt_b1, od_filt_w2, od_filt_b2, od_filt_w3, od_filt_b3, od_filt_w4, od_filt_freq, od_filt_bias, od_w_out, od_b_out, moe_router_w, moe_router_b, moe_w1, moe_b1, moe_w2, moe_b2):
    b, s, d = x.shape
    assert ada_w.shape[0] == 2 and ev_w_in.shape[0] == 1 and od_w_in.shape[0] == 1 and d == D_MODEL
    cc = jnp.zeros((16, d), F32).at[:b].set(c).at[b].set(c_ctx)
    mod = _ada_mod(cc, ada_w, ada_b)
    chunk = lambda m, k: m[:, k * d:(k + 1) * d]
    moe = (moe_w1, moe_b1, moe_w2, moe_b2)

    m0 = mod[0, :b]
    mc0 = jnp.broadcast_to(mod[0, b:b + 1], (b, N_MOD * d))
    w_in = ev_w_in[0]
    n_main = 2 * NA_WIDTH + SSM_D_INNER + NA_WIDTH + SSM_XBC
    kv0 = NA_WIDTH + SSM_D_INNER
    w_main = w_in[:, :n_main].astype(BF16)
    w_ctx = w_in[:, kv0:n_main].astype(BF16)
    w_dt = jnp.zeros((d, LANES), BF16).at[:, :2 * SSM_HEADS].set(w_in[:, n_main:].astype(BF16))
    zb = lambda n: jnp.zeros((n,), F32)
    g0 = norm_mix[0]
    proj = _norm_mod_matmul(x, g0, chunk(m0, 0), chunk(m0, 1), w_main, zb(n_main), BF16, 1024, 1024)
    dt_raw = _norm_mod_matmul(x, g0, chunk(m0, 0), chunk(m0, 1), w_dt, zb(LANES), F32, 1024, LANES)[..., :2 * SSM_HEADS]
    proj_c = _norm_mod_matmul(ctx, g0, chunk(mc0, 0), chunk(mc0, 1), w_ctx, zb(n_main - kv0), BF16, 256, 1024)
    dt_raw_c = _norm_mod_matmul(ctx, g0, chunk(mc0, 0), chunk(mc0, 1), w_dt, zb(LANES), F32, 256, LANES)[..., :2 * SSM_HEADS]
    attn = _attention(proj, proj_c, ev_q_norm[0], ev_k_norm[0], _attn_bias_table(ev_rpb[0]))
    ssm_args = (ev_conv_w[0], ev_conv_b[0], ev_dt_bias[0], ev_a_log[0])
    x_blk_c = 2 * NA_WIDTH // GROUP_WIDTH
    states = _ssd(proj_c, x_blk_c, None, dt_raw_c, *ssm_args, None, None, None)
    x_blk = (kv0 + 2 * NA_WIDTH) // GROUP_WIDTH
    z_blk = NA_WIDTH // GROUP_WIDTH
    y_ssm = _ssd(proj, x_blk, z_blk, dt_raw, *ssm_args, ev_d_skip[0], ev_gate_norm[0], states)
    x = _proj_residual([attn, y_ssm], ev_w_out[0].astype(BF16), zb(d), chunk(m0, 2), x)
    x = _moe_ffn(0, x, norm_ffn[0], chunk(m0, 3), chunk(m0, 4), chunk(m0, 5), moe_router_w[0], moe_router_b[0], *moe)

    m1 = mod[1, :b]
    proj_h = _norm_mod_matmul(x, norm_mix[1], chunk(m1, 0), chunk(m1, 1), od_w_in[0].astype(BF16), od_b_in[0], BF16, 1024, 1024)
    filt = _hyena_filters(s, od_filt_w1[0], od_filt_b1[0], od_filt_w2[0], od_filt_b2[0], od_filt_w3[0], od_filt_b3[0],
                          od_filt_w4[0], od_filt_freq[0])
    fwd, inv = _dft_matrices(s)
    coef = _filter_spectrum(filt, fwd)
    x0, u = _hyena_conv(proj_h, od_conv_w[0], od_conv_b[0])
    spec = _dft_forward(u, fwd, coef)
    y_h = _dft_inverse(spec.reshape(b, 2 * s, d), inv, x0, u, od_filt_bias[0])
    x = _proj_residual([y_h], od_w_out[0].astype(BF16), od_b_out[0], chunk(m1, 2), x)
    x = _moe_ffn(1, x, norm_ffn[1], chunk(m1, 3), chunk(m1, 4), chunk(m1, 5), moe_router_w[1], moe_router_b[1], *moe)
    return x
```

```python
import functools
import math

import jax
import jax.numpy as jnp
import numpy as np
from jax import lax
from jax.experimental import pallas as pl
from jax.experimental.pallas import tpu as pltpu

F32 = jnp.float32
BF16 = jnp.bfloat16
HI = lax.Precision.HIGHEST

D_MODEL = 2048
N_MOD = 6
EPS = 1e-6
NEG_INF = -1e9
GRID_W = 64
NA_HEADS = 16
NA_HEAD_DIM = 128
NA_WIDTH = NA_HEADS * NA_HEAD_DIM
NA_KH = 8
NA_KW = 16
SSM_D_INNER = 2048
SSM_HEAD_DIM = 64
SSM_HEADS = SSM_D_INNER // SSM_HEAD_DIM
SSM_GROUPS = 8
SSM_STATE = 128
SSM_CHUNK = 128
SSM_XBC = SSM_D_INNER + 2 * SSM_GROUPS * SSM_STATE
HEADS_PER_GROUP = SSM_HEADS // SSM_GROUPS
GROUP_WIDTH = HEADS_PER_GROUP * SSM_HEAD_DIM
HY_EMB = 33
HY_FILTER_WIDTH = 64
HY_TARGET = 1e-2
HY_FAST_PCT = 0.3
HY_SLOW_PCT = 1.5
N_EXPERTS = 32
TOP_K = 4
MOE_FF = 2048
SWIGLU_LIMIT = 7.0
SWIGLU_ALPHA = 1.702
MOE_ROWS = 512
MOE_FF_TILE = 512
LANES = 128
VMEM_LIMIT = 56 * 1024 * 1024


def _cparams(sem, vmem=VMEM_LIMIT):
    return pltpu.CompilerParams(dimension_semantics=sem, vmem_limit_bytes=vmem)


def _dot(a, b):
    return jnp.dot(a, b, preferred_element_type=F32)


def _dot_nt(a, b):
    return lax.dot_general(a, b, (((1,), (1,)), ((), ())), preferred_element_type=F32)


def _dot_tn(a, b):
    return lax.dot_general(a, b, (((0,), (0,)), ((), ())), preferred_element_type=F32)


def _silu(x):
    return x / (1.0 + jnp.exp(-x))


def _softplus(x):
    return jnp.maximum(x, 0.0) + jnp.log1p(jnp.exp(-jnp.abs(x)))


def _ada_kernel(c_ref, w_ref, b_ref, o_ref):
    sc = _silu(c_ref[...])
    hi = sc.astype(BF16)
    lo = (sc - hi.astype(F32)).astype(BF16)
    w = w_ref[0].astype(BF16)
    o_ref[0] = _dot(hi, w) + _dot(lo, w) + b_ref[0]


def _ada_mod(cc, ada_w, ada_b):
    depth, d, n = ada_w.shape
    tn = 1024
    return pl.pallas_call(
        _ada_kernel,
        grid=(depth, n // tn),
        in_specs=[
            pl.BlockSpec((cc.shape[0], d), lambda i, j: (0, 0)),
            pl.BlockSpec((1, d, tn), lambda i, j: (i, 0, j)),
            pl.BlockSpec((1, 1, tn), lambda i, j: (i, 0, j)),
        ],
        out_specs=pl.BlockSpec((1, cc.shape[0], tn), lambda i, j: (i, 0, j)),
        out_shape=jax.ShapeDtypeStruct((depth, cc.shape[0], n), F32),
        compiler_params=_cparams(("parallel", "parallel")),
        name="ada_mod",
    )(cc, ada_w, ada_b.reshape(depth, 1, n))


def _nmm_kernel(x_ref, g_ref, sh_ref, sc_ref, w_ref, b_ref, o_ref, h_scr):
    @pl.when(pl.program_id(2) == 0)
    def _():
        x = x_ref[0]
        ms = jnp.mean(x * x, axis=-1, keepdims=True)
        xn = x * lax.rsqrt(ms + EPS) * g_ref[...]
        h_scr[...] = (xn * (1.0 + sc_ref[0]) + sh_ref[0]).astype(BF16)

    o_ref[0] = (_dot(h_scr[...], w_ref[...]) + b_ref[...]).astype(o_ref.dtype)


def _norm_mod_matmul(x, g, shift, scale, w, bias, out_dtype, tm, tn):
    b, s, d = x.shape
    n = w.shape[1]
    tm = min(tm, s)
    return pl.pallas_call(
        _nmm_kernel,
        grid=(b, s // tm, n // tn),
        in_specs=[
            pl.BlockSpec((1, tm, d), lambda bi, i, j: (bi, i, 0)),
            pl.BlockSpec((1, d), lambda bi, i, j: (0, 0)),
            pl.BlockSpec((1, 1, d), lambda bi, i, j: (bi, 0, 0)),
            pl.BlockSpec((1, 1, d), lambda bi, i, j: (bi, 0, 0)),
            pl.BlockSpec((d, tn), lambda bi, i, j: (0, j)),
            pl.BlockSpec((1, tn), lambda bi, i, j: (0, j)),
        ],
        out_specs=pl.BlockSpec((1, tm, tn), lambda bi, i, j: (bi, i, j)),
        out_shape=jax.ShapeDtypeStruct((b, s, n), out_dtype),
        scratch_shapes=[pltpu.VMEM((tm, d), BF16)],
        compiler_params=_cparams(("parallel", "parallel", "arbitrary")),
        name="norm_mod_matmul",
    )(x, g.reshape(1, d), shift.reshape(b, 1, d), scale.reshape(b, 1, d), w, bias.reshape(1, n))


NA_PAIR_ROWS = NA_KH + 2
NA_VARIANTS = 5


def _pair_window_start(i, rows):
    return np.clip(2 * i - NA_KH // 2, 0, rows - NA_PAIR_ROWS)


def _attn_bias_table(rpb, rows):
    nh = rpb.shape[0]
    kc = np.arange(GRID_W)[:, None]
    qc = np.arange(GRID_W)[None, :]
    col_off = np.clip(kc - qc, -(NA_KW - 1), NA_KW - 1) + NA_KW - 1
    onehot = (col_off[None] == np.arange(2 * NA_KW - 1)[:, None, None]).astype(np.float32)
    toep = jnp.einsum('hrc,ckq->hrkq', rpb.astype(F32), onehot, precision=HI)
    ws_col = np.clip(qc - NA_KW // 2, 0, GRID_W - NA_KW)
    toep = jnp.where(((kc >= ws_col) & (kc < ws_col + NA_KW))[None, None], toep, NEG_INF)
    dead = jnp.full((nh, GRID_W, GRID_W), NEG_INF, F32)
    rep_pair = [0, 1, 2, rows // 2 - 2, rows // 2 - 1]
    blocks = []
    for i in rep_pair:
        ws = int(_pair_window_start(i, rows))
        for t in range(NA_PAIR_ROWS):
            for e in range(2):
                r = 2 * i + e
                rs = int(np.clip(r - NA_KH // 2, 0, rows - NA_KH))
                kr = ws + t
                blocks.append(toep[:, kr - r + NA_KH - 1] if rs <= kr < rs + NA_KH else dead)
    bias = jnp.stack(blocks, axis=1).reshape(nh, NA_VARIANTS, NA_PAIR_ROWS, 2, GRID_W, GRID_W)
    return bias.transpose(0, 1, 2, 4, 3, 5).reshape(nh, NA_VARIANTS, NA_PAIR_ROWS * GRID_W, 2 * GRID_W)


def _head_rmsnorm(t, g):
    sq = t * t
    hi = sq.astype(BF16)
    lo = (sq - hi.astype(F32)).astype(BF16)
    avg = jnp.full((NA_HEAD_DIM, NA_HEAD_DIM), 1.0 / NA_HEAD_DIM, BF16)
    ms = _dot(hi, avg) + _dot(lo, avg)
    return t * lax.rsqrt(ms + EPS) * g


def _attn_kernel(q_ref, k_ref, v_ref, kc_ref, vc_ref, qg_ref, kg_ref, bias_ref, o_ref, qs, ks, kcs, vt, vct):
    seq, hd = q_ref.shape[1], q_ref.shape[2]
    rows = seq // GRID_W
    pair = 2 * GRID_W
    win_blocks = NA_PAIR_ROWS // 2
    ctx_blocks = kc_ref.shape[1] // pair
    qs[...] = (_head_rmsnorm(q_ref[0].astype(F32), qg_ref[...]) * hd ** -0.5).astype(BF16)
    ks[...] = _head_rmsnorm(k_ref[0].astype(F32), kg_ref[...]).astype(BF16)
    kcs[...] = _head_rmsnorm(kc_ref[0].astype(F32), kg_ref[...]).astype(BF16)
    for blk in range(seq // pair):
        vt[blk] = v_ref[0, blk * pair:(blk + 1) * pair, :].astype(F32).T.astype(BF16)
    for blk in range(ctx_blocks):
        vct[blk] = vc_ref[0, blk * pair:(blk + 1) * pair, :].astype(F32).T.astype(BF16)

    def body(i, carry):
        ws = jnp.clip(2 * i - NA_KH // 2, 0, rows - NA_PAIR_ROWS)
        var = (2 * i - ws) // 2
        wb = ws // 2
        q0 = pl.multiple_of(i * pair, pair)
        k0 = pl.multiple_of(ws * GRID_W, pair)
        q_p = qs[pl.ds(q0, pair), :]
        s_lat = _dot_nt(ks[pl.ds(k0, win_blocks * pair), :], q_p) + bias_ref[0, var]
        s_ctx = _dot_nt(kcs[...], q_p)
        m = jnp.maximum(jnp.max(s_lat, axis=0, keepdims=True), jnp.max(s_ctx, axis=0, keepdims=True))
        p_lat = jnp.exp(s_lat - m)
        p_ctx = jnp.exp(s_ctx - m)
        den = jnp.sum(p_lat, axis=0, keepdims=True) + jnp.sum(p_ctx, axis=0, keepdims=True)
        p_lat = p_lat.astype(BF16)
        p_ctx = p_ctx.astype(BF16)
        o_t = jnp.zeros((hd, pair), F32)
        for blk in range(win_blocks):
            o_t = o_t + _dot(vt[wb + blk], p_lat[blk * pair:(blk + 1) * pair, :])
        for blk in range(ctx_blocks):
            o_t = o_t + _dot(vct[blk], p_ctx[blk * pair:(blk + 1) * pair, :])
        o_ref[0, pl.ds(q0, pair), :] = (o_t / den).T.astype(o_ref.dtype)
        return carry

    lax.fori_loop(0, rows // 2, body, 0, unroll=2)


def _attention(proj, proj_c, q_norm, k_norm, bias_tab):
    b, s, _ = proj.shape
    ctx = proj_c.shape[1]
    hd = NA_HEAD_DIM
    nh = NA_HEADS
    return pl.pallas_call(
        _attn_kernel,
        grid=(b, nh),
        in_specs=[
            pl.BlockSpec((1, s, hd), lambda bi, h: (bi, 0, h)),
            pl.BlockSpec((1, s, hd), lambda bi, h: (bi, 0, 2 * nh + h)),
            pl.BlockSpec((1, s, hd), lambda bi, h: (bi, 0, 3 * nh + h)),
            pl.BlockSpec((1, ctx, hd), lambda bi, h: (bi, 0, h)),
            pl.BlockSpec((1, ctx, hd), lambda bi, h: (bi, 0, nh + h)),
            pl.BlockSpec((1, hd), lambda bi, h: (0, 0)),
            pl.BlockSpec((1, hd), lambda bi, h: (0, 0)),
            pl.BlockSpec((1,) + bias_tab.shape[1:], lambda bi, h: (h, 0, 0, 0)),
        ],
        out_specs=pl.BlockSpec((1, s, hd), lambda bi, h: (bi, 0, h)),
        out_shape=jax.ShapeDtypeStruct((b, s, nh * hd), BF16),
        scratch_shapes=[pltpu.VMEM((s, hd), BF16), pltpu.VMEM((s, hd), BF16), pltpu.VMEM((ctx, hd), BF16),
                        pltpu.VMEM((s // (2 * GRID_W), hd, 2 * GRID_W), BF16), pltpu.VMEM((ctx // (2 * GRID_W), hd, 2 * GRID_W), BF16)],
        compiler_params=_cparams(("parallel", "parallel")),
        name="nbr_attention",
    )(proj, proj, proj, proj_c, proj_c, q_norm.reshape(1, hd), k_norm.reshape(1, hd), bias_tab)


def _split3(v):
    hi = v.astype(BF16)
    r1 = v - hi.astype(F32)
    mid = r1.astype(BF16)
    lo = (r1 - mid.astype(F32)).astype(BF16)
    return hi, mid, lo


def _ssd_kernel(*refs, with_y, nc):
    (xr_ref, br_ref, cr_ref, cwx, cbx, cwb, cbb, cwc, cbc, dt_ref, bias_ref, a_ref), rest = refs[:12], refs[12:]
    if with_y:
        (z_ref, dsk_ref, gn_ref, sf0_ref, sb0_ref, y_ref, xs, bs, cs, contrib, dec, wrow, xf, sfa, sba, drow, dtrow, dsplit, esplit) = rest
    else:
        sf_out, sb_out, xs, bs, cs, contrib, dec, wrow = rest
    ck = SSM_CHUNK
    length = nc * ck
    nh = HEADS_PER_GROUP
    nhd = 2 * nh
    gw = GROUP_WIDTH
    dot_hi = functools.partial(jnp.dot, precision=HI, preferred_element_type=F32)

    halo = 16
    win = ck + 2 * halo
    conv_w = jnp.concatenate([cwx[...], cwb[...], cwc[...]], axis=1)
    conv_b = jnp.concatenate([cbx[...], cbb[...], cbc[...]], axis=1)
    wi = lax.broadcasted_iota(jnp.int32, (ck, win), 1) - lax.broadcasted_iota(jnp.int32, (ck, win), 0)

    def conv_silu_chunk(c):
        l0 = pl.multiple_of(c * ck, ck)
        s0 = pl.multiple_of(jnp.clip(l0 - halo, 0, length - win), halo)
        rel = wi + (s0 - l0)
        window = jnp.concatenate([r[0, pl.ds(s0, win), :] for r in (xr_ref, br_ref, cr_ref)], axis=1)
        cur = jnp.concatenate([r[0, pl.ds(l0, ck), :] for r in (xr_ref, br_ref, cr_ref)], axis=1).astype(F32)
        prev = _dot((rel == -1).astype(BF16), window)
        nxt = _dot((rel == 1).astype(BF16), window)
        y = _silu(conv_b + prev * conv_w[0:1, :] + cur * conv_w[1:2, :] + nxt * conv_w[2:3, :])
        xs[pl.ds(l0, ck), :] = y[:, :gw].astype(BF16)
        if with_y:
            xf[pl.ds(l0, ck), :] = y[:, :gw]
        bs[pl.ds(l0, ck), :] = y[:, gw:gw + SSM_STATE].astype(BF16)
        cs[pl.ds(l0, ck), :] = y[:, gw + SSM_STATE:].astype(BF16)
        return y[:, :gw], y[:, gw:gw + SSM_STATE].astype(BF16)

    ri = lax.broadcasted_iota(jnp.int32, (ck, ck), 0)
    ci = lax.broadcasted_iota(jnp.int32, (ck, ck), 1)
    lower = ci <= ri
    upper = ci >= ri
    eye = (ci == ri).astype(BF16)
    dt_row = _softplus(dt_ref[0, 0] + bias_ref[0])
    a_row = dt_row * a_ref[0]
    cum_f = dot_hi(a_row, upper.astype(F32))
    cum_b = dot_hi(a_row, lower.astype(F32))
    tot = dot_hi(a_row, jnp.ones((ck, ck), F32))
    d_row = jnp.where((ri % nhd) < nh, cum_f, cum_b)
    wrow[...] = jnp.exp(tot - d_row) * dt_row
    texp = jnp.exp(tot)
    row4 = lax.broadcasted_iota(jnp.int32, (ck, 2 * gw), 0)
    lane4 = lax.broadcasted_iota(jnp.int32, (ck, 2 * gw), 1) // SSM_HEAD_DIM
    tmask = jnp.where(row4 % nhd == lane4, jnp.concatenate([texp] * (2 * gw // ck), axis=1), 0.0)
    pick = (lax.broadcasted_iota(jnp.int32, (dec.shape[0], ck), 1) // nhd
            == lax.broadcasted_iota(jnp.int32, (dec.shape[0], ck), 0)).astype(F32)
    dec[...] = dot_hi(pick, tmask)
    if with_y:
        drow[...] = d_row
        dtrow[...] = dt_row
        for k, part in enumerate(_split3(d_row)):
            dsplit[k] = part.astype(F32)
        for k, part in enumerate(_split3(jnp.exp(d_row))[:2]):
            esplit[k] = part.astype(F32)

    def spread(src, r0, width):
        rows = jnp.concatenate([jnp.broadcast_to(src[pl.ds(r0 + hd, 1), :], (width, ck)) for hd in range(nhd)], axis=0)
        return _dot_nt(eye, rows.astype(BF16))

    def chunk_states(c, carry):
        xc, b_c = conv_silu_chunk(c)
        wx = spread(wrow, c * nhd, SSM_HEAD_DIM)
        xw = jnp.concatenate([(xc * wx[:, :gw]).astype(BF16), (xc * wx[:, gw:]).astype(BF16)], axis=1)
        contrib[c] = _dot_tn(b_c, xw)
        return carry

    lax.fori_loop(0, nc, chunk_states, 0, unroll=2)

    def fwd_chain(c, s):
        if with_y:
            sfa[c] = s
        return s * dec[pl.ds(c, 1), :gw] + contrib[c, :, :gw]

    def bwd_chain(i, s):
        c = nc - 1 - i
        if with_y:
            sba[c] = s
        return s * dec[pl.ds(c, 1), gw:] + contrib[c, :, gw:]

    if with_y:
        s_f0 = sf0_ref[0, 0]
        s_b0 = sb0_ref[0, 0]
    else:
        s_f0 = jnp.zeros((SSM_STATE, gw), F32)
        s_b0 = s_f0
    s_f = lax.fori_loop(0, nc, fwd_chain, s_f0)
    s_b = lax.fori_loop(0, nc, bwd_chain, s_b0)
    if not with_y:
        sf_out[0, 0] = s_f
        sb_out[0, 0] = s_b
        return

    lane_head = lax.broadcasted_iota(jnp.int32, (ck, gw), 1) // SSM_HEAD_DIM

    def out_step(c, carry):
        l0 = pl.multiple_of(c * ck, ck)
        r0 = c * nhd
        d_b = spread(dsplit.at[0], r0, ck) + spread(dsplit.at[1], r0, ck) + spread(dsplit.at[2], r0, ck)
        e_b = spread(esplit.at[0], r0, SSM_HEAD_DIM) + spread(esplit.at[1], r0, SSM_HEAD_DIM)
        b_c = bs[pl.ds(l0, ck), :]
        c_c = cs[pl.ds(l0, ck), :]
        x_c = xs[pl.ds(l0, ck), :]
        g = _dot_nt(c_c, b_c)
        acc = jnp.zeros((ck, gw), F32)
        for j in range(nh):
            jb = nh + j
            lf = jnp.exp(jnp.where(lower, d_b[:, j * ck:(j + 1) * ck] - drow[pl.ds(r0 + j, 1), :], -1e30)) * dtrow[pl.ds(r0 + j, 1), :]
            lb = jnp.exp(jnp.where(upper, d_b[:, jb * ck:(jb + 1) * ck] - drow[pl.ds(r0 + jb, 1), :], -1e30)) * dtrow[pl.ds(r0 + jb, 1), :]
            m = (g * (lf + lb)).astype(BF16)
            acc = acc + _dot(m, jnp.where(lane_head == j, x_c, jnp.zeros_like(x_c)))
        acc = acc + _dot(c_c, sfa[c].astype(BF16)) * e_b[:, :gw]
        acc = acc + _dot(c_c, sba[c].astype(BF16)) * e_b[:, gw:]
        y = acc + xf[pl.ds(l0, ck), :] * dsk_ref[...]
        y = y * _silu(z_ref[0, pl.ds(l0, ck), :].astype(F32))
        ms = jnp.mean(y * y, axis=-1, keepdims=True)
        y_ref[0, pl.ds(l0, ck), :] = (y * lax.rsqrt(ms + EPS) * gn_ref[...]).astype(y_ref.dtype)
        return carry

    lax.fori_loop(0, nc, out_step, 0, unroll=2)


def _ssd(proj, x_blk0, z_blk0, dt_raw, conv_w, conv_b, dt_bias, a_log, d_skip, gate_norm, init):
    b, length, _ = proj.shape
    nc = length // SSM_CHUNK
    ng, nh, gw, st = SSM_GROUPS, HEADS_PER_GROUP, GROUP_WIDTH, SSM_STATE
    with_y = init is not None
    b_blk0 = x_blk0 * (gw // st) + SSM_D_INNER // st
    c_blk0 = b_blk0 + ng
    ck = SSM_CHUNK
    nhd = 2 * nh
    assert nc * nhd <= ck
    dtg = dt_raw.reshape(b, nc, ck, 2, ng, nh).transpose(0, 4, 1, 3, 5, 2).reshape(b, ng, nc * nhd, ck)
    dtg = jnp.pad(dtg, ((0, 0), (0, 0), (0, ck - nc * nhd), (0, 0)))
    per_row = lambda t: jnp.tile(t.reshape(2, ng, nh).transpose(1, 0, 2).reshape(ng, nhd), (1, ck // nhd)).reshape(ng, ck, 1)
    bias_r = per_row(dt_bias)
    a_r = per_row(-jnp.exp(a_log.astype(F32)))
    cw_x, cw_b, cw_c = conv_w[:, :SSM_D_INNER], conv_w[:, SSM_D_INNER:SSM_D_INNER + ng * st], conv_w[:, SSM_D_INNER + ng * st:]
    cb = conv_b.reshape(1, -1)
    cb_x, cb_b, cb_c = cb[:, :SSM_D_INNER], cb[:, SSM_D_INNER:SSM_D_INNER + ng * st], cb[:, SSM_D_INNER + ng * st:]
    in_specs = [
        pl.BlockSpec((1, length, gw), lambda bi, g: (bi, 0, x_blk0 + g)),
        pl.BlockSpec((1, length, st), lambda bi, g: (bi, 0, b_blk0 + g)),
        pl.BlockSpec((1, length, st), lambda bi, g: (bi, 0, c_blk0 + g)),
        pl.BlockSpec((3, gw), lambda bi, g: (0, g)),
        pl.BlockSpec((1, gw), lambda bi, g: (0, g)),
        pl.BlockSpec((3, st), lambda bi, g: (0, g)),
        pl.BlockSpec((1, st), lambda bi, g: (0, g)),
        pl.BlockSpec((3, st), lambda bi, g: (0, g)),
        pl.BlockSpec((1, st), lambda bi, g: (0, g)),
        pl.BlockSpec((1, 1, ck, ck), lambda bi, g: (bi, g, 0, 0)),
        pl.BlockSpec((1, ck, 1), lambda bi, g: (g, 0, 0)),
        pl.BlockSpec((1, ck, 1), lambda bi, g: (g, 0, 0)),
    ]
    args = [proj, proj, proj, cw_x, cb_x, cw_b, cb_b, cw_c, cb_c, dtg, bias_r, a_r]
    scratch = [pltpu.VMEM((length, gw), BF16), pltpu.VMEM((length, st), BF16), pltpu.VMEM((length, st), BF16),
               pltpu.VMEM((nc, st, 2 * gw), F32), pltpu.VMEM((ck // nhd, 2 * gw), F32), pltpu.VMEM((ck, ck), F32)]
    state_spec = pl.BlockSpec((1, 1, st, gw), lambda bi, g: (bi, g, 0, 0))
    state_shape = jax.ShapeDtypeStruct((b, ng, st, gw), F32)
    if with_y:
        in_specs += [
            pl.BlockSpec((1, length, gw), lambda bi, g: (bi, 0, z_blk0 + g)),
            pl.BlockSpec((1, gw), lambda bi, g: (0, g)),
            pl.BlockSpec((1, gw), lambda bi, g: (0, g)),
            state_spec, state_spec,
        ]
        args += [proj, jnp.repeat(d_skip, SSM_HEAD_DIM).reshape(1, -1), gate_norm.reshape(1, -1), init[0], init[1]]
        out_specs = pl.BlockSpec((1, length, gw), lambda bi, g: (bi, 0, g))
        out_shape = jax.ShapeDtypeStruct((b, length, SSM_D_INNER), BF16)
        scratch += [pltpu.VMEM((length, gw), F32), pltpu.VMEM((nc, st, gw), F32), pltpu.VMEM((nc, st, gw), F32),
                    pltpu.VMEM((ck, ck), F32), pltpu.VMEM((ck, ck), F32), pltpu.VMEM((3, ck, ck), F32), pltpu.VMEM((2, ck, ck), F32)]
    else:
        out_specs = [state_spec, state_spec]
        out_shape = [state_shape, state_shape]
    return pl.pallas_call(
        functools.partial(_ssd_kernel, with_y=with_y, nc=nc),
        grid=(b, ng),
        in_specs=in_specs,
        out_specs=out_specs,
        out_shape=out_shape,
        scratch_shapes=scratch,
        compiler_params=_cparams(("parallel", "parallel")),
        name="ssd_main" if with_y else "ssd_ctx",
    )(*args)


def _proj_res_kernel(*refs, n_lhs):
    a_refs, w_refs = refs[:n_lhs], refs[n_lhs:2 * n_lhs]
    b_ref, gate_ref, res_ref, o_ref = refs[2 * n_lhs:]
    acc = _dot(a_refs[0][0], w_refs[0][...])
    for a_ref, w_ref in zip(a_refs[1:], w_refs[1:]):
        acc = acc + _dot(a_ref[0], w_ref[...])
    o_ref[0] = res_ref[0] + gate_ref[0] * (acc + b_ref[...])


def _proj_residual(lhs, w, bias, gate, resid, tm=1024, tn=1024):
    b, s, k = lhs[0].shape
    n = w.shape[1]
    n_lhs = len(lhs)
    in_specs = [pl.BlockSpec((1, tm, k), lambda bi, i, j: (bi, i, 0)) for _ in lhs]
    in_specs += [pl.BlockSpec((k, tn), functools.partial(lambda bi, i, j, t: (t, j), t=t)) for t in range(n_lhs)]
    in_specs += [
        pl.BlockSpec((1, tn), lambda bi, i, j: (0, j)),
        pl.BlockSpec((1, 1, tn), lambda bi, i, j: (bi, 0, j)),
        pl.BlockSpec((1, tm, tn), lambda bi, i, j: (bi, i, j)),
    ]
    return pl.pallas_call(
        functools.partial(_proj_res_kernel, n_lhs=n_lhs),
        grid=(b, s // tm, n // tn),
        in_specs=in_specs,
        out_specs=pl.BlockSpec((1, tm, tn), lambda bi, i, j: (bi, i, j)),
        out_shape=jax.ShapeDtypeStruct((b, s, n), F32),
        compiler_params=_cparams(("parallel", "parallel", "parallel")),
        name="proj_residual",
    )(*lhs, *([w] * n_lhs), bias.reshape(1, n), gate.reshape(b, 1, n), resid)


def _hy_filter_kernel(tw_ref, f_ref, w1t, w1c, w1s, b1, w2, b2, w3, b3, fr, w4_ref, dl_ref, o_ref, h_scr):
    dot_hi = lambda a, b: jnp.dot(a, b, precision=HI, preferred_element_type=F32)

    @pl.when(pl.program_id(0) == 0)
    def _():
        t = tw_ref[:, 0:1]
        ang = tw_ref[:, 1:2] * f_ref[...]
        pre = t * w1t[...] + dot_hi(jnp.cos(ang), w1c[...]) + dot_hi(-jnp.sin(ang), w1s[...]) + b1[...]
        h = jnp.sin(fr[...] * pre)
        h = jnp.sin(fr[...] * (dot_hi(h, w2[...]) + b2[...]))
        h_scr[...] = jnp.sin(fr[...] * (dot_hi(h, w3[...]) + b3[...]))

    o_ref[...] = dot_hi(h_scr[...], w4_ref[...]) * jnp.exp(-tw_ref[:, 0:1] * dl_ref[...])


def _hyena_filters(length, w1, b1, w2, b2, w3, b3, w4, freq):
    bands = (HY_EMB - 1) // 2
    fw = HY_FILTER_WIDTH
    t = jnp.linspace(0.0, 1.0, length, dtype=F32)
    w = 2 * math.pi * jnp.arange(length, dtype=F32) / length
    tw = jnp.stack([t, w], axis=1)
    f = jnp.linspace(1e-4, bands - 1, bands, dtype=F32).reshape(1, bands)
    min_decay = math.log(HY_TARGET) / HY_SLOW_PCT
    max_decay = math.log(HY_TARGET) / HY_FAST_PCT
    deltas = jnp.abs(jnp.linspace(min_decay, max_decay, D_MODEL, dtype=F32))
    dl = jnp.concatenate([deltas, deltas]).reshape(1, 2 * D_MODEL)
    tn = 1024
    small = lambda shape: pl.BlockSpec(shape, lambda j: (0, 0))
    row = lambda v: v.reshape(1, -1)
    return pl.pallas_call(
        _hy_filter_kernel,
        grid=(2 * D_MODEL // tn,),
        in_specs=[small((length, 2)), small((1, bands)), small((1, fw)), small((bands, fw)), small((bands, fw)), small((1, fw)),
                  small((fw, fw)), small((1, fw)), small((fw, fw)), small((1, fw)), small((1, fw)),
                  pl.BlockSpec((fw, tn), lambda j: (0, j)), pl.BlockSpec((1, tn), lambda j: (0, j))],
        out_specs=pl.BlockSpec((length, tn), lambda j: (0, j)),
        out_shape=jax.ShapeDtypeStruct((length, 2 * D_MODEL), F32),
        scratch_shapes=[pltpu.VMEM((length, fw), F32)],
        compiler_params=_cparams(("arbitrary",)),
        name="hyena_filters",
    )(tw, f, w1[0:1], w1[1:1 + bands], w1[1 + bands:], row(b1), w2, row(b2), w3, row(b3), row(freq), w4, dl)


def _hy_conv_kernel(x0_ref, x1_ref, v_ref, w0, b0, w1, b1, w2, b2, x0_out, u_out):
    length = x0_ref.shape[1]

    def conv(raw_ref, w_ref, b_ref):
        x = raw_ref[0].astype(F32)
        rows = lax.broadcasted_iota(jnp.int32, x.shape, 0)
        xm1 = jnp.where(rows == 0, 0.0, pltpu.roll(x, 1, 0))
        xp1 = jnp.where(rows == length - 1, 0.0, pltpu.roll(x, length - 1, 0))
        return b_ref[...] + xm1 * w_ref[0:1, :] + x * w_ref[1:2, :] + xp1 * w_ref[2:3, :]

    x0_out[0] = conv(x0_ref, w0, b0).astype(x0_out.dtype)
    u_out[0] = (conv(v_ref, w2, b2) * conv(x1_ref, w1, b1)).astype(u_out.dtype)


def _hyena_conv(proj, conv_w, conv_b):
    b, length, _ = proj.shape
    d = D_MODEL
    tn = 512
    nb = d // tn
    cb = conv_b.reshape(1, -1)
    seg = lambda k: pl.BlockSpec((1, length, tn), lambda bi, j: (bi, 0, k * nb + j))
    wseg = lambda k: pl.BlockSpec((3, tn), lambda bi, j: (0, k * nb + j))
    bseg = lambda k: pl.BlockSpec((1, tn), lambda bi, j: (0, k * nb + j))
    out = pl.BlockSpec((1, length, tn), lambda bi, j: (bi, 0, j))
    return pl.pallas_call(
        _hy_conv_kernel,
        grid=(b, nb),
        in_specs=[seg(0), seg(1), seg(2), wseg(0), bseg(0), wseg(1), bseg(1), wseg(2), bseg(2)],
        out_specs=[out, out],
        out_shape=[jax.ShapeDtypeStruct((b, length, d), BF16)] * 2,
        compiler_params=_cparams(("parallel", "parallel")),
        name="hyena_conv",
    )(proj, proj, proj, conv_w, cb, conv_w, cb, conv_w, cb)


def _dft_matrices(length):
    n = 2 * length
    f = lax.broadcasted_iota(jnp.int32, (length, length), 0)
    t = lax.broadcasted_iota(jnp.int32, (length, length), 1)
    ang = ((f * t) % n).astype(F32) * (2 * math.pi / n)
    sign = (1 - 2 * (t % 2)).astype(F32)
    fwd_c = jnp.cos(ang)
    fwd_s = jnp.where(f == 0, sign, -jnp.sin(ang))
    fwd = jnp.concatenate([fwd_c, fwd_s], axis=0).astype(BF16)
    wgt = jnp.where(f == 0, 1.0, 2.0) / n
    inv = jnp.concatenate([(fwd_c * wgt).T, (fwd_s * wgt).T], axis=1).astype(BF16)
    return fwd, inv


def _mm_kernel(a_ref, b_ref, o_ref):
    o_ref[...] = _dot(a_ref[...], b_ref[...]).astype(o_ref.dtype)


def _matmul(a, b, out_dtype=F32, tm=1024, tn=1024):
    m, k = a.shape
    n = b.shape[1]
    return pl.pallas_call(
        _mm_kernel,
        grid=(n // tn, m // tm),
        in_specs=[pl.BlockSpec((tm, k), lambda j, i: (i, 0)), pl.BlockSpec((k, tn), lambda j, i: (0, j))],
        out_specs=pl.BlockSpec((tm, tn), lambda j, i: (i, j)),
        out_shape=jax.ShapeDtypeStruct((m, n), out_dtype),
        compiler_params=_cparams(("parallel", "parallel")),
        name="matmul",
    )(a, b)


def _filter_spectrum(kk, fwd):
    length = kk.shape[0]
    d = D_MODEL
    k_f, k_b = kk[:, :d], kk[:, d:]
    k_fp = k_f.at[0].add(k_b[0])
    k_bp = k_b.at[0].set(0.0)
    spec = _matmul(fwd, jnp.concatenate([k_fp, k_bp], axis=1).astype(BF16))
    a_re, a_im, b_re, b_im = spec[:length, :d], spec[length:, :d], spec[:length, d:], spec[length:, d:]
    nz = (jnp.arange(length) != 0)[:, None]
    k_re = a_re + b_re
    k_im = jnp.where(nz, a_im - b_im, a_im + b_im)
    return jnp.stack([k_re, jnp.where(nz, k_im, 0.0), jnp.where(nz, k_re, k_im)])


def _dft_fwd_kernel(fc_ref, fs_ref, u_ref, k_ref, y_ref):
    u = u_ref[0]
    re = _dot(fc_ref[...], u)
    im = _dot(fs_ref[...], u)
    y_ref[0, 0] = (re * k_ref[0] - im * k_ref[1]).astype(y_ref.dtype)
    y_ref[0, 1] = (re * k_ref[1] + im * k_ref[2]).astype(y_ref.dtype)


def _dft_forward(u, fwd, coef, tm=1024, tn=512):
    b, length, d = u.shape
    ni = length // tm
    return pl.pallas_call(
        _dft_fwd_kernel,
        grid=(d // tn, ni, b),
        in_specs=[
            pl.BlockSpec((tm, length), lambda j, i, bi: (i, 0)),
            pl.BlockSpec((tm, length), lambda j, i, bi: (ni + i, 0)),
            pl.BlockSpec((1, length, tn), lambda j, i, bi: (bi, 0, j)),
            pl.BlockSpec((3, tm, tn), lambda j, i, bi: (0, i, j)),
        ],
        out_specs=pl.BlockSpec((1, 2, tm, tn), lambda j, i, bi: (bi, 0, i, j)),
        out_shape=jax.ShapeDtypeStruct((b, 2, length, d), BF16),
        compiler_params=_cparams(("parallel", "parallel", "parallel")),
        name="dft_forward",
    )(fwd, fwd, u, coef)


def _dft_inv_kernel(g_ref, y_ref, x0_ref, u_ref, fb_ref, o_ref):
    conv = _dot(g_ref[...], y_ref[0])
    o_ref[0] = (x0_ref[0].astype(F32) * (conv + u_ref[0].astype(F32) * fb_ref[...])).astype(o_ref.dtype)


def _dft_inverse(y, inv, x0, u, filt_bias, tm=512, tn=512):
    b, n2, d = y.shape
    length = n2 // 2
    return pl.pallas_call(
        _dft_inv_kernel,
        grid=(d // tn, length // tm, b),
        in_specs=[
            pl.BlockSpec((tm, n2), lambda j, i, bi: (i, 0)),
            pl.BlockSpec((1, n2, tn), lambda j, i, bi: (bi, 0, j)),
            pl.BlockSpec((1, tm, tn), lambda j, i, bi: (bi, i, j)),
            pl.BlockSpec((1, tm, tn), lambda j, i, bi: (bi, i, j)),
            pl.BlockSpec((1, tn), lambda j, i, bi: (0, j)),
        ],
        out_specs=pl.BlockSpec((1, tm, tn), lambda j, i, bi: (bi, i, j)),
        out_shape=jax.ShapeDtypeStruct((b, length, d), BF16),
        compiler_params=_cparams(("parallel", "parallel", "parallel")),
        name="dft_inverse",
    )(inv, y, x0, u, filt_bias.reshape(1, d))


def _router_kernel(x_ref, g_ref, sh_ref, sc_ref, wr_ref, br_ref, hp_ref, idx_ref, gate_ref, rank_ref, cnt_ref, run):
    tm = x_ref.shape[1]
    half = x_ref.shape[2] // 2
    first = (pl.program_id(0) == 0) & (pl.program_id(1) == 0)

    @pl.when(first)
    def _():
        run[...] = jnp.zeros_like(run)

    x = x_ref[0]
    ms = jnp.mean(x * x, axis=-1, keepdims=True)
    h = x * lax.rsqrt(ms + EPS) * g_ref[...] * (1.0 + sc_ref[0]) + sh_ref[0]
    hb = pltpu.bitcast(h.astype(BF16).astype(F32), jnp.uint32)
    hp_ref[0] = (hb[:, :half] >> 16) | (hb[:, half:] & jnp.uint32(0xFFFF0000))

    logits = jnp.dot(h, wr_ref[...], precision=HI, preferred_element_type=F32) + br_ref[...]
    lane_i = lax.broadcasted_iota(jnp.int32, logits.shape, 1)
    lane = lane_i.astype(F32)
    work = jnp.where(lane_i < N_EXPERTS, logits, -jnp.inf)
    li = lax.broadcasted_iota(jnp.int32, (tm, tm), 0)
    ki = lax.broadcasted_iota(jnp.int32, (tm, tm), 1)
    strict_lower = (ki < li).astype(BF16)
    sels, tops = [], []
    for _ in range(TOP_K):
        m = jnp.max(work, axis=-1, keepdims=True)
        first_idx = jnp.min(jnp.where(work == m, lane, float(LANES)), axis=-1, keepdims=True)
        sel = lane == first_idx
        sels.append(sel)
        tops.append((m, first_idx))
        work = jnp.where(sel, -jnp.inf, work)
    chosen = sels[0] | sels[1] | sels[2] | sels[3]
    before = _dot(strict_lower, chosen.astype(BF16)) + run[...]
    run[...] = run[...] + jnp.sum(chosen.astype(F32), axis=0, keepdims=True)
    den = sum(jnp.exp(m - tops[0][0]) for m, _ in tops)
    idx_o = jnp.zeros(logits.shape, jnp.int32)
    gate_o = jnp.zeros(logits.shape, F32)
    rank_o = jnp.zeros(logits.shape, jnp.int32)
    for k, (sel, (m, first_idx)) in enumerate(zip(sels, tops)):
        rank = jnp.sum(jnp.where(sel, before, 0.0), axis=-1, keepdims=True).astype(jnp.int32)
        idx_o = jnp.where(lane_i == k, first_idx.astype(jnp.int32), idx_o)
        gate_o = jnp.where(lane_i == k, jnp.exp(m - tops[0][0]) / den, gate_o)
        rank_o = jnp.where(lane_i == k, rank, rank_o)
    idx_ref[0] = idx_o
    gate_ref[0] = gate_o
    rank_ref[0] = rank_o
    cnt_ref[...] = run[...]


def _moe_route(x, g, shift, scale, w_r, b_r, tm=512):
    b, s, d = x.shape
    wr = jnp.zeros((d, LANES), F32).at[:, :N_EXPERTS].set(w_r)
    br = jnp.zeros((1, LANES), F32).at[0, :N_EXPERTS].set(b_r)
    tok = lambda n: pl.BlockSpec((1, tm, n), lambda bi, i: (bi, i, 0))
    return pl.pallas_call(
        _router_kernel,
        grid=(b, s // tm),
        in_specs=[
            tok(d),
            pl.BlockSpec((1, d), lambda bi, i: (0, 0)),
            pl.BlockSpec((1, 1, d), lambda bi, i: (bi, 0, 0)),
            pl.BlockSpec((1, 1, d), lambda bi, i: (bi, 0, 0)),
            pl.BlockSpec((d, LANES), lambda bi, i: (0, 0)),
            pl.BlockSpec((1, LANES), lambda bi, i: (0, 0)),
        ],
        out_specs=[tok(d // 2), tok(LANES), tok(LANES), tok(LANES), pl.BlockSpec((1, LANES), lambda bi, i: (0, 0))],
        out_shape=[
            jax.ShapeDtypeStruct((b, s, d // 2), jnp.uint32),
            jax.ShapeDtypeStruct((b, s, LANES), jnp.int32),
            jax.ShapeDtypeStruct((b, s, LANES), F32),
            jax.ShapeDtypeStruct((b, s, LANES), jnp.int32),
            jax.ShapeDtypeStruct((1, LANES), F32),
        ],
        scratch_shapes=[pltpu.VMEM((1, LANES), F32)],
        compiler_params=_cparams(("arbitrary", "arbitrary")),
        name="moe_route",
    )(x, g.reshape(1, d), shift.reshape(b, 1, d), scale.reshape(b, 1, d), wr, br)


def _gather_kernel(idx_ref, src_ref, o_ref, sem):
    rows = o_ref.shape[0]

    def issue(r, carry):
        pltpu.make_async_copy(src_ref.at[pl.ds(idx_ref[0, 0, r], 1), :], o_ref.at[pl.ds(r, 1), :], sem).start()
        return carry

    lax.fori_loop(0, rows, issue, 0)
    pltpu.make_async_copy(src_ref.at[pl.ds(0, rows), :], o_ref, sem).wait()


def _gather_rows(src, idx, rows):
    n_out = idx.shape[0]
    w = src.shape[1]
    nb = n_out // rows
    return pl.pallas_call(
        _gather_kernel,
        grid=(nb,),
        in_specs=[
            pl.BlockSpec((1, 1, rows), lambda i: (i, 0, 0), memory_space=pltpu.SMEM),
            pl.BlockSpec(memory_space=pl.ANY),
        ],
        out_specs=pl.BlockSpec((rows, w), lambda i: (i, 0)),
        out_shape=jax.ShapeDtypeStruct((n_out, w), src.dtype),
        scratch_shapes=[pltpu.SemaphoreType.DMA(())],
        compiler_params=_cparams(("arbitrary",)),
        name="gather_rows",
    )(idx.reshape(nb, 1, rows), src)


def _ffn_kernel(be_ref, na_ref, x_ref, w1g_ref, w1u_ref, b1g_ref, b1u_ref, w2_ref, b2_ref, o_ref, xb):
    i = pl.program_id(0)
    f = pl.program_id(1)
    active = i < na_ref[0]
    half = x_ref.shape[1]

    @pl.when(f == 0)
    def _():
        xp = x_ref[...]
        xb[:, :half] = pltpu.bitcast(xp << 16, F32).astype(BF16)
        xb[:, half:] = pltpu.bitcast(xp & jnp.uint32(0xFFFF0000), F32).astype(BF16)
        o_ref[...] = jnp.broadcast_to(b2_ref[0, 0], o_ref.shape)

    @pl.when(active)
    def _():
        x = xb[...]
        g = _dot(x, w1g_ref[0, 0].astype(BF16)) + b1g_ref[0, 0]
        u = _dot(x, w1u_ref[0, 0].astype(BF16)) + b1u_ref[0, 0]
        g = jnp.minimum(g, SWIGLU_LIMIT)
        u = jnp.clip(u, -SWIGLU_LIMIT, SWIGLU_LIMIT)
        a = (u + 1.0) * g / (1.0 + jnp.exp(-SWIGLU_ALPHA * g))
        o_ref[...] += _dot(a.astype(BF16), w2_ref[0, 0].astype(BF16))


def _moe_experts(layer, xs_packed, block_e, n_active, w1, b1, w2, b2):
    n_pad, half = xs_packed.shape
    d = 2 * half
    tm, tf = MOE_ROWS, MOE_FF_TILE
    nf = MOE_FF // tf
    nb = n_pad // tm

    def blk(i, na):
        return jnp.minimum(i, na[0] - 1)

    def ff(i, f, na):
        return jnp.where(i < na[0], f, nf - 1)

    grid_spec = pltpu.PrefetchScalarGridSpec(
        num_scalar_prefetch=2,
        grid=(nb, nf),
        in_specs=[
            pl.BlockSpec((tm, half), lambda i, f, be, na: (blk(i, na), 0)),
            pl.BlockSpec((1, 1, d, tf), lambda i, f, be, na: (layer, be[blk(i, na)], 0, ff(i, f, na))),
            pl.BlockSpec((1, 1, d, tf), lambda i, f, be, na: (layer, be[blk(i, na)], 0, nf + ff(i, f, na))),
            pl.BlockSpec((1, 1, 1, tf), lambda i, f, be, na: (layer, be[blk(i, na)], 0, ff(i, f, na))),
            pl.BlockSpec((1, 1, 1, tf), lambda i, f, be, na: (layer, be[blk(i, na)], 0, nf + ff(i, f, na))),
            pl.BlockSpec((1, 1, tf, d), lambda i, f, be, na: (layer, be[blk(i, na)], ff(i, f, na), 0)),
            pl.BlockSpec((1, 1, 1, d), lambda i, f, be, na: (layer, be[blk(i, na)], 0, 0)),
        ],
        out_specs=pl.BlockSpec((tm, d), lambda i, f, be, na: (i, 0)),
        scratch_shapes=[pltpu.VMEM((tm, d), BF16)],
    )
    depth, ne = w1.shape[:2]
    return pl.pallas_call(
        _ffn_kernel,
        grid_spec=grid_spec,
        out_shape=jax.ShapeDtypeStruct((n_pad, d), F32),
        compiler_params=_cparams(("arbitrary", "arbitrary")),
        name="moe_experts",
    )(block_e, n_active, xs_packed, w1, w1, b1.reshape(depth, ne, 1, -1), b1.reshape(depth, ne, 1, -1), w2, b2.reshape(depth, ne, 1, -1))


def _combine_kernel(pos_ref, y_ref, gate_ref, x_ref, gm_ref, o_ref, buf, sem):
    tc = x_ref.shape[1]

    def issue(r, carry):
        pltpu.make_async_copy(y_ref.at[pl.ds(pos_ref[0, 0, r], 1), :], buf.at[pl.ds(r, 1), :], sem).start()
        return carry

    lax.fori_loop(0, TOP_K * tc, issue, 0)
    pltpu.make_async_copy(y_ref.at[pl.ds(0, TOP_K * tc), :], buf, sem).wait()
    gates = gate_ref[0]
    acc = jnp.zeros((tc, x_ref.shape[2]), F32)
    for k in range(TOP_K):
        acc = acc + gates[:, k:k + 1] * buf[k * tc:(k + 1) * tc, :]
    o_ref[0] = x_ref[0] + gm_ref[0] * acc


def _moe_combine(y_sorted, pos, gates, x, gmod, tc=128):
    b, s, d = x.shape
    nt = s // tc
    pos_blk = pos.reshape(b * nt, tc, TOP_K).transpose(0, 2, 1).reshape(b * nt, 1, TOP_K * tc)
    return pl.pallas_call(
        _combine_kernel,
        grid=(b, nt),
        in_specs=[
            pl.BlockSpec((1, 1, TOP_K * tc), lambda bi, i: (bi * nt + i, 0, 0), memory_space=pltpu.SMEM),
            pl.BlockSpec(memory_space=pl.ANY),
            pl.BlockSpec((1, tc, LANES), lambda bi, i: (bi, i, 0)),
            pl.BlockSpec((1, tc, d), lambda bi, i: (bi, i, 0)),
            pl.BlockSpec((1, 1, d), lambda bi, i: (bi, 0, 0)),
        ],
        out_specs=pl.BlockSpec((1, tc, d), lambda bi, i: (bi, i, 0)),
        out_shape=jax.ShapeDtypeStruct((b, s, d), F32),
        scratch_shapes=[pltpu.VMEM((TOP_K * tc, d), F32), pltpu.SemaphoreType.DMA(())],
        compiler_params=_cparams(("arbitrary", "arbitrary")),
        name="moe_combine",
    )(pos_blk, y_sorted, gates, x, gmod.reshape(b, 1, d))


def _moe_ffn(layer, x, g, shift, scale, gmod, w_r, b_r, w1, b1, w2, b2):
    b, s, d = x.shape
    n = b * s
    hp, idx, gates, rank, counts = _moe_route(x, g, shift, scale, w_r, b_r)
    counts = counts[0, :N_EXPERTS].astype(jnp.int32)
    padded = (counts + MOE_ROWS - 1) // MOE_ROWS * MOE_ROWS
    pad_end = jnp.cumsum(padded)
    pad_start = pad_end - padded
    idx4 = idx.reshape(n, LANES)[:, :TOP_K]
    onehot = idx4[:, :, None] == jnp.arange(N_EXPERTS, dtype=jnp.int32)
    pos = jnp.sum(jnp.where(onehot, pad_start, 0), axis=-1) + rank.reshape(n, LANES)[:, :TOP_K]
    n_blocks = n * TOP_K // MOE_ROWS + N_EXPERTS
    n_pad = n_blocks * MOE_ROWS
    tok_ids = jnp.broadcast_to(jnp.arange(n, dtype=jnp.int32)[:, None], (n, TOP_K))
    slot_tok = jnp.zeros((n_pad,), jnp.int32).at[pos.reshape(-1)].set(tok_ids.reshape(-1))
    block_start = jnp.arange(n_blocks, dtype=jnp.int32) * MOE_ROWS
    block_e = jnp.minimum(jnp.sum(pad_end[None, :] <= block_start[:, None], axis=1), N_EXPERTS - 1).astype(jnp.int32)
    n_active = (pad_end[-1:] // MOE_ROWS).astype(jnp.int32)
    xs = _gather_rows(hp.reshape(n, d // 2), slot_tok, MOE_ROWS)
    ys = _moe_experts(layer, xs, block_e, n_active, w1, b1, w2, b2)
    return _moe_combine(ys, pos.astype(jnp.int32), gates, x, gmod)


def kernel(x, c, ctx, c_ctx, ada_w, ada_b, norm_mix, norm_ffn, ev_w_in, ev_conv_w, ev_conv_b, ev_q_norm, ev_k_norm, ev_rpb, ev_a_log, ev_dt_bias, ev_d_skip, ev_gate_norm, ev_w_out, od_w_in, od_b_in, od_conv_w, od_conv_b, od_filt_w1, od_filt_b1, od_filt_w2, od_filt_b2, od_filt_w3, od_filt_b3, od_filt_w4, od_filt_freq, od_filt_bias, od_w_out, od_b_out, moe_router_w, moe_router_b, moe_w1, moe_b1, moe_w2, moe_b2):
    b, s, d = x.shape
    assert ada_w.shape[0] == 2 and ev_w_in.shape[0] == 1 and od_w_in.shape[0] == 1 and d == D_MODEL
    cc = jnp.zeros((16, d), F32).at[:b].set(c).at[b].set(c_ctx)
    mod = _ada_mod(cc, ada_w, ada_b)
    chunk = lambda m, k: m[:, k * d:(k + 1) * d]
    moe = (moe_w1, moe_b1, moe_w2, moe_b2)

    m0 = mod[0, :b]
    mc0 = jnp.broadcast_to(mod[0, b:b + 1], (b, N_MOD * d))
    w_in = ev_w_in[0]
    n_main = 2 * NA_WIDTH + SSM_D_INNER + NA_WIDTH + SSM_XBC
    kv0 = NA_WIDTH + SSM_D_INNER
    w_main = w_in[:, :n_main].astype(BF16)
    w_ctx = w_in[:, kv0:n_main].astype(BF16)
    w_dt = jnp.zeros((d, LANES), BF16).at[:, :2 * SSM_HEADS].set(w_in[:, n_main:].astype(BF16))
    zb = lambda n: jnp.zeros((n,), F32)
    g0 = norm_mix[0]
    proj = _norm_mod_matmul(x, g0, chunk(m0, 0), chunk(m0, 1), w_main, zb(n_main), BF16, 1024, 1024)
    dt_raw = _norm_mod_matmul(x, g0, chunk(m0, 0), chunk(m0, 1), w_dt, zb(LANES), F32, 1024, LANES)[..., :2 * SSM_HEADS]
    proj_c = _norm_mod_matmul(ctx, g0, chunk(mc0, 0), chunk(mc0, 1), w_ctx, zb(n_main - kv0), BF16, 256, 1024)
    dt_raw_c = _norm_mod_matmul(ctx, g0, chunk(mc0, 0), chunk(mc0, 1), w_dt, zb(LANES), F32, 256, LANES)[..., :2 * SSM_HEADS]
    attn = _attention(proj, proj_c, ev_q_norm[0], ev_k_norm[0], _attn_bias_table(ev_rpb[0], s // GRID_W))
    ssm_args = (ev_conv_w[0], ev_conv_b[0], ev_dt_bias[0], ev_a_log[0])
    x_blk_c = 2 * NA_WIDTH // GROUP_WIDTH
    states = _ssd(proj_c, x_blk_c, None, dt_raw_c, *ssm_args, None, None, None)
    x_blk = (kv0 + 2 * NA_WIDTH) // GROUP_WIDTH
    z_blk = NA_WIDTH // GROUP_WIDTH
    y_ssm = _ssd(proj, x_blk, z_blk, dt_raw, *ssm_args, ev_d_skip[0], ev_gate_norm[0], states)
    x = _proj_residual([attn, y_ssm], ev_w_out[0].astype(BF16), zb(d), chunk(m0, 2), x)
    x = _moe_ffn(0, x, norm_ffn[0], chunk(m0, 3), chunk(m0, 4), chunk(m0, 5), moe_router_w[0], moe_router_b[0], *moe)

    m1 = mod[1, :b]
    proj_h = _norm_mod_matmul(x, norm_mix[1], chunk(m1, 0), chunk(m1, 1), od_w_in[0].astype(BF16), od_b_in[0], BF16, 1024, 1024)
    filt = _hyena_filters(s, od_filt_w1[0], od_filt_b1[0], od_filt_w2[0], od_filt_b2[0], od_filt_w3[0], od_filt_b3[0],
                          od_filt_w4[0], od_filt_freq[0])
    fwd, inv = _dft_matrices(s)
    coef = _filter_spectrum(filt, fwd)
    x0, u = _hyena_conv(proj_h, od_conv_w[0], od_conv_b[0])
    spec = _dft_forward(u, fwd, coef)
    y_h = _dft_inverse(spec.reshape(b, 2 * s, d), inv, x0, u, od_filt_bias[0])
    x = _proj_residual([y_h], od_w_out[0].astype(BF16), od_b_out[0], chunk(m1, 2), x)
    x = _moe_ffn(1, x, norm_ffn[1], chunk(m1, 3), chunk(m1, 4), chunk(m1, 5), moe_router_w[1], moe_router_b[1], *moe)
    return x
```

```python
import functools
import math

import jax
import jax.numpy as jnp
import numpy as np
from jax import lax
from jax.experimental import pallas as pl
from jax.experimental.pallas import tpu as pltpu

F32 = jnp.float32
BF16 = jnp.bfloat16
HI = lax.Precision.HIGHEST

D_MODEL = 2048
N_MOD = 6
EPS = 1e-6
NEG_INF = -1e9
GRID_W = 64
NA_HEADS = 16
NA_HEAD_DIM = 128
NA_WIDTH = NA_HEADS * NA_HEAD_DIM
NA_KH = 8
NA_KW = 16
SSM_D_INNER = 2048
SSM_HEAD_DIM = 64
SSM_HEADS = SSM_D_INNER // SSM_HEAD_DIM
SSM_GROUPS = 8
SSM_STATE = 128
SSM_CHUNK = 128
SSM_XBC = SSM_D_INNER + 2 * SSM_GROUPS * SSM_STATE
HEADS_PER_GROUP = SSM_HEADS // SSM_GROUPS
GROUP_WIDTH = HEADS_PER_GROUP * SSM_HEAD_DIM
HY_EMB = 33
HY_FILTER_WIDTH = 64
HY_TARGET = 1e-2
HY_FAST_PCT = 0.3
HY_SLOW_PCT = 1.5
N_EXPERTS = 32
TOP_K = 4
MOE_FF = 2048
SWIGLU_LIMIT = 7.0
SWIGLU_ALPHA = 1.702
MOE_ROWS = 512
MOE_FF_TILE = 512
MOE_N_FF = MOE_FF // MOE_FF_TILE
LANES = 128
VMEM_LIMIT = 56 * 1024 * 1024


def _cparams(sem, vmem=VMEM_LIMIT):
    return pltpu.CompilerParams(dimension_semantics=sem, vmem_limit_bytes=vmem)


def _dot(a, b):
    return jnp.dot(a, b, preferred_element_type=F32)


def _dot_nt(a, b):
    return lax.dot_general(a, b, (((1,), (1,)), ((), ())), preferred_element_type=F32)


def _dot_tn(a, b):
    return lax.dot_general(a, b, (((0,), (0,)), ((), ())), preferred_element_type=F32)


def _silu(x):
    return x / (1.0 + jnp.exp(-x))


def _softplus(x):
    return jnp.maximum(x, 0.0) + jnp.log1p(jnp.exp(-jnp.abs(x)))


def _ada_kernel(c_ref, w_ref, b_ref, o_ref):
    sc = _silu(c_ref[...])
    hi = sc.astype(BF16)
    lo = (sc - hi.astype(F32)).astype(BF16)
    w = w_ref[0].astype(BF16)
    o_ref[0] = _dot(hi, w) + _dot(lo, w) + b_ref[0]


def _ada_mod(cc, ada_w, ada_b):
    depth, d, n = ada_w.shape
    tn = 1024
    return pl.pallas_call(
        _ada_kernel,
        grid=(depth, n // tn),
        in_specs=[
            pl.BlockSpec((cc.shape[0], d), lambda i, j: (0, 0)),
            pl.BlockSpec((1, d, tn), lambda i, j: (i, 0, j)),
            pl.BlockSpec((1, 1, tn), lambda i, j: (i, 0, j)),
        ],
        out_specs=pl.BlockSpec((1, cc.shape[0], tn), lambda i, j: (i, 0, j)),
        out_shape=jax.ShapeDtypeStruct((depth, cc.shape[0], n), F32),
        compiler_params=_cparams(("parallel", "parallel")),
        name="ada_mod",
    )(cc, ada_w, ada_b.reshape(depth, 1, n))


def _nmm_kernel(x_ref, g_ref, sh_ref, sc_ref, w_ref, b_ref, o_ref, h_scr):
    @pl.when(pl.program_id(2) == 0)
    def _():
        x = x_ref[0]
        ms = jnp.mean(x * x, axis=-1, keepdims=True)
        xn = x * lax.rsqrt(ms + EPS) * g_ref[...]
        h_scr[...] = (xn * (1.0 + sc_ref[0]) + sh_ref[0]).astype(BF16)

    o_ref[0] = (_dot(h_scr[...], w_ref[...]) + b_ref[...]).astype(o_ref.dtype)


def _norm_mod_matmul(x, g, shift, scale, w, bias, out_dtype, tm, tn):
    b, s, d = x.shape
    n = w.shape[1]
    tm = min(tm, s)
    return pl.pallas_call(
        _nmm_kernel,
        grid=(b, s // tm, n // tn),
        in_specs=[
            pl.BlockSpec((1, tm, d), lambda bi, i, j: (bi, i, 0)),
            pl.BlockSpec((1, d), lambda bi, i, j: (0, 0)),
            pl.BlockSpec((1, 1, d), lambda bi, i, j: (bi, 0, 0)),
            pl.BlockSpec((1, 1, d), lambda bi, i, j: (bi, 0, 0)),
            pl.BlockSpec((d, tn), lambda bi, i, j: (0, j)),
            pl.BlockSpec((1, tn), lambda bi, i, j: (0, j)),
        ],
        out_specs=pl.BlockSpec((1, tm, tn), lambda bi, i, j: (bi, i, j)),
        out_shape=jax.ShapeDtypeStruct((b, s, n), out_dtype),
        scratch_shapes=[pltpu.VMEM((tm, d), BF16)],
        compiler_params=_cparams(("parallel", "parallel", "arbitrary")),
        name="norm_mod_matmul",
    )(x, g.reshape(1, d), shift.reshape(b, 1, d), scale.reshape(b, 1, d), w, bias.reshape(1, n))


NA_PAIR_ROWS = NA_KH + 2
NA_VARIANTS = 5


def _pair_window_start(i, rows):
    return np.clip(2 * i - NA_KH // 2, 0, rows - NA_PAIR_ROWS)


def _attn_bias_table(rpb, rows):
    nh = rpb.shape[0]
    kc = np.arange(GRID_W)[:, None]
    qc = np.arange(GRID_W)[None, :]
    col_off = np.clip(kc - qc, -(NA_KW - 1), NA_KW - 1) + NA_KW - 1
    onehot = (col_off[None] == np.arange(2 * NA_KW - 1)[:, None, None]).astype(np.float32)
    toep = jnp.einsum('hrc,ckq->hrkq', rpb.astype(F32), onehot, precision=HI)
    ws_col = np.clip(qc - NA_KW // 2, 0, GRID_W - NA_KW)
    toep = jnp.where(((kc >= ws_col) & (kc < ws_col + NA_KW))[None, None], toep, NEG_INF)
    dead = jnp.full((nh, GRID_W, GRID_W), NEG_INF, F32)
    rep_pair = [0, 1, 2, rows // 2 - 2, rows // 2 - 1]
    blocks = []
    for i in rep_pair:
        ws = int(_pair_window_start(i, rows))
        for t in range(NA_PAIR_ROWS):
            for e in range(2):
                r = 2 * i + e
                rs = int(np.clip(r - NA_KH // 2, 0, rows - NA_KH))
                kr = ws + t
                blocks.append(toep[:, kr - r + NA_KH - 1] if rs <= kr < rs + NA_KH else dead)
    bias = jnp.stack(blocks, axis=1).reshape(nh, NA_VARIANTS, NA_PAIR_ROWS, 2, GRID_W, GRID_W)
    return bias.transpose(0, 1, 2, 4, 3, 5).reshape(nh, NA_VARIANTS, NA_PAIR_ROWS * GRID_W, 2 * GRID_W)


def _head_rmsnorm(t, g):
    sq = t * t
    hi = sq.astype(BF16)
    lo = (sq - hi.astype(F32)).astype(BF16)
    avg = jnp.full((NA_HEAD_DIM, NA_HEAD_DIM), 1.0 / NA_HEAD_DIM, BF16)
    ms = _dot(hi, avg) + _dot(lo, avg)
    return t * lax.rsqrt(ms + EPS) * g


def _attn_kernel(q_ref, k_ref, v_ref, kc_ref, vc_ref, qg_ref, kg_ref, bias_ref, o_ref, qs, ks, kcs, vt, vct):
    seq, hd = q_ref.shape[1], q_ref.shape[2]
    rows = seq // GRID_W
    pair = 2 * GRID_W
    win_blocks = NA_PAIR_ROWS // 2
    ctx_blocks = kc_ref.shape[1] // pair
    qs[...] = (_head_rmsnorm(q_ref[0].astype(F32), qg_ref[...]) * hd ** -0.5).astype(BF16)
    ks[...] = _head_rmsnorm(k_ref[0].astype(F32), kg_ref[...]).astype(BF16)
    kcs[...] = _head_rmsnorm(kc_ref[0].astype(F32), kg_ref[...]).astype(BF16)
    for blk in range(seq // pair):
        vt[blk] = v_ref[0, blk * pair:(blk + 1) * pair, :].astype(F32).T.astype(BF16)
    for blk in range(ctx_blocks):
        vct[blk] = vc_ref[0, blk * pair:(blk + 1) * pair, :].astype(F32).T.astype(BF16)

    def body(i, carry):
        ws = jnp.clip(2 * i - NA_KH // 2, 0, rows - NA_PAIR_ROWS)
        var = (2 * i - ws) // 2
        wb = ws // 2
        q0 = pl.multiple_of(i * pair, pair)
        k0 = pl.multiple_of(ws * GRID_W, pair)
        q_p = qs[pl.ds(q0, pair), :]
        s_lat = _dot_nt(ks[pl.ds(k0, win_blocks * pair), :], q_p) + bias_ref[0, var]
        s_ctx = _dot_nt(kcs[...], q_p)
        m = jnp.maximum(jnp.max(s_lat, axis=0, keepdims=True), jnp.max(s_ctx, axis=0, keepdims=True))
        p_lat = jnp.exp(s_lat - m)
        p_ctx = jnp.exp(s_ctx - m)
        den = jnp.sum(p_lat, axis=0, keepdims=True) + jnp.sum(p_ctx, axis=0, keepdims=True)
        p_lat = p_lat.astype(BF16)
        p_ctx = p_ctx.astype(BF16)
        o_t = jnp.zeros((hd, pair), F32)
        for blk in range(win_blocks):
            o_t = o_t + _dot(vt[wb + blk], p_lat[blk * pair:(blk + 1) * pair, :])
        for blk in range(ctx_blocks):
            o_t = o_t + _dot(vct[blk], p_ctx[blk * pair:(blk + 1) * pair, :])
        o_ref[0, pl.ds(q0, pair), :] = (o_t / den).T.astype(o_ref.dtype)
        return carry

    lax.fori_loop(0, rows // 2, body, 0, unroll=2)


def _attention(proj, proj_c, q_norm, k_norm, bias_tab):
    b, s, _ = proj.shape
    ctx = proj_c.shape[1]
    hd = NA_HEAD_DIM
    nh = NA_HEADS
    return pl.pallas_call(
        _attn_kernel,
        grid=(b, nh),
        in_specs=[
            pl.BlockSpec((1, s, hd), lambda bi, h: (bi, 0, h)),
            pl.BlockSpec((1, s, hd), lambda bi, h: (bi, 0, 2 * nh + h)),
            pl.BlockSpec((1, s, hd), lambda bi, h: (bi, 0, 3 * nh + h)),
            pl.BlockSpec((1, ctx, hd), lambda bi, h: (bi, 0, h)),
            pl.BlockSpec((1, ctx, hd), lambda bi, h: (bi, 0, nh + h)),
            pl.BlockSpec((1, hd), lambda bi, h: (0, 0)),
            pl.BlockSpec((1, hd), lambda bi, h: (0, 0)),
            pl.BlockSpec((1,) + bias_tab.shape[1:], lambda bi, h: (h, 0, 0, 0)),
        ],
        out_specs=pl.BlockSpec((1, s, hd), lambda bi, h: (bi, 0, h)),
        out_shape=jax.ShapeDtypeStruct((b, s, nh * hd), BF16),
        scratch_shapes=[pltpu.VMEM((s, hd), BF16), pltpu.VMEM((s, hd), BF16), pltpu.VMEM((ctx, hd), BF16),
                        pltpu.VMEM((s // (2 * GRID_W), hd, 2 * GRID_W), BF16), pltpu.VMEM((ctx // (2 * GRID_W), hd, 2 * GRID_W), BF16)],
        compiler_params=_cparams(("parallel", "parallel")),
        name="nbr_attention",
    )(proj, proj, proj, proj_c, proj_c, q_norm.reshape(1, hd), k_norm.reshape(1, hd), bias_tab)


def _split3(v):
    hi = v.astype(BF16)
    r1 = v - hi.astype(F32)
    mid = r1.astype(BF16)
    lo = (r1 - mid.astype(F32)).astype(BF16)
    return hi, mid, lo


def _ssd_kernel(*refs, with_y, nc):
    (xr_ref, br_ref, cr_ref, cwx, cbx, cwb, cbb, cwc, cbc, dt_ref, bias_ref, a_ref), rest = refs[:12], refs[12:]
    if with_y:
        (z_ref, dsk_ref, gn_ref, sf0_ref, sb0_ref, y_ref, xs, bs, cs, contrib, dec, wrow, xf, sfa, sba, drow, dtrow, dsplit, esplit) = rest
    else:
        sf_out, sb_out, xs, bs, cs, contrib, dec, wrow = rest
    ck = SSM_CHUNK
    length = nc * ck
    nh = HEADS_PER_GROUP
    nhd = 2 * nh
    gw = GROUP_WIDTH
    dot_hi = functools.partial(jnp.dot, precision=HI, preferred_element_type=F32)

    halo = 16
    win = ck + 2 * halo
    conv_w = jnp.concatenate([cwx[...], cwb[...], cwc[...]], axis=1)
    conv_b = jnp.concatenate([cbx[...], cbb[...], cbc[...]], axis=1)
    wi = lax.broadcasted_iota(jnp.int32, (ck, win), 1) - lax.broadcasted_iota(jnp.int32, (ck, win), 0)

    def conv_silu_chunk(c):
        l0 = pl.multiple_of(c * ck, ck)
        s0 = pl.multiple_of(jnp.clip(l0 - halo, 0, length - win), halo)
        rel = wi + (s0 - l0)
        window = jnp.concatenate([r[0, pl.ds(s0, win), :] for r in (xr_ref, br_ref, cr_ref)], axis=1)
        cur = jnp.concatenate([r[0, pl.ds(l0, ck), :] for r in (xr_ref, br_ref, cr_ref)], axis=1).astype(F32)
        prev = _dot((rel == -1).astype(BF16), window)
        nxt = _dot((rel == 1).astype(BF16), window)
        y = _silu(conv_b + prev * conv_w[0:1, :] + cur * conv_w[1:2, :] + nxt * conv_w[2:3, :])
        xs[pl.ds(l0, ck), :] = y[:, :gw].astype(BF16)
        if with_y:
            xf[pl.ds(l0, ck), :] = y[:, :gw]
        bs[pl.ds(l0, ck), :] = y[:, gw:gw + SSM_STATE].astype(BF16)
        cs[pl.ds(l0, ck), :] = y[:, gw + SSM_STATE:].astype(BF16)
        return y[:, :gw], y[:, gw:gw + SSM_STATE].astype(BF16)

    ri = lax.broadcasted_iota(jnp.int32, (ck, ck), 0)
    ci = lax.broadcasted_iota(jnp.int32, (ck, ck), 1)
    lower = ci <= ri
    upper = ci >= ri
    eye = (ci == ri).astype(BF16)
    dt_row = _softplus(dt_ref[0, 0] + bias_ref[0])
    a_row = dt_row * a_ref[0]
    cum_f = dot_hi(a_row, upper.astype(F32))
    cum_b = dot_hi(a_row, lower.astype(F32))
    tot = dot_hi(a_row, jnp.ones((ck, ck), F32))
    d_row = jnp.where((ri % nhd) < nh, cum_f, cum_b)
    wrow[...] = jnp.exp(tot - d_row) * dt_row
    texp = jnp.exp(tot)
    row4 = lax.broadcasted_iota(jnp.int32, (ck, 2 * gw), 0)
    lane4 = lax.broadcasted_iota(jnp.int32, (ck, 2 * gw), 1) // SSM_HEAD_DIM
    tmask = jnp.where(row4 % nhd == lane4, jnp.concatenate([texp] * (2 * gw // ck), axis=1), 0.0)
    pick = (lax.broadcasted_iota(jnp.int32, (dec.shape[0], ck), 1) // nhd
            == lax.broadcasted_iota(jnp.int32, (dec.shape[0], ck), 0)).astype(F32)
    dec[...] = dot_hi(pick, tmask)
    if with_y:
        drow[...] = d_row
        dtrow[...] = dt_row
        for k, part in enumerate(_split3(d_row)):
            dsplit[k] = part.astype(F32)
        for k, part in enumerate(_split3(jnp.exp(d_row))[:2]):
            esplit[k] = part.astype(F32)

    def spread(src, r0, width):
        rows = jnp.concatenate([jnp.broadcast_to(src[pl.ds(r0 + hd, 1), :], (width, ck)) for hd in range(nhd)], axis=0)
        return _dot_nt(eye, rows.astype(BF16))

    def chunk_states(c, carry):
        xc, b_c = conv_silu_chunk(c)
        wx = spread(wrow, c * nhd, SSM_HEAD_DIM)
        xw = jnp.concatenate([(xc * wx[:, :gw]).astype(BF16), (xc * wx[:, gw:]).astype(BF16)], axis=1)
        contrib[c] = _dot_tn(b_c, xw)
        return carry

    lax.fori_loop(0, nc, chunk_states, 0, unroll=2)

    def fwd_chain(c, s):
        if with_y:
            sfa[c] = s
        return s * dec[pl.ds(c, 1), :gw] + contrib[c, :, :gw]

    def bwd_chain(i, s):
        c = nc - 1 - i
        if with_y:
            sba[c] = s
        return s * dec[pl.ds(c, 1), gw:] + contrib[c, :, gw:]

    if with_y:
        s_f0 = sf0_ref[0, 0]
        s_b0 = sb0_ref[0, 0]
    else:
        s_f0 = jnp.zeros((SSM_STATE, gw), F32)
        s_b0 = s_f0
    s_f = lax.fori_loop(0, nc, fwd_chain, s_f0)
    s_b = lax.fori_loop(0, nc, bwd_chain, s_b0)
    if not with_y:
        sf_out[0, 0] = s_f
        sb_out[0, 0] = s_b
        return

    lane_head = lax.broadcasted_iota(jnp.int32, (ck, gw), 1) // SSM_HEAD_DIM

    def out_step(c, carry):
        l0 = pl.multiple_of(c * ck, ck)
        r0 = c * nhd
        d_b = spread(dsplit.at[0], r0, ck) + spread(dsplit.at[1], r0, ck) + spread(dsplit.at[2], r0, ck)
        e_b = spread(esplit.at[0], r0, SSM_HEAD_DIM) + spread(esplit.at[1], r0, SSM_HEAD_DIM)
        b_c = bs[pl.ds(l0, ck), :]
        c_c = cs[pl.ds(l0, ck), :]
        x_c = xs[pl.ds(l0, ck), :]
        g = _dot_nt(c_c, b_c)
        acc = jnp.zeros((ck, gw), F32)
        for j in range(nh):
            jb = nh + j
            lf = jnp.exp(jnp.where(lower, d_b[:, j * ck:(j + 1) * ck] - drow[pl.ds(r0 + j, 1), :], -1e30)) * dtrow[pl.ds(r0 + j, 1), :]
            lb = jnp.exp(jnp.where(upper, d_b[:, jb * ck:(jb + 1) * ck] - drow[pl.ds(r0 + jb, 1), :], -1e30)) * dtrow[pl.ds(r0 + jb, 1), :]
            m = (g * (lf + lb)).astype(BF16)
            acc = acc + _dot(m, jnp.where(lane_head == j, x_c, jnp.zeros_like(x_c)))
        acc = acc + _dot(c_c, sfa[c].astype(BF16)) * e_b[:, :gw]
        acc = acc + _dot(c_c, sba[c].astype(BF16)) * e_b[:, gw:]
        y = acc + xf[pl.ds(l0, ck), :] * dsk_ref[...]
        y = y * _silu(z_ref[0, pl.ds(l0, ck), :].astype(F32))
        ms = jnp.mean(y * y, axis=-1, keepdims=True)
        y_ref[0, pl.ds(l0, ck), :] = (y * lax.rsqrt(ms + EPS) * gn_ref[...]).astype(y_ref.dtype)
        return carry

    lax.fori_loop(0, nc, out_step, 0, unroll=2)


def _ssd(proj, x_blk0, z_blk0, dt_raw, conv_w, conv_b, dt_bias, a_log, d_skip, gate_norm, init):
    b, length, _ = proj.shape
    nc = length // SSM_CHUNK
    ng, nh, gw, st = SSM_GROUPS, HEADS_PER_GROUP, GROUP_WIDTH, SSM_STATE
    with_y = init is not None
    b_blk0 = x_blk0 * (gw // st) + SSM_D_INNER // st
    c_blk0 = b_blk0 + ng
    ck = SSM_CHUNK
    nhd = 2 * nh
    assert nc * nhd <= ck
    dtg = dt_raw.reshape(b, nc, ck, 2, ng, nh).transpose(0, 4, 1, 3, 5, 2).reshape(b, ng, nc * nhd, ck)
    dtg = jnp.pad(dtg, ((0, 0), (0, 0), (0, ck - nc * nhd), (0, 0)))
    per_row = lambda t: jnp.tile(t.reshape(2, ng, nh).transpose(1, 0, 2).reshape(ng, nhd), (1, ck // nhd)).reshape(ng, ck, 1)
    bias_r = per_row(dt_bias)
    a_r = per_row(-jnp.exp(a_log.astype(F32)))
    cw_x, cw_b, cw_c = conv_w[:, :SSM_D_INNER], conv_w[:, SSM_D_INNER:SSM_D_INNER + ng * st], conv_w[:, SSM_D_INNER + ng * st:]
    cb = conv_b.reshape(1, -1)
    cb_x, cb_b, cb_c = cb[:, :SSM_D_INNER], cb[:, SSM_D_INNER:SSM_D_INNER + ng * st], cb[:, SSM_D_INNER + ng * st:]
    in_specs = [
        pl.BlockSpec((1, length, gw), lambda bi, g: (bi, 0, x_blk0 + g)),
        pl.BlockSpec((1, length, st), lambda bi, g: (bi, 0, b_blk0 + g)),
        pl.BlockSpec((1, length, st), lambda bi, g: (bi, 0, c_blk0 + g)),
        pl.BlockSpec((3, gw), lambda bi, g: (0, g)),
        pl.BlockSpec((1, gw), lambda bi, g: (0, g)),
        pl.BlockSpec((3, st), lambda bi, g: (0, g)),
        pl.BlockSpec((1, st), lambda bi, g: (0, g)),
        pl.BlockSpec((3, st), lambda bi, g: (0, g)),
        pl.BlockSpec((1, st), lambda bi, g: (0, g)),
        pl.BlockSpec((1, 1, ck, ck), lambda bi, g: (bi, g, 0, 0)),
        pl.BlockSpec((1, ck, 1), lambda bi, g: (g, 0, 0)),
        pl.BlockSpec((1, ck, 1), lambda bi, g: (g, 0, 0)),
    ]
    args = [proj, proj, proj, cw_x, cb_x, cw_b, cb_b, cw_c, cb_c, dtg, bias_r, a_r]
    scratch = [pltpu.VMEM((length, gw), BF16), pltpu.VMEM((length, st), BF16), pltpu.VMEM((length, st), BF16),
               pltpu.VMEM((nc, st, 2 * gw), F32), pltpu.VMEM((ck // nhd, 2 * gw), F32), pltpu.VMEM((ck, ck), F32)]
    state_spec = pl.BlockSpec((1, 1, st, gw), lambda bi, g: (bi, g, 0, 0))
    state_shape = jax.ShapeDtypeStruct((b, ng, st, gw), F32)
    if with_y:
        in_specs += [
            pl.BlockSpec((1, length, gw), lambda bi, g: (bi, 0, z_blk0 + g)),
            pl.BlockSpec((1, gw), lambda bi, g: (0, g)),
            pl.BlockSpec((1, gw), lambda bi, g: (0, g)),
            state_spec, state_spec,
        ]
        args += [proj, jnp.repeat(d_skip, SSM_HEAD_DIM).reshape(1, -1), gate_norm.reshape(1, -1), init[0], init[1]]
        out_specs = pl.BlockSpec((1, length, gw), lambda bi, g: (bi, 0, g))
        out_shape = jax.ShapeDtypeStruct((b, length, SSM_D_INNER), BF16)
        scratch += [pltpu.VMEM((length, gw), F32), pltpu.VMEM((nc, st, gw), F32), pltpu.VMEM((nc, st, gw), F32),
                    pltpu.VMEM((ck, ck), F32), pltpu.VMEM((ck, ck), F32), pltpu.VMEM((3, ck, ck), F32), pltpu.VMEM((2, ck, ck), F32)]
    else:
        out_specs = [state_spec, state_spec]
        out_shape = [state_shape, state_shape]
    return pl.pallas_call(
        functools.partial(_ssd_kernel, with_y=with_y, nc=nc),
        grid=(b, ng),
        in_specs=in_specs,
        out_specs=out_specs,
        out_shape=out_shape,
        scratch_shapes=scratch,
        compiler_params=_cparams(("parallel", "parallel")),
        name="ssd_main" if with_y else "ssd_ctx",
    )(*args)


def _proj_res_kernel(*refs, n_lhs):
    a_refs, w_refs = refs[:n_lhs], refs[n_lhs:2 * n_lhs]
    b_ref, gate_ref, res_ref, o_ref = refs[2 * n_lhs:]
    acc = _dot(a_refs[0][0], w_refs[0][...])
    for a_ref, w_ref in zip(a_refs[1:], w_refs[1:]):
        acc = acc + _dot(a_ref[0], w_ref[...])
    o_ref[0] = res_ref[0] + gate_ref[0] * (acc + b_ref[...])


def _proj_residual(lhs, w, bias, gate, resid, tm=1024, tn=1024):
    b, s, k = lhs[0].shape
    n = w.shape[1]
    n_lhs = len(lhs)
    in_specs = [pl.BlockSpec((1, tm, k), lambda bi, i, j: (bi, i, 0)) for _ in lhs]
    in_specs += [pl.BlockSpec((k, tn), functools.partial(lambda bi, i, j, t: (t, j), t=t)) for t in range(n_lhs)]
    in_specs += [
        pl.BlockSpec((1, tn), lambda bi, i, j: (0, j)),
        pl.BlockSpec((1, 1, tn), lambda bi, i, j: (bi, 0, j)),
        pl.BlockSpec((1, tm, tn), lambda bi, i, j: (bi, i, j)),
    ]
    return pl.pallas_call(
        functools.partial(_proj_res_kernel, n_lhs=n_lhs),
        grid=(b, s // tm, n // tn),
        in_specs=in_specs,
        out_specs=pl.BlockSpec((1, tm, tn), lambda bi, i, j: (bi, i, j)),
        out_shape=jax.ShapeDtypeStruct((b, s, n), F32),
        compiler_params=_cparams(("parallel", "parallel", "parallel")),
        name="proj_residual",
    )(*lhs, *([w] * n_lhs), bias.reshape(1, n), gate.reshape(b, 1, n), resid)


def _hy_filter_kernel(tw_ref, f_ref, w1t, w1c, w1s, b1, w2, b2, w3, b3, fr, w4_ref, dl_ref, o_ref, h_scr):
    dot_hi = lambda a, b: jnp.dot(a, b, precision=HI, preferred_element_type=F32)

    @pl.when(pl.program_id(0) == 0)
    def _():
        t = tw_ref[:, 0:1]
        ang = tw_ref[:, 1:2] * f_ref[...]
        pre = t * w1t[...] + dot_hi(jnp.cos(ang), w1c[...]) + dot_hi(-jnp.sin(ang), w1s[...]) + b1[...]
        h = jnp.sin(fr[...] * pre)
        h = jnp.sin(fr[...] * (dot_hi(h, w2[...]) + b2[...]))
        h_scr[...] = jnp.sin(fr[...] * (dot_hi(h, w3[...]) + b3[...]))

    o_ref[...] = dot_hi(h_scr[...], w4_ref[...]) * jnp.exp(-tw_ref[:, 0:1] * dl_ref[...])


def _hyena_filters(length, w1, b1, w2, b2, w3, b3, w4, freq):
    bands = (HY_EMB - 1) // 2
    fw = HY_FILTER_WIDTH
    t = jnp.linspace(0.0, 1.0, length, dtype=F32)
    w = 2 * math.pi * jnp.arange(length, dtype=F32) / length
    tw = jnp.stack([t, w], axis=1)
    f = jnp.linspace(1e-4, bands - 1, bands, dtype=F32).reshape(1, bands)
    min_decay = math.log(HY_TARGET) / HY_SLOW_PCT
    max_decay = math.log(HY_TARGET) / HY_FAST_PCT
    deltas = jnp.abs(jnp.linspace(min_decay, max_decay, D_MODEL, dtype=F32))
    dl = jnp.concatenate([deltas, deltas]).reshape(1, 2 * D_MODEL)
    tn = 1024
    small = lambda shape: pl.BlockSpec(shape, lambda j: (0, 0))
    row = lambda v: v.reshape(1, -1)
    return pl.pallas_call(
        _hy_filter_kernel,
        grid=(2 * D_MODEL // tn,),
        in_specs=[small((length, 2)), small((1, bands)), small((1, fw)), small((bands, fw)), small((bands, fw)), small((1, fw)),
                  small((fw, fw)), small((1, fw)), small((fw, fw)), small((1, fw)), small((1, fw)),
                  pl.BlockSpec((fw, tn), lambda j: (0, j)), pl.BlockSpec((1, tn), lambda j: (0, j))],
        out_specs=pl.BlockSpec((length, tn), lambda j: (0, j)),
        out_shape=jax.ShapeDtypeStruct((length, 2 * D_MODEL), F32),
        scratch_shapes=[pltpu.VMEM((length, fw), F32)],
        compiler_params=_cparams(("arbitrary",)),
        name="hyena_filters",
    )(tw, f, w1[0:1], w1[1:1 + bands], w1[1 + bands:], row(b1), w2, row(b2), w3, row(b3), row(freq), w4, dl)


def _hy_conv_kernel(x0_ref, x1_ref, v_ref, w0, b0, w1, b1, w2, b2, x0_out, u_out):
    length = x0_ref.shape[1]

    def conv(raw_ref, w_ref, b_ref):
        x = raw_ref[0].astype(F32)
        rows = lax.broadcasted_iota(jnp.int32, x.shape, 0)
        xm1 = jnp.where(rows == 0, 0.0, pltpu.roll(x, 1, 0))
        xp1 = jnp.where(rows == length - 1, 0.0, pltpu.roll(x, length - 1, 0))
        return b_ref[...] + xm1 * w_ref[0:1, :] + x * w_ref[1:2, :] + xp1 * w_ref[2:3, :]

    x0_out[0] = conv(x0_ref, w0, b0).astype(x0_out.dtype)
    u_out[0] = (conv(v_ref, w2, b2) * conv(x1_ref, w1, b1)).astype(u_out.dtype)


def _hyena_conv(proj, conv_w, conv_b):
    b, length, _ = proj.shape
    d = D_MODEL
    tn = 512
    nb = d // tn
    cb = conv_b.reshape(1, -1)
    seg = lambda k: pl.BlockSpec((1, length, tn), lambda bi, j: (bi, 0, k * nb + j))
    wseg = lambda k: pl.BlockSpec((3, tn), lambda bi, j: (0, k * nb + j))
    bseg = lambda k: pl.BlockSpec((1, tn), lambda bi, j: (0, k * nb + j))
    out = pl.BlockSpec((1, length, tn), lambda bi, j: (bi, 0, j))
    return pl.pallas_call(
        _hy_conv_kernel,
        grid=(b, nb),
        in_specs=[seg(0), seg(1), seg(2), wseg(0), bseg(0), wseg(1), bseg(1), wseg(2), bseg(2)],
        out_specs=[out, out],
        out_shape=[jax.ShapeDtypeStruct((b, length, d), BF16)] * 2,
        compiler_params=_cparams(("parallel", "parallel")),
        name="hyena_conv",
    )(proj, proj, proj, conv_w, cb, conv_w, cb, conv_w, cb)


def _dft_matrices(length):
    n = 2 * length
    f = lax.broadcasted_iota(jnp.int32, (length, length), 0)
    t = lax.broadcasted_iota(jnp.int32, (length, length), 1)
    ang = ((f * t) % n).astype(F32) * (2 * math.pi / n)
    sign = (1 - 2 * (t % 2)).astype(F32)
    fwd_c = jnp.cos(ang)
    fwd_s = jnp.where(f == 0, sign, -jnp.sin(ang))
    fwd = jnp.concatenate([fwd_c, fwd_s], axis=0).astype(BF16)
    wgt = jnp.where(f == 0, 1.0, 2.0) / n
    inv = jnp.concatenate([(fwd_c * wgt).T, (fwd_s * wgt).T], axis=1).astype(BF16)
    return fwd, inv


def _mm_kernel(a_ref, b_ref, o_ref):
    o_ref[...] = _dot(a_ref[...], b_ref[...]).astype(o_ref.dtype)


def _matmul(a, b, out_dtype=F32, tm=1024, tn=1024):
    m, k = a.shape
    n = b.shape[1]
    return pl.pallas_call(
        _mm_kernel,
        grid=(n // tn, m // tm),
        in_specs=[pl.BlockSpec((tm, k), lambda j, i: (i, 0)), pl.BlockSpec((k, tn), lambda j, i: (0, j))],
        out_specs=pl.BlockSpec((tm, tn), lambda j, i: (i, j)),
        out_shape=jax.ShapeDtypeStruct((m, n), out_dtype),
        compiler_params=_cparams(("parallel", "parallel")),
        name="matmul",
    )(a, b)


def _filter_spectrum(kk, fwd):
    length = kk.shape[0]
    d = D_MODEL
    k_f, k_b = kk[:, :d], kk[:, d:]
    k_fp = k_f.at[0].add(k_b[0])
    k_bp = k_b.at[0].set(0.0)
    spec = _matmul(fwd, jnp.concatenate([k_fp, k_bp], axis=1).astype(BF16))
    a_re, a_im, b_re, b_im = spec[:length, :d], spec[length:, :d], spec[:length, d:], spec[length:, d:]
    nz = (jnp.arange(length) != 0)[:, None]
    k_re = a_re + b_re
    k_im = jnp.where(nz, a_im - b_im, a_im + b_im)
    return jnp.stack([k_re, jnp.where(nz, k_im, 0.0), jnp.where(nz, k_re, k_im)])


def _dft_fwd_kernel(fc_ref, fs_ref, u_ref, k_ref, y_ref):
    u = u_ref[0]
    re = _dot(fc_ref[...], u)
    im = _dot(fs_ref[...], u)
    y_ref[0, 0] = (re * k_ref[0] - im * k_ref[1]).astype(y_ref.dtype)
    y_ref[0, 1] = (re * k_ref[1] + im * k_ref[2]).astype(y_ref.dtype)


def _dft_forward(u, fwd, coef, tm=1024, tn=512):
    b, length, d = u.shape
    ni = length // tm
    return pl.pallas_call(
        _dft_fwd_kernel,
        grid=(d // tn, ni, b),
        in_specs=[
            pl.BlockSpec((tm, length), lambda j, i, bi: (i, 0)),
            pl.BlockSpec((tm, length), lambda j, i, bi: (ni + i, 0)),
            pl.BlockSpec((1, length, tn), lambda j, i, bi: (bi, 0, j)),
            pl.BlockSpec((3, tm, tn), lambda j, i, bi: (0, i, j)),
        ],
        out_specs=pl.BlockSpec((1, 2, tm, tn), lambda j, i, bi: (bi, 0, i, j)),
        out_shape=jax.ShapeDtypeStruct((b, 2, length, d), BF16),
        compiler_params=_cparams(("parallel", "parallel", "parallel")),
        name="dft_forward",
    )(fwd, fwd, u, coef)


def _dft_inv_kernel(g_ref, y_ref, x0_ref, u_ref, fb_ref, o_ref):
    conv = _dot(g_ref[...], y_ref[0])
    o_ref[0] = (x0_ref[0].astype(F32) * (conv + u_ref[0].astype(F32) * fb_ref[...])).astype(o_ref.dtype)


def _dft_inverse(y, inv, x0, u, filt_bias, tm=512, tn=512):
    b, n2, d = y.shape
    length = n2 // 2
    return pl.pallas_call(
        _dft_inv_kernel,
        grid=(d // tn, length // tm, b),
        in_specs=[
            pl.BlockSpec((tm, n2), lambda j, i, bi: (i, 0)),
            pl.BlockSpec((1, n2, tn), lambda j, i, bi: (bi, 0, j)),
            pl.BlockSpec((1, tm, tn), lambda j, i, bi: (bi, i, j)),
            pl.BlockSpec((1, tm, tn), lambda j, i, bi: (bi, i, j)),
            pl.BlockSpec((1, tn), lambda j, i, bi: (0, j)),
        ],
        out_specs=pl.BlockSpec((1, tm, tn), lambda j, i, bi: (bi, i, j)),
        out_shape=jax.ShapeDtypeStruct((b, length, d), BF16),
        compiler_params=_cparams(("parallel", "parallel", "parallel")),
        name="dft_inverse",
    )(inv, y, x0, u, filt_bias.reshape(1, d))


PACK_HI = 0xFFFF0000
TILE_SUBLANES = 8


def _pack_pairs(lo, hi):
    lo_b = pltpu.bitcast(lo.astype(BF16).astype(F32), jnp.uint32)
    hi_b = pltpu.bitcast(hi.astype(BF16).astype(F32), jnp.uint32)
    return (lo_b >> 16) | (hi_b & jnp.uint32(PACK_HI))


def _unpack_pairs(w):
    return pltpu.bitcast(w << 16, F32), pltpu.bitcast(w & jnp.uint32(PACK_HI), F32)


def _router_kernel(x_ref, g_ref, sh_ref, sc_ref, wr_ref, br_ref, hp_ref, idx_ref, gate_ref, rank_ref, cnt_ref, run):
    tm = x_ref.shape[1]
    half = x_ref.shape[2] // 2
    first = (pl.program_id(0) == 0) & (pl.program_id(1) == 0)

    @pl.when(first)
    def _():
        run[...] = jnp.zeros_like(run)

    x = x_ref[0]
    ms = jnp.mean(x * x, axis=-1, keepdims=True)
    h = x * lax.rsqrt(ms + EPS) * g_ref[...] * (1.0 + sc_ref[0]) + sh_ref[0]
    hp_ref[0] = _pack_pairs(h[:, :half], h[:, half:])

    logits = jnp.dot(h, wr_ref[...], precision=HI, preferred_element_type=F32) + br_ref[...]
    lane_i = lax.broadcasted_iota(jnp.int32, logits.shape, 1)
    lane = lane_i.astype(F32)
    work = jnp.where(lane_i < N_EXPERTS, logits, -jnp.inf)
    li = lax.broadcasted_iota(jnp.int32, (tm, tm), 0)
    ki = lax.broadcasted_iota(jnp.int32, (tm, tm), 1)
    strict_lower = (ki < li).astype(BF16)
    sels, tops = [], []
    for _ in range(TOP_K):
        m = jnp.max(work, axis=-1, keepdims=True)
        first_idx = jnp.min(jnp.where(work == m, lane, float(LANES)), axis=-1, keepdims=True)
        sel = lane == first_idx
        sels.append(sel)
        tops.append((m, first_idx))
        work = jnp.where(sel, -jnp.inf, work)
    chosen = sels[0] | sels[1] | sels[2] | sels[3]
    before = _dot(strict_lower, chosen.astype(BF16)) + run[...]
    run[...] = run[...] + jnp.sum(chosen.astype(F32), axis=0, keepdims=True)
    den = sum(jnp.exp(m - tops[0][0]) for m, _ in tops)
    idx_o = jnp.zeros(logits.shape, jnp.int32)
    gate_o = jnp.zeros(logits.shape, F32)
    rank_o = jnp.zeros(logits.shape, jnp.int32)
    for k, (sel, (m, first_idx)) in enumerate(zip(sels, tops)):
        rank = jnp.sum(jnp.where(sel, before, 0.0), axis=-1, keepdims=True).astype(jnp.int32)
        idx_o = jnp.where(lane_i == k, first_idx.astype(jnp.int32), idx_o)
        gate_o = jnp.where(lane_i == k, jnp.exp(m - tops[0][0]) / den, gate_o)
        rank_o = jnp.where(lane_i == k, rank, rank_o)
    idx_ref[0] = idx_o
    gate_ref[0] = gate_o
    rank_ref[0] = rank_o
    cnt_ref[...] = run[...]


def _moe_route(x, g, shift, scale, w_r, b_r, tm=512):
    b, s, d = x.shape
    wr = jnp.zeros((d, LANES), F32).at[:, :N_EXPERTS].set(w_r)
    br = jnp.zeros((1, LANES), F32).at[0, :N_EXPERTS].set(b_r)
    tok = lambda n: pl.BlockSpec((1, tm, n), lambda bi, i: (bi, i, 0))
    return pl.pallas_call(
        _router_kernel,
        grid=(b, s // tm),
        in_specs=[
            tok(d),
            pl.BlockSpec((1, d), lambda bi, i: (0, 0)),
            pl.BlockSpec((1, 1, d), lambda bi, i: (bi, 0, 0)),
            pl.BlockSpec((1, 1, d), lambda bi, i: (bi, 0, 0)),
            pl.BlockSpec((d, LANES), lambda bi, i: (0, 0)),
            pl.BlockSpec((1, LANES), lambda bi, i: (0, 0)),
        ],
        out_specs=[tok(d // 2), tok(LANES), tok(LANES), tok(LANES), pl.BlockSpec((1, LANES), lambda bi, i: (0, 0))],
        out_shape=[
            jax.ShapeDtypeStruct((b, s, d // 2), jnp.uint32),
            jax.ShapeDtypeStruct((b, s, LANES), jnp.int32),
            jax.ShapeDtypeStruct((b, s, LANES), F32),
            jax.ShapeDtypeStruct((b, s, LANES), jnp.int32),
            jax.ShapeDtypeStruct((1, LANES), F32),
        ],
        scratch_shapes=[pltpu.VMEM((1, LANES), F32)],
        compiler_params=_cparams(("arbitrary", "arbitrary")),
        name="moe_route",
    )(x, g.reshape(1, d), shift.reshape(b, 1, d), scale.reshape(b, 1, d), wr, br)


MOE_PITCH = TILE_SUBLANES + 1


def _ffn_kernel(be_ref, na_ref, dst_prv, tok_cur, tok_nxt, hp_ref, w1g_ref, w1u_ref, b1g_ref, b1u_ref, w2_ref, b2_ref, y_ref,
                xg, stg, xb, acc, zeros, gsem, usem, zsem):
    i = pl.program_id(0)
    f = pl.program_id(1)
    nb = pl.num_programs(0)
    nf = pl.num_programs(1)
    tm = MOE_ROWS
    sub = hp_ref.shape[1]
    half = sub * LANES
    per = tm // MOE_N_FF
    slot = i % 2
    active = i < na_ref[0]

    def gather(tok_ref, to_slot, r):
        return pltpu.make_async_copy(hp_ref.at[tok_ref[0, 0, r]], xg.at[to_slot, pl.ds(r * MOE_PITCH, sub), :], gsem.at[to_slot])

    def scatter(dst_ref, from_slot, r):
        return pltpu.make_async_copy(stg.at[from_slot, pl.ds(r * MOE_PITCH, sub), :], y_ref.at[dst_ref[0, 0, r]], usem.at[from_slot])

    def wait_rows(buf, sem, s):
        pltpu.make_async_copy(buf.at[1 - s, pl.ds(0, tm * sub), :], buf.at[s, pl.ds(0, tm * sub), :], sem.at[s]).wait()

    @pl.when((i == 0) & (f == 0))
    def _():
        stg[...] = jnp.zeros_like(stg)
        zrows = zeros.shape[0]
        zeros[...] = jnp.zeros_like(zeros)
        dump0 = y_ref.shape[0] - 2 * tm
        fills = [pltpu.make_async_copy(zeros, y_ref.at[pl.ds(dump0 + c * zrows, zrows)], zsem) for c in range(2 * tm // zrows)]
        for cp in fills:
            cp.start()
        for cp in fills:
            cp.wait()
        for r in range(tm):
            gather(tok_cur, 0, r).start()

    @pl.when(f == 0)
    def _():
        wait_rows(xg, gsem, slot)
        for j in range(sub):
            lo, hi = _unpack_pairs(xg[slot, pl.ds(j, tm, stride=MOE_PITCH), :])
            xb[:, j * LANES:(j + 1) * LANES] = lo.astype(BF16)
            xb[:, half + j * LANES:half + (j + 1) * LANES] = hi.astype(BF16)
        acc[...] = jnp.broadcast_to(b2_ref[0, 0], acc.shape)

    def move_rows():
        for rr in range(per):
            r = f * per + rr
            gather(tok_nxt, 1 - slot, r).start()
            scatter(dst_prv, 1 - slot, r).start()

    @pl.when(active)
    def _():
        move_rows()
        x = xb[...]
        g = _dot(x, w1g_ref[0, 0].astype(BF16)) + b1g_ref[0, 0]
        u = _dot(x, w1u_ref[0, 0].astype(BF16)) + b1u_ref[0, 0]
        g = jnp.minimum(g, SWIGLU_LIMIT)
        u = jnp.clip(u, -SWIGLU_LIMIT, SWIGLU_LIMIT)
        a = (u + 1.0) * g / (1.0 + jnp.exp(-SWIGLU_ALPHA * g))
        acc[...] += _dot(a.astype(BF16), w2_ref[0, 0].astype(BF16))

    @pl.when(jnp.logical_not(active))
    def _():
        move_rows()

    @pl.when(f == nf - 1)
    def _():
        @pl.when(i >= 1)
        def _():
            wait_rows(stg, usem, slot)
        for j in range(sub):
            stg[slot, pl.ds(j, tm, stride=MOE_PITCH), :] = _pack_pairs(acc[:, j * LANES:(j + 1) * LANES],
                                                                      acc[:, half + j * LANES:half + (j + 1) * LANES])

    @pl.when((i == nb - 1) & (f == nf - 1))
    def _():
        wait_rows(xg, gsem, 1 - slot)
        wait_rows(stg, usem, 1 - slot)


def _moe_experts(layer, hp_tiles, slot_src, block_e, n_active, w1, b1, w2, b2):
    n_tok, sub, _ = hp_tiles.shape
    d = 2 * sub * LANES
    tm, tf = MOE_ROWS, MOE_FF_TILE
    nf = MOE_N_FF
    n_real = slot_src.shape[0] // tm
    nb = n_real + 2
    n_rows = TOP_K * n_tok + 2 * tm
    ext = jnp.full(((nb + 2) * tm,), -1, jnp.int32).at[tm:(n_real + 1) * tm].set(slot_src)
    slot = jnp.arange(ext.shape[0], dtype=jnp.int32)
    tok_ext = (jnp.maximum(ext, 0) >> 2).reshape(nb + 2, 1, tm)
    dump = TOP_K * n_tok + ((slot // tm) % 2) * tm + slot % tm
    dst_ext = jnp.where(ext < 0, dump, (ext & (TOP_K - 1)) * n_tok + (ext >> 2)).reshape(nb + 2, 1, tm)

    def blk(i, na):
        return jnp.minimum(i, na[0] - 1)

    def ff(i, f, na):
        return jnp.where(i < na[0], f, nf - 1)

    smem = lambda off: pl.BlockSpec((1, 1, tm), lambda i, f, be, na: (i + off, 0, 0), memory_space=pltpu.SMEM)
    grid_spec = pltpu.PrefetchScalarGridSpec(
        num_scalar_prefetch=2,
        grid=(nb, nf),
        in_specs=[
            smem(0), smem(1), smem(2),
            pl.BlockSpec(memory_space=pl.ANY),
            pl.BlockSpec((1, 1, d, tf), lambda i, f, be, na: (layer, be[blk(i, na)], 0, ff(i, f, na))),
            pl.BlockSpec((1, 1, d, tf), lambda i, f, be, na: (layer, be[blk(i, na)], 0, nf + ff(i, f, na))),
            pl.BlockSpec((1, 1, 1, tf), lambda i, f, be, na: (layer, be[blk(i, na)], 0, ff(i, f, na))),
            pl.BlockSpec((1, 1, 1, tf), lambda i, f, be, na: (layer, be[blk(i, na)], 0, nf + ff(i, f, na))),
            pl.BlockSpec((1, 1, tf, d), lambda i, f, be, na: (layer, be[blk(i, na)], ff(i, f, na), 0)),
            pl.BlockSpec((1, 1, 1, d), lambda i, f, be, na: (layer, be[blk(i, na)], 0, 0)),
        ],
        out_specs=pl.BlockSpec(memory_space=pl.ANY),
        scratch_shapes=[pltpu.VMEM((2, tm * MOE_PITCH, LANES), jnp.uint32), pltpu.VMEM((2, tm * MOE_PITCH, LANES), jnp.uint32),
                        pltpu.VMEM((tm, d), BF16), pltpu.VMEM((tm, d), F32), pltpu.VMEM((64, sub, LANES), jnp.uint32),
                        pltpu.SemaphoreType.DMA((2,)), pltpu.SemaphoreType.DMA((2,)), pltpu.SemaphoreType.DMA(())],
    )
    depth, ne = w1.shape[:2]
    be_ext = jnp.concatenate([block_e, jnp.full((2,), N_EXPERTS - 1, jnp.int32)])
    return pl.pallas_call(
        _ffn_kernel,
        grid_spec=grid_spec,
        out_shape=jax.ShapeDtypeStruct((n_rows, sub, LANES), jnp.uint32),
        compiler_params=_cparams(("arbitrary", "arbitrary")),
        name="moe_experts",
    )(be_ext, n_active, dst_ext, tok_ext, tok_ext, hp_tiles, w1, w1, b1.reshape(depth, ne, 1, -1), b1.reshape(depth, ne, 1, -1), w2,
      b2.reshape(depth, ne, 1, -1))


def _combine_kernel(*refs):
    y_refs, (gate_ref, x_ref, gm_ref, o_ref) = refs[:TOP_K], refs[TOP_K:]
    half = y_refs[0].shape[1]
    gates = gate_ref[0]
    lo_acc = jnp.zeros((x_ref.shape[1], half), F32)
    hi_acc = lo_acc
    for k in range(TOP_K):
        lo, hi = _unpack_pairs(y_refs[k][...])
        lo_acc = lo_acc + gates[:, k:k + 1] * lo
        hi_acc = hi_acc + gates[:, k:k + 1] * hi
    o_ref[0, :, :half] = x_ref[0, :, :half] + gm_ref[0, :, :half] * lo_acc
    o_ref[0, :, half:] = x_ref[0, :, half:] + gm_ref[0, :, half:] * hi_acc


def _moe_combine(y_tok, gates, x, gmod, tc=256):
    b, s, d = x.shape
    nt = s // tc
    plane = lambda k: pl.BlockSpec((tc, d // 2), lambda bi, i: (k * b * nt + bi * nt + i, 0))
    return pl.pallas_call(
        _combine_kernel,
        grid=(b, nt),
        in_specs=[plane(k) for k in range(TOP_K)] + [
            pl.BlockSpec((1, tc, LANES), lambda bi, i: (bi, i, 0)),
            pl.BlockSpec((1, tc, d), lambda bi, i: (bi, i, 0)),
            pl.BlockSpec((1, 1, d), lambda bi, i: (bi, 0, 0)),
        ],
        out_specs=pl.BlockSpec((1, tc, d), lambda bi, i: (bi, i, 0)),
        out_shape=jax.ShapeDtypeStruct((b, s, d), F32),
        compiler_params=_cparams(("parallel", "parallel")),
        name="moe_combine",
    )(*([y_tok] * TOP_K), gates, x, gmod.reshape(b, 1, d))


def _moe_ffn(layer, x, g, shift, scale, gmod, w_r, b_r, w1, b1, w2, b2):
    b, s, d = x.shape
    n = b * s
    hp, idx, gates, rank, counts = _moe_route(x, g, shift, scale, w_r, b_r)
    counts = counts[0, :N_EXPERTS].astype(jnp.int32)
    padded = (counts + MOE_ROWS - 1) // MOE_ROWS * MOE_ROWS
    pad_end = jnp.cumsum(padded)
    pad_start = pad_end - padded
    idx4 = idx.reshape(n, LANES)[:, :TOP_K]
    onehot = idx4[:, :, None] == jnp.arange(N_EXPERTS, dtype=jnp.int32)
    pos = jnp.sum(jnp.where(onehot, pad_start, 0), axis=-1) + rank.reshape(n, LANES)[:, :TOP_K]
    n_blocks = n * TOP_K // MOE_ROWS + N_EXPERTS
    slot_ids = jnp.arange(n * TOP_K, dtype=jnp.int32)
    slot_src = jnp.full((n_blocks * MOE_ROWS,), -1, jnp.int32).at[pos.reshape(-1)].set(slot_ids)
    block_start = jnp.arange(n_blocks, dtype=jnp.int32) * MOE_ROWS
    block_e = jnp.minimum(jnp.sum(pad_end[None, :] <= block_start[:, None], axis=1), N_EXPERTS - 1).astype(jnp.int32)
    n_active = (pad_end[-1:] // MOE_ROWS).astype(jnp.int32)
    hp_tiles = hp.reshape(n, d // 2 // LANES, LANES)
    y_tok = _moe_experts(layer, hp_tiles, slot_src, block_e, n_active, w1, b1, w2, b2)
    return _moe_combine(y_tok.reshape(y_tok.shape[0], d // 2), gates, x, gmod)


def kernel(x, c, ctx, c_ctx, ada_w, ada_b, norm_mix, norm_ffn, ev_w_in, ev_conv_w, ev_conv_b, ev_q_norm, ev_k_norm, ev_rpb, ev_a_log, ev_dt_bias, ev_d_skip, ev_gate_norm, ev_w_out, od_w_in, od_b_in, od_conv_w, od_conv_b, od_filt_w1, od_filt_b1, od_filt_w2, od_filt_b2, od_filt_w3, od_filt_b3, od_filt_w4, od_filt_freq, od_filt_bias, od_w_out, od_b_out, moe_router_w, moe_router_b, moe_w1, moe_b1, moe_w2, moe_b2):
    b, s, d = x.shape
    assert ada_w.shape[0] == 2 and ev_w_in.shape[0] == 1 and od_w_in.shape[0] == 1 and d == D_MODEL
    cc = jnp.zeros((16, d), F32).at[:b].set(c).at[b].set(c_ctx)
    mod = _ada_mod(cc, ada_w, ada_b)
    chunk = lambda m, k: m[:, k * d:(k + 1) * d]
    moe = (moe_w1, moe_b1, moe_w2, moe_b2)

    m0 = mod[0, :b]
    mc0 = jnp.broadcast_to(mod[0, b:b + 1], (b, N_MOD * d))
    w_in = ev_w_in[0]
    n_main = 2 * NA_WIDTH + SSM_D_INNER + NA_WIDTH + SSM_XBC
    kv0 = NA_WIDTH + SSM_D_INNER
    w_main = w_in[:, :n_main].astype(BF16)
    w_ctx = w_in[:, kv0:n_main].astype(BF16)
    w_dt = jnp.zeros((d, LANES), BF16).at[:, :2 * SSM_HEADS].set(w_in[:, n_main:].astype(BF16))
    zb = lambda n: jnp.zeros((n,), F32)
    g0 = norm_mix[0]
    proj = _norm_mod_matmul(x, g0, chunk(m0, 0), chunk(m0, 1), w_main, zb(n_main), BF16, 1024, 1024)
    dt_raw = _norm_mod_matmul(x, g0, chunk(m0, 0), chunk(m0, 1), w_dt, zb(LANES), F32, 1024, LANES)[..., :2 * SSM_HEADS]
    proj_c = _norm_mod_matmul(ctx, g0, chunk(mc0, 0), chunk(mc0, 1), w_ctx, zb(n_main - kv0), BF16, 256, 1024)
    dt_raw_c = _norm_mod_matmul(ctx, g0, chunk(mc0, 0), chunk(mc0, 1), w_dt, zb(LANES), F32, 256, LANES)[..., :2 * SSM_HEADS]
    attn = _attention(proj, proj_c, ev_q_norm[0], ev_k_norm[0], _attn_bias_table(ev_rpb[0], s // GRID_W))
    ssm_args = (ev_conv_w[0], ev_conv_b[0], ev_dt_bias[0], ev_a_log[0])
    x_blk_c = 2 * NA_WIDTH // GROUP_WIDTH
    states = _ssd(proj_c, x_blk_c, None, dt_raw_c, *ssm_args, None, None, None)
    x_blk = (kv0 + 2 * NA_WIDTH) // GROUP_WIDTH
    z_blk = NA_WIDTH // GROUP_WIDTH
    y_ssm = _ssd(proj, x_blk, z_blk, dt_raw, *ssm_args, ev_d_skip[0], ev_gate_norm[0], states)
    x = _proj_residual([attn, y_ssm], ev_w_out[0].astype(BF16), zb(d), chunk(m0, 2), x)
    x = _moe_ffn(0, x, norm_ffn[0], chunk(m0, 3), chunk(m0, 4), chunk(m0, 5), moe_router_w[0], moe_router_b[0], *moe)

    m1 = mod[1, :b]
    proj_h = _norm_mod_matmul(x, norm_mix[1], chunk(m1, 0), chunk(m1, 1), od_w_in[0].astype(BF16), od_b_in[0], BF16, 1024, 1024)
    filt = _hyena_filters(s, od_filt_w1[0], od_filt_b1[0], od_filt_w2[0], od_filt_b2[0], od_filt_w3[0], od_filt_b3[0],
                          od_filt_w4[0], od_filt_freq[0])
    fwd, inv = _dft_matrices(s)
    coef = _filter_spectrum(filt, fwd)
    x0, u = _hyena_conv(proj_h, od_conv_w[0], od_conv_b[0])
    spec = _dft_forward(u, fwd, coef)
    y_h = _dft_inverse(spec.reshape(b, 2 * s, d), inv, x0, u, od_filt_bias[0])
    x = _proj_residual([y_h], od_w_out[0].astype(BF16), od_b_out[0], chunk(m1, 2), x)
    x = _moe_ffn(1, x, norm_ffn[1], chunk(m1, 3), chunk(m1, 4), chunk(m1, 5), moe_router_w[1], moe_router_b[1], *moe)
    return x
```

```python
import functools
import math

import jax
import jax.numpy as jnp
import numpy as np
from jax import lax
from jax.experimental import pallas as pl
from jax.experimental.pallas import tpu as pltpu

F32 = jnp.float32
BF16 = jnp.bfloat16
HI = lax.Precision.HIGHEST

D_MODEL = 2048
N_MOD = 6
EPS = 1e-6
NEG_INF = -1e9
GRID_W = 64
NA_HEADS = 16
NA_HEAD_DIM = 128
NA_WIDTH = NA_HEADS * NA_HEAD_DIM
NA_KH = 8
NA_KW = 16
SSM_D_INNER = 2048
SSM_HEAD_DIM = 64
SSM_HEADS = SSM_D_INNER // SSM_HEAD_DIM
SSM_GROUPS = 8
SSM_STATE = 128
SSM_CHUNK = 128
SSM_XBC = SSM_D_INNER + 2 * SSM_GROUPS * SSM_STATE
HEADS_PER_GROUP = SSM_HEADS // SSM_GROUPS
GROUP_WIDTH = HEADS_PER_GROUP * SSM_HEAD_DIM
HY_EMB = 33
HY_FILTER_WIDTH = 64
HY_TARGET = 1e-2
HY_FAST_PCT = 0.3
HY_SLOW_PCT = 1.5
N_EXPERTS = 32
TOP_K = 4
MOE_FF = 2048
SWIGLU_LIMIT = 7.0
SWIGLU_ALPHA = 1.702
MOE_ROWS = 1024
MOE_SUB_ROWS = 512
MOE_FF_TILE = 256
MOE_N_FF = MOE_FF // MOE_FF_TILE
LANES = 128
VMEM_LIMIT = 56 * 1024 * 1024


def _cparams(sem, vmem=VMEM_LIMIT):
    return pltpu.CompilerParams(dimension_semantics=sem, vmem_limit_bytes=vmem)


def _dot(a, b):
    return jnp.dot(a, b, preferred_element_type=F32)


def _dot_nt(a, b):
    return lax.dot_general(a, b, (((1,), (1,)), ((), ())), preferred_element_type=F32)


def _dot_tn(a, b):
    return lax.dot_general(a, b, (((0,), (0,)), ((), ())), preferred_element_type=F32)


def _silu(x):
    return x / (1.0 + jnp.exp(-x))


def _softplus(x):
    return jnp.maximum(x, 0.0) + jnp.log1p(jnp.exp(-jnp.abs(x)))


def _ada_kernel(c_ref, w_ref, b_ref, o_ref):
    sc = _silu(c_ref[...])
    hi = sc.astype(BF16)
    lo = (sc - hi.astype(F32)).astype(BF16)
    w = w_ref[0].astype(BF16)
    o_ref[0] = _dot(hi, w) + _dot(lo, w) + b_ref[0]


def _ada_mod(cc, ada_w, ada_b):
    depth, d, n = ada_w.shape
    tn = 1024
    return pl.pallas_call(
        _ada_kernel,
        grid=(depth, n // tn),
        in_specs=[
            pl.BlockSpec((cc.shape[0], d), lambda i, j: (0, 0)),
            pl.BlockSpec((1, d, tn), lambda i, j: (i, 0, j)),
            pl.BlockSpec((1, 1, tn), lambda i, j: (i, 0, j)),
        ],
        out_specs=pl.BlockSpec((1, cc.shape[0], tn), lambda i, j: (i, 0, j)),
        out_shape=jax.ShapeDtypeStruct((depth, cc.shape[0], n), F32),
        compiler_params=_cparams(("parallel", "parallel")),
        name="ada_mod",
    )(cc, ada_w, ada_b.reshape(depth, 1, n))


def _nmm_kernel(x_ref, g_ref, sh_ref, sc_ref, w_ref, b_ref, o_ref, h_scr):
    @pl.when(pl.program_id(2) == 0)
    def _():
        x = x_ref[0]
        ms = jnp.mean(x * x, axis=-1, keepdims=True)
        xn = x * lax.rsqrt(ms + EPS) * g_ref[...]
        h_scr[...] = (xn * (1.0 + sc_ref[0]) + sh_ref[0]).astype(BF16)

    o_ref[0] = (_dot(h_scr[...], w_ref[...]) + b_ref[...]).astype(o_ref.dtype)


def _norm_mod_matmul(x, g, shift, scale, w, bias, out_dtype, tm, tn):
    b, s, d = x.shape
    n = w.shape[1]
    tm = min(tm, s)
    return pl.pallas_call(
        _nmm_kernel,
        grid=(b, s // tm, n // tn),
        in_specs=[
            pl.BlockSpec((1, tm, d), lambda bi, i, j: (bi, i, 0)),
            pl.BlockSpec((1, d), lambda bi, i, j: (0, 0)),
            pl.BlockSpec((1, 1, d), lambda bi, i, j: (bi, 0, 0)),
            pl.BlockSpec((1, 1, d), lambda bi, i, j: (bi, 0, 0)),
            pl.BlockSpec((d, tn), lambda bi, i, j: (0, j)),
            pl.BlockSpec((1, tn), lambda bi, i, j: (0, j)),
        ],
        out_specs=pl.BlockSpec((1, tm, tn), lambda bi, i, j: (bi, i, j)),
        out_shape=jax.ShapeDtypeStruct((b, s, n), out_dtype),
        scratch_shapes=[pltpu.VMEM((tm, d), BF16)],
        compiler_params=_cparams(("parallel", "parallel", "arbitrary")),
        name="norm_mod_matmul",
    )(x, g.reshape(1, d), shift.reshape(b, 1, d), scale.reshape(b, 1, d), w, bias.reshape(1, n))


NA_PAIR_ROWS = NA_KH + 2
NA_VARIANTS = 5


def _pair_window_start(i, rows):
    return np.clip(2 * i - NA_KH // 2, 0, rows - NA_PAIR_ROWS)


def _attn_bias_table(rpb, rows):
    nh = rpb.shape[0]
    kc = np.arange(GRID_W)[:, None]
    qc = np.arange(GRID_W)[None, :]
    col_off = np.clip(kc - qc, -(NA_KW - 1), NA_KW - 1) + NA_KW - 1
    onehot = (col_off[None] == np.arange(2 * NA_KW - 1)[:, None, None]).astype(np.float32)
    toep = jnp.einsum('hrc,ckq->hrkq', rpb.astype(F32), onehot, precision=HI)
    ws_col = np.clip(qc - NA_KW // 2, 0, GRID_W - NA_KW)
    toep = jnp.where(((kc >= ws_col) & (kc < ws_col + NA_KW))[None, None], toep, NEG_INF)
    dead = jnp.full((nh, GRID_W, GRID_W), NEG_INF, F32)
    rep_pair = [0, 1, 2, rows // 2 - 2, rows // 2 - 1]
    blocks = []
    for i in rep_pair:
        ws = int(_pair_window_start(i, rows))
        for t in range(NA_PAIR_ROWS):
            for e in range(2):
                r = 2 * i + e
                rs = int(np.clip(r - NA_KH // 2, 0, rows - NA_KH))
                kr = ws + t
                blocks.append(toep[:, kr - r + NA_KH - 1] if rs <= kr < rs + NA_KH else dead)
    bias = jnp.stack(blocks, axis=1).reshape(nh, NA_VARIANTS, NA_PAIR_ROWS, 2, GRID_W, GRID_W)
    return bias.transpose(0, 1, 2, 4, 3, 5).reshape(nh, NA_VARIANTS, NA_PAIR_ROWS * GRID_W, 2 * GRID_W)


def _head_rmsnorm(t, g):
    sq = t * t
    hi = sq.astype(BF16)
    lo = (sq - hi.astype(F32)).astype(BF16)
    avg = jnp.full((NA_HEAD_DIM, NA_HEAD_DIM), 1.0 / NA_HEAD_DIM, BF16)
    ms = _dot(hi, avg) + _dot(lo, avg)
    return t * lax.rsqrt(ms + EPS) * g


def _attn_kernel(q_ref, k_ref, v_ref, kc_ref, vc_ref, qg_ref, kg_ref, bias_ref, o_ref, qs, ks, kcs, vt, vct):
    seq, hd = q_ref.shape[1], q_ref.shape[2]
    rows = seq // GRID_W
    pair = 2 * GRID_W
    win_blocks = NA_PAIR_ROWS // 2
    ctx_blocks = kc_ref.shape[1] // pair
    qs[...] = (_head_rmsnorm(q_ref[0].astype(F32), qg_ref[...]) * hd ** -0.5).astype(BF16)
    ks[...] = _head_rmsnorm(k_ref[0].astype(F32), kg_ref[...]).astype(BF16)
    kcs[...] = _head_rmsnorm(kc_ref[0].astype(F32), kg_ref[...]).astype(BF16)
    for blk in range(seq // pair):
        vt[blk] = v_ref[0, blk * pair:(blk + 1) * pair, :].astype(F32).T.astype(BF16)
    for blk in range(ctx_blocks):
        vct[blk] = vc_ref[0, blk * pair:(blk + 1) * pair, :].astype(F32).T.astype(BF16)

    def body(i, carry):
        ws = jnp.clip(2 * i - NA_KH // 2, 0, rows - NA_PAIR_ROWS)
        var = (2 * i - ws) // 2
        wb = ws // 2
        q0 = pl.multiple_of(i * pair, pair)
        k0 = pl.multiple_of(ws * GRID_W, pair)
        q_p = qs[pl.ds(q0, pair), :]
        s_lat = _dot_nt(ks[pl.ds(k0, win_blocks * pair), :], q_p) + bias_ref[0, var]
        s_ctx = _dot_nt(kcs[...], q_p)
        m = jnp.maximum(jnp.max(s_lat, axis=0, keepdims=True), jnp.max(s_ctx, axis=0, keepdims=True))
        p_lat = jnp.exp(s_lat - m)
        p_ctx = jnp.exp(s_ctx - m)
        den = jnp.sum(p_lat, axis=0, keepdims=True) + jnp.sum(p_ctx, axis=0, keepdims=True)
        p_lat = p_lat.astype(BF16)
        p_ctx = p_ctx.astype(BF16)
        o_t = jnp.zeros((hd, pair), F32)
        for blk in range(win_blocks):
            o_t = o_t + _dot(vt[wb + blk], p_lat[blk * pair:(blk + 1) * pair, :])
        for blk in range(ctx_blocks):
            o_t = o_t + _dot(vct[blk], p_ctx[blk * pair:(blk + 1) * pair, :])
        o_ref[0, pl.ds(q0, pair), :] = (o_t / den).T.astype(o_ref.dtype)
        return carry

    lax.fori_loop(0, rows // 2, body, 0, unroll=2)


def _attention(proj, proj_c, q_norm, k_norm, bias_tab):
    b, s, _ = proj.shape
    ctx = proj_c.shape[1]
    hd = NA_HEAD_DIM
    nh = NA_HEADS
    return pl.pallas_call(
        _attn_kernel,
        grid=(b, nh),
        in_specs=[
            pl.BlockSpec((1, s, hd), lambda bi, h: (bi, 0, h)),
            pl.BlockSpec((1, s, hd), lambda bi, h: (bi, 0, 2 * nh + h)),
            pl.BlockSpec((1, s, hd), lambda bi, h: (bi, 0, 3 * nh + h)),
            pl.BlockSpec((1, ctx, hd), lambda bi, h: (bi, 0, h)),
            pl.BlockSpec((1, ctx, hd), lambda bi, h: (bi, 0, nh + h)),
            pl.BlockSpec((1, hd), lambda bi, h: (0, 0)),
            pl.BlockSpec((1, hd), lambda bi, h: (0, 0)),
            pl.BlockSpec((1,) + bias_tab.shape[1:], lambda bi, h: (h, 0, 0, 0)),
        ],
        out_specs=pl.BlockSpec((1, s, hd), lambda bi, h: (bi, 0, h)),
        out_shape=jax.ShapeDtypeStruct((b, s, nh * hd), BF16),
        scratch_shapes=[pltpu.VMEM((s, hd), BF16), pltpu.VMEM((s, hd), BF16), pltpu.VMEM((ctx, hd), BF16),
                        pltpu.VMEM((s // (2 * GRID_W), hd, 2 * GRID_W), BF16), pltpu.VMEM((ctx // (2 * GRID_W), hd, 2 * GRID_W), BF16)],
        compiler_params=_cparams(("parallel", "parallel")),
        name="nbr_attention",
    )(proj, proj, proj, proj_c, proj_c, q_norm.reshape(1, hd), k_norm.reshape(1, hd), bias_tab)


def _split3(v):
    hi = v.astype(BF16)
    r1 = v - hi.astype(F32)
    mid = r1.astype(BF16)
    lo = (r1 - mid.astype(F32)).astype(BF16)
    return hi, mid, lo


def _ssd_kernel(*refs, with_y, nc):
    (xr_ref, br_ref, cr_ref, cwx, cbx, cwb, cbb, cwc, cbc, dt_ref, bias_ref, a_ref), rest = refs[:12], refs[12:]
    if with_y:
        (z_ref, dsk_ref, gn_ref, sf0_ref, sb0_ref, y_ref, xs, bs, cs, contrib, dec, wrow, xf, sfa, sba, drow, dtrow, dsplit, esplit) = rest
    else:
        sf_out, sb_out, xs, bs, cs, contrib, dec, wrow = rest
    ck = SSM_CHUNK
    length = nc * ck
    nh = HEADS_PER_GROUP
    nhd = 2 * nh
    gw = GROUP_WIDTH
    dot_hi = functools.partial(jnp.dot, precision=HI, preferred_element_type=F32)

    halo = 16
    win = ck + 2 * halo
    conv_w = jnp.concatenate([cwx[...], cwb[...], cwc[...]], axis=1)
    conv_b = jnp.concatenate([cbx[...], cbb[...], cbc[...]], axis=1)
    wi = lax.broadcasted_iota(jnp.int32, (ck, win), 1) - lax.broadcasted_iota(jnp.int32, (ck, win), 0)

    def conv_silu_chunk(c):
        l0 = pl.multiple_of(c * ck, ck)
        s0 = pl.multiple_of(jnp.clip(l0 - halo, 0, length - win), halo)
        rel = wi + (s0 - l0)
        window = jnp.concatenate([r[0, pl.ds(s0, win), :] for r in (xr_ref, br_ref, cr_ref)], axis=1)
        cur = jnp.concatenate([r[0, pl.ds(l0, ck), :] for r in (xr_ref, br_ref, cr_ref)], axis=1).astype(F32)
        prev = _dot((rel == -1).astype(BF16), window)
        nxt = _dot((rel == 1).astype(BF16), window)
        y = _silu(conv_b + prev * conv_w[0:1, :] + cur * conv_w[1:2, :] + nxt * conv_w[2:3, :])
        xs[pl.ds(l0, ck), :] = y[:, :gw].astype(BF16)
        if with_y:
            xf[pl.ds(l0, ck), :] = y[:, :gw]
        bs[pl.ds(l0, ck), :] = y[:, gw:gw + SSM_STATE].astype(BF16)
        cs[pl.ds(l0, ck), :] = y[:, gw + SSM_STATE:].astype(BF16)
        return y[:, :gw], y[:, gw:gw + SSM_STATE].astype(BF16)

    ri = lax.broadcasted_iota(jnp.int32, (ck, ck), 0)
    ci = lax.broadcasted_iota(jnp.int32, (ck, ck), 1)
    lower = ci <= ri
    upper = ci >= ri
    eye = (ci == ri).astype(BF16)
    dt_row = _softplus(dt_ref[0, 0] + bias_ref[0])
    a_row = dt_row * a_ref[0]
    cum_f = dot_hi(a_row, upper.astype(F32))
    cum_b = dot_hi(a_row, lower.astype(F32))
    tot = dot_hi(a_row, jnp.ones((ck, ck), F32))
    d_row = jnp.where((ri % nhd) < nh, cum_f, cum_b)
    wrow[...] = jnp.exp(tot - d_row) * dt_row
    texp = jnp.exp(tot)
    row4 = lax.broadcasted_iota(jnp.int32, (ck, 2 * gw), 0)
    lane4 = lax.broadcasted_iota(jnp.int32, (ck, 2 * gw), 1) // SSM_HEAD_DIM
    tmask = jnp.where(row4 % nhd == lane4, jnp.concatenate([texp] * (2 * gw // ck), axis=1), 0.0)
    pick = (lax.broadcasted_iota(jnp.int32, (dec.shape[0], ck), 1) // nhd
            == lax.broadcasted_iota(jnp.int32, (dec.shape[0], ck), 0)).astype(F32)
    dec[...] = dot_hi(pick, tmask)
    if with_y:
        drow[...] = d_row
        dtrow[...] = dt_row
        for k, part in enumerate(_split3(d_row)):
            dsplit[k] = part.astype(F32)
        for k, part in enumerate(_split3(jnp.exp(d_row))[:2]):
            esplit[k] = part.astype(F32)

    def spread(src, r0, width):
        rows = jnp.concatenate([jnp.broadcast_to(src[pl.ds(r0 + hd, 1), :], (width, ck)) for hd in range(nhd)], axis=0)
        return _dot_nt(eye, rows.astype(BF16))

    def chunk_states(c, carry):
        xc, b_c = conv_silu_chunk(c)
        wx = spread(wrow, c * nhd, SSM_HEAD_DIM)
        xw = jnp.concatenate([(xc * wx[:, :gw]).astype(BF16), (xc * wx[:, gw:]).astype(BF16)], axis=1)
        contrib[c] = _dot_tn(b_c, xw)
        return carry

    lax.fori_loop(0, nc, chunk_states, 0, unroll=2)

    def fwd_chain(c, s):
        if with_y:
            sfa[c] = s
        return s * dec[pl.ds(c, 1), :gw] + contrib[c, :, :gw]

    def bwd_chain(i, s):
        c = nc - 1 - i
        if with_y:
            sba[c] = s
        return s * dec[pl.ds(c, 1), gw:] + contrib[c, :, gw:]

    if with_y:
        s_f0 = sf0_ref[0, 0]
        s_b0 = sb0_ref[0, 0]
    else:
        s_f0 = jnp.zeros((SSM_STATE, gw), F32)
        s_b0 = s_f0
    s_f = lax.fori_loop(0, nc, fwd_chain, s_f0)
    s_b = lax.fori_loop(0, nc, bwd_chain, s_b0)
    if not with_y:
        sf_out[0, 0] = s_f
        sb_out[0, 0] = s_b
        return

    lane_head = lax.broadcasted_iota(jnp.int32, (ck, gw), 1) // SSM_HEAD_DIM

    def out_step(c, carry):
        l0 = pl.multiple_of(c * ck, ck)
        r0 = c * nhd
        d_b = spread(dsplit.at[0], r0, ck) + spread(dsplit.at[1], r0, ck) + spread(dsplit.at[2], r0, ck)
        e_b = spread(esplit.at[0], r0, SSM_HEAD_DIM) + spread(esplit.at[1], r0, SSM_HEAD_DIM)
        b_c = bs[pl.ds(l0, ck), :]
        c_c = cs[pl.ds(l0, ck), :]
        x_c = xs[pl.ds(l0, ck), :]
        g = _dot_nt(c_c, b_c)
        acc = jnp.zeros((ck, gw), F32)
        for j in range(nh):
            jb = nh + j
            lf = jnp.exp(jnp.where(lower, d_b[:, j * ck:(j + 1) * ck] - drow[pl.ds(r0 + j, 1), :], -1e30)) * dtrow[pl.ds(r0 + j, 1), :]
            lb = jnp.exp(jnp.where(upper, d_b[:, jb * ck:(jb + 1) * ck] - drow[pl.ds(r0 + jb, 1), :], -1e30)) * dtrow[pl.ds(r0 + jb, 1), :]
            m = (g * (lf + lb)).astype(BF16)
            acc = acc + _dot(m, jnp.where(lane_head == j, x_c, jnp.zeros_like(x_c)))
        acc = acc + _dot(c_c, sfa[c].astype(BF16)) * e_b[:, :gw]
        acc = acc + _dot(c_c, sba[c].astype(BF16)) * e_b[:, gw:]
        y = acc + xf[pl.ds(l0, ck), :] * dsk_ref[...]
        y = y * _silu(z_ref[0, pl.ds(l0, ck), :].astype(F32))
        ms = jnp.mean(y * y, axis=-1, keepdims=True)
        y_ref[0, pl.ds(l0, ck), :] = (y * lax.rsqrt(ms + EPS) * gn_ref[...]).astype(y_ref.dtype)
        return carry

    lax.fori_loop(0, nc, out_step, 0, unroll=2)


def _ssd(proj, x_blk0, z_blk0, dt_raw, conv_w, conv_b, dt_bias, a_log, d_skip, gate_norm, init):
    b, length, _ = proj.shape
    nc = length // SSM_CHUNK
    ng, nh, gw, st = SSM_GROUPS, HEADS_PER_GROUP, GROUP_WIDTH, SSM_STATE
    with_y = init is not None
    b_blk0 = x_blk0 * (gw // st) + SSM_D_INNER // st
    c_blk0 = b_blk0 + ng
    ck = SSM_CHUNK
    nhd = 2 * nh
    assert nc * nhd <= ck
    dtg = dt_raw.reshape(b, nc, ck, 2, ng, nh).transpose(0, 4, 1, 3, 5, 2).reshape(b, ng, nc * nhd, ck)
    dtg = jnp.pad(dtg, ((0, 0), (0, 0), (0, ck - nc * nhd), (0, 0)))
    per_row = lambda t: jnp.tile(t.reshape(2, ng, nh).transpose(1, 0, 2).reshape(ng, nhd), (1, ck // nhd)).reshape(ng, ck, 1)
    bias_r = per_row(dt_bias)
    a_r = per_row(-jnp.exp(a_log.astype(F32)))
    cw_x, cw_b, cw_c = conv_w[:, :SSM_D_INNER], conv_w[:, SSM_D_INNER:SSM_D_INNER + ng * st], conv_w[:, SSM_D_INNER + ng * st:]
    cb = conv_b.reshape(1, -1)
    cb_x, cb_b, cb_c = cb[:, :SSM_D_INNER], cb[:, SSM_D_INNER:SSM_D_INNER + ng * st], cb[:, SSM_D_INNER + ng * st:]
    in_specs = [
        pl.BlockSpec((1, length, gw), lambda bi, g: (bi, 0, x_blk0 + g)),
        pl.BlockSpec((1, length, st), lambda bi, g: (bi, 0, b_blk0 + g)),
        pl.BlockSpec((1, length, st), lambda bi, g: (bi, 0, c_blk0 + g)),
        pl.BlockSpec((3, gw), lambda bi, g: (0, g)),
        pl.BlockSpec((1, gw), lambda bi, g: (0, g)),
        pl.BlockSpec((3, st), lambda bi, g: (0, g)),
        pl.BlockSpec((1, st), lambda bi, g: (0, g)),
        pl.BlockSpec((3, st), lambda bi, g: (0, g)),
        pl.BlockSpec((1, st), lambda bi, g: (0, g)),
        pl.BlockSpec((1, 1, ck, ck), lambda bi, g: (bi, g, 0, 0)),
        pl.BlockSpec((1, ck, 1), lambda bi, g: (g, 0, 0)),
        pl.BlockSpec((1, ck, 1), lambda bi, g: (g, 0, 0)),
    ]
    args = [proj, proj, proj, cw_x, cb_x, cw_b, cb_b, cw_c, cb_c, dtg, bias_r, a_r]
    scratch = [pltpu.VMEM((length, gw), BF16), pltpu.VMEM((length, st), BF16), pltpu.VMEM((length, st), BF16),
               pltpu.VMEM((nc, st, 2 * gw), F32), pltpu.VMEM((ck // nhd, 2 * gw), F32), pltpu.VMEM((ck, ck), F32)]
    state_spec = pl.BlockSpec((1, 1, st, gw), lambda bi, g: (bi, g, 0, 0))
    state_shape = jax.ShapeDtypeStruct((b, ng, st, gw), F32)
    if with_y:
        in_specs += [
            pl.BlockSpec((1, length, gw), lambda bi, g: (bi, 0, z_blk0 + g)),
            pl.BlockSpec((1, gw), lambda bi, g: (0, g)),
            pl.BlockSpec((1, gw), lambda bi, g: (0, g)),
            state_spec, state_spec,
        ]
        args += [proj, jnp.repeat(d_skip, SSM_HEAD_DIM).reshape(1, -1), gate_norm.reshape(1, -1), init[0], init[1]]
        out_specs = pl.BlockSpec((1, length, gw), lambda bi, g: (bi, 0, g))
        out_shape = jax.ShapeDtypeStruct((b, length, SSM_D_INNER), BF16)
        scratch += [pltpu.VMEM((length, gw), F32), pltpu.VMEM((nc, st, gw), F32), pltpu.VMEM((nc, st, gw), F32),
                    pltpu.VMEM((ck, ck), F32), pltpu.VMEM((ck, ck), F32), pltpu.VMEM((3, ck, ck), F32), pltpu.VMEM((2, ck, ck), F32)]
    else:
        out_specs = [state_spec, state_spec]
        out_shape = [state_shape, state_shape]
    return pl.pallas_call(
        functools.partial(_ssd_kernel, with_y=with_y, nc=nc),
        grid=(b, ng),
        in_specs=in_specs,
        out_specs=out_specs,
        out_shape=out_shape,
        scratch_shapes=scratch,
        compiler_params=_cparams(("parallel", "parallel")),
        name="ssd_main" if with_y else "ssd_ctx",
    )(*args)


def _proj_res_kernel(*refs, n_lhs):
    a_refs, w_refs = refs[:n_lhs], refs[n_lhs:2 * n_lhs]
    b_ref, gate_ref, res_ref, o_ref = refs[2 * n_lhs:]
    acc = _dot(a_refs[0][0], w_refs[0][...])
    for a_ref, w_ref in zip(a_refs[1:], w_refs[1:]):
        acc = acc + _dot(a_ref[0], w_ref[...])
    o_ref[0] = res_ref[0] + gate_ref[0] * (acc + b_ref[...])


def _proj_residual(lhs, w, bias, gate, resid, tm=1024, tn=1024):
    b, s, k = lhs[0].shape
    n = w.shape[1]
    n_lhs = len(lhs)
    in_specs = [pl.BlockSpec((1, tm, k), lambda bi, i, j: (bi, i, 0)) for _ in lhs]
    in_specs += [pl.BlockSpec((k, tn), functools.partial(lambda bi, i, j, t: (t, j), t=t)) for t in range(n_lhs)]
    in_specs += [
        pl.BlockSpec((1, tn), lambda bi, i, j: (0, j)),
        pl.BlockSpec((1, 1, tn), lambda bi, i, j: (bi, 0, j)),
        pl.BlockSpec((1, tm, tn), lambda bi, i, j: (bi, i, j)),
    ]
    return pl.pallas_call(
        functools.partial(_proj_res_kernel, n_lhs=n_lhs),
        grid=(b, s // tm, n // tn),
        in_specs=in_specs,
        out_specs=pl.BlockSpec((1, tm, tn), lambda bi, i, j: (bi, i, j)),
        out_shape=jax.ShapeDtypeStruct((b, s, n), F32),
        compiler_params=_cparams(("parallel", "parallel", "parallel")),
        name="proj_residual",
    )(*lhs, *([w] * n_lhs), bias.reshape(1, n), gate.reshape(b, 1, n), resid)


def _hy_filter_kernel(tw_ref, f_ref, w1t, w1c, w1s, b1, w2, b2, w3, b3, fr, w4_ref, dl_ref, o_ref, h_scr):
    dot_hi = lambda a, b: jnp.dot(a, b, precision=HI, preferred_element_type=F32)

    @pl.when(pl.program_id(0) == 0)
    def _():
        t = tw_ref[:, 0:1]
        ang = tw_ref[:, 1:2] * f_ref[...]
        pre = t * w1t[...] + dot_hi(jnp.cos(ang), w1c[...]) + dot_hi(-jnp.sin(ang), w1s[...]) + b1[...]
        h = jnp.sin(fr[...] * pre)
        h = jnp.sin(fr[...] * (dot_hi(h, w2[...]) + b2[...]))
        h_scr[...] = jnp.sin(fr[...] * (dot_hi(h, w3[...]) + b3[...]))

    o_ref[...] = dot_hi(h_scr[...], w4_ref[...]) * jnp.exp(-tw_ref[:, 0:1] * dl_ref[...])


def _hyena_filters(length, w1, b1, w2, b2, w3, b3, w4, freq):
    bands = (HY_EMB - 1) // 2
    fw = HY_FILTER_WIDTH
    t = jnp.linspace(0.0, 1.0, length, dtype=F32)
    w = 2 * math.pi * jnp.arange(length, dtype=F32) / length
    tw = jnp.stack([t, w], axis=1)
    f = jnp.linspace(1e-4, bands - 1, bands, dtype=F32).reshape(1, bands)
    min_decay = math.log(HY_TARGET) / HY_SLOW_PCT
    max_decay = math.log(HY_TARGET) / HY_FAST_PCT
    deltas = jnp.abs(jnp.linspace(min_decay, max_decay, D_MODEL, dtype=F32))
    dl = jnp.concatenate([deltas, deltas]).reshape(1, 2 * D_MODEL)
    tn = 1024
    small = lambda shape: pl.BlockSpec(shape, lambda j: (0, 0))
    row = lambda v: v.reshape(1, -1)
    return pl.pallas_call(
        _hy_filter_kernel,
        grid=(2 * D_MODEL // tn,),
        in_specs=[small((length, 2)), small((1, bands)), small((1, fw)), small((bands, fw)), small((bands, fw)), small((1, fw)),
                  small((fw, fw)), small((1, fw)), small((fw, fw)), small((1, fw)), small((1, fw)),
                  pl.BlockSpec((fw, tn), lambda j: (0, j)), pl.BlockSpec((1, tn), lambda j: (0, j))],
        out_specs=pl.BlockSpec((length, tn), lambda j: (0, j)),
        out_shape=jax.ShapeDtypeStruct((length, 2 * D_MODEL), F32),
        scratch_shapes=[pltpu.VMEM((length, fw), F32)],
        compiler_params=_cparams(("arbitrary",)),
        name="hyena_filters",
    )(tw, f, w1[0:1], w1[1:1 + bands], w1[1 + bands:], row(b1), w2, row(b2), w3, row(b3), row(freq), w4, dl)


def _hy_conv_kernel(x0_ref, x1_ref, v_ref, w0, b0, w1, b1, w2, b2, x0_out, u_out):
    length = x0_ref.shape[1]

    def conv(raw_ref, w_ref, b_ref):
        x = raw_ref[0].astype(F32)
        rows = lax.broadcasted_iota(jnp.int32, x.shape, 0)
        xm1 = jnp.where(rows == 0, 0.0, pltpu.roll(x, 1, 0))
        xp1 = jnp.where(rows == length - 1, 0.0, pltpu.roll(x, length - 1, 0))
        return b_ref[...] + xm1 * w_ref[0:1, :] + x * w_ref[1:2, :] + xp1 * w_ref[2:3, :]

    x0_out[0] = conv(x0_ref, w0, b0).astype(x0_out.dtype)
    u_out[0] = (conv(v_ref, w2, b2) * conv(x1_ref, w1, b1)).astype(u_out.dtype)


def _hyena_conv(proj, conv_w, conv_b):
    b, length, _ = proj.shape
    d = D_MODEL
    tn = 512
    nb = d // tn
    cb = conv_b.reshape(1, -1)
    seg = lambda k: pl.BlockSpec((1, length, tn), lambda bi, j: (bi, 0, k * nb + j))
    wseg = lambda k: pl.BlockSpec((3, tn), lambda bi, j: (0, k * nb + j))
    bseg = lambda k: pl.BlockSpec((1, tn), lambda bi, j: (0, k * nb + j))
    out = pl.BlockSpec((1, length, tn), lambda bi, j: (bi, 0, j))
    return pl.pallas_call(
        _hy_conv_kernel,
        grid=(b, nb),
        in_specs=[seg(0), seg(1), seg(2), wseg(0), bseg(0), wseg(1), bseg(1), wseg(2), bseg(2)],
        out_specs=[out, out],
        out_shape=[jax.ShapeDtypeStruct((b, length, d), BF16)] * 2,
        compiler_params=_cparams(("parallel", "parallel")),
        name="hyena_conv",
    )(proj, proj, proj, conv_w, cb, conv_w, cb, conv_w, cb)


def _dft_matrices(length):
    n = 2 * length
    f = lax.broadcasted_iota(jnp.int32, (length, length), 0)
    t = lax.broadcasted_iota(jnp.int32, (length, length), 1)
    ang = ((f * t) % n).astype(F32) * (2 * math.pi / n)
    sign = (1 - 2 * (t % 2)).astype(F32)
    fwd_c = jnp.cos(ang)
    fwd_s = jnp.where(f == 0, sign, -jnp.sin(ang))
    fwd = jnp.concatenate([fwd_c, fwd_s], axis=0).astype(BF16)
    wgt = jnp.where(f == 0, 1.0, 2.0) / n
    inv = jnp.concatenate([(fwd_c * wgt).T, (fwd_s * wgt).T], axis=1).astype(BF16)
    return fwd, inv


def _mm_kernel(a_ref, b_ref, o_ref):
    o_ref[...] = _dot(a_ref[...], b_ref[...]).astype(o_ref.dtype)


def _matmul(a, b, out_dtype=F32, tm=1024, tn=1024):
    m, k = a.shape
    n = b.shape[1]
    return pl.pallas_call(
        _mm_kernel,
        grid=(n // tn, m // tm),
        in_specs=[pl.BlockSpec((tm, k), lambda j, i: (i, 0)), pl.BlockSpec((k, tn), lambda j, i: (0, j))],
        out_specs=pl.BlockSpec((tm, tn), lambda j, i: (i, j)),
        out_shape=jax.ShapeDtypeStruct((m, n), out_dtype),
        compiler_params=_cparams(("parallel", "parallel")),
        name="matmul",
    )(a, b)


def _filter_spectrum(kk, fwd):
    length = kk.shape[0]
    d = D_MODEL
    k_f, k_b = kk[:, :d], kk[:, d:]
    k_fp = k_f.at[0].add(k_b[0])
    k_bp = k_b.at[0].set(0.0)
    spec = _matmul(fwd, jnp.concatenate([k_fp, k_bp], axis=1).astype(BF16))
    a_re, a_im, b_re, b_im = spec[:length, :d], spec[length:, :d], spec[:length, d:], spec[length:, d:]
    nz = (jnp.arange(length) != 0)[:, None]
    k_re = a_re + b_re
    k_im = jnp.where(nz, a_im - b_im, a_im + b_im)
    return jnp.stack([k_re, jnp.where(nz, k_im, 0.0), jnp.where(nz, k_re, k_im)])


def _dft_fwd_kernel(fc_ref, fs_ref, u_ref, k_ref, y_ref):
    u = u_ref[0]
    re = _dot(fc_ref[...], u)
    im = _dot(fs_ref[...], u)
    y_ref[0, 0] = (re * k_ref[0] - im * k_ref[1]).astype(y_ref.dtype)
    y_ref[0, 1] = (re * k_ref[1] + im * k_ref[2]).astype(y_ref.dtype)


def _dft_forward(u, fwd, coef, tm=1024, tn=512):
    b, length, d = u.shape
    ni = length // tm
    return pl.pallas_call(
        _dft_fwd_kernel,
        grid=(d // tn, ni, b),
        in_specs=[
            pl.BlockSpec((tm, length), lambda j, i, bi: (i, 0)),
            pl.BlockSpec((tm, length), lambda j, i, bi: (ni + i, 0)),
            pl.BlockSpec((1, length, tn), lambda j, i, bi: (bi, 0, j)),
            pl.BlockSpec((3, tm, tn), lambda j, i, bi: (0, i, j)),
        ],
        out_specs=pl.BlockSpec((1, 2, tm, tn), lambda j, i, bi: (bi, 0, i, j)),
        out_shape=jax.ShapeDtypeStruct((b, 2, length, d), BF16),
        compiler_params=_cparams(("parallel", "parallel", "parallel")),
        name="dft_forward",
    )(fwd, fwd, u, coef)


def _dft_inv_kernel(g_ref, y_ref, x0_ref, u_ref, fb_ref, o_ref):
    conv = _dot(g_ref[...], y_ref[0])
    o_ref[0] = (x0_ref[0].astype(F32) * (conv + u_ref[0].astype(F32) * fb_ref[...])).astype(o_ref.dtype)


def _dft_inverse(y, inv, x0, u, filt_bias, tm=512, tn=512):
    b, n2, d = y.shape
    length = n2 // 2
    return pl.pallas_call(
        _dft_inv_kernel,
        grid=(d // tn, length // tm, b),
        in_specs=[
            pl.BlockSpec((tm, n2), lambda j, i, bi: (i, 0)),
            pl.BlockSpec((1, n2, tn), lambda j, i, bi: (bi, 0, j)),
            pl.BlockSpec((1, tm, tn), lambda j, i, bi: (bi, i, j)),
            pl.BlockSpec((1, tm, tn), lambda j, i, bi: (bi, i, j)),
            pl.BlockSpec((1, tn), lambda j, i, bi: (0, j)),
        ],
        out_specs=pl.BlockSpec((1, tm, tn), lambda j, i, bi: (bi, i, j)),
        out_shape=jax.ShapeDtypeStruct((b, length, d), BF16),
        compiler_params=_cparams(("parallel", "parallel", "parallel")),
        name="dft_inverse",
    )(inv, y, x0, u, filt_bias.reshape(1, d))


PACK_HI = 0xFFFF0000
TILE_SUBLANES = 8


def _pack_pairs(lo, hi):
    lo_b = pltpu.bitcast(lo.astype(BF16).astype(F32), jnp.uint32)
    hi_b = pltpu.bitcast(hi.astype(BF16).astype(F32), jnp.uint32)
    return (lo_b >> 16) | (hi_b & jnp.uint32(PACK_HI))


def _unpack_pairs(w):
    return pltpu.bitcast(w << 16, F32), pltpu.bitcast(w & jnp.uint32(PACK_HI), F32)


def _router_kernel(x_ref, g_ref, sh_ref, sc_ref, wr_ref, br_ref, hp_ref, idx_ref, gate_ref, rank_ref, cnt_ref, run):
    tm = x_ref.shape[1]
    half = x_ref.shape[2] // 2
    first = (pl.program_id(0) == 0) & (pl.program_id(1) == 0)

    @pl.when(first)
    def _():
        run[...] = jnp.zeros_like(run)

    x = x_ref[0]
    ms = jnp.mean(x * x, axis=-1, keepdims=True)
    h = x * lax.rsqrt(ms + EPS) * g_ref[...] * (1.0 + sc_ref[0]) + sh_ref[0]
    hp_ref[0] = _pack_pairs(h[:, :half], h[:, half:])

    logits = jnp.dot(h, wr_ref[...], precision=HI, preferred_element_type=F32) + br_ref[...]
    lane_i = lax.broadcasted_iota(jnp.int32, logits.shape, 1)
    lane = lane_i.astype(F32)
    work = jnp.where(lane_i < N_EXPERTS, logits, -jnp.inf)
    li = lax.broadcasted_iota(jnp.int32, (tm, tm), 0)
    ki = lax.broadcasted_iota(jnp.int32, (tm, tm), 1)
    strict_lower = (ki < li).astype(BF16)
    sels, tops = [], []
    for _ in range(TOP_K):
        m = jnp.max(work, axis=-1, keepdims=True)
        first_idx = jnp.min(jnp.where(work == m, lane, float(LANES)), axis=-1, keepdims=True)
        sel = lane == first_idx
        sels.append(sel)
        tops.append((m, first_idx))
        work = jnp.where(sel, -jnp.inf, work)
    chosen = sels[0] | sels[1] | sels[2] | sels[3]
    before = _dot(strict_lower, chosen.astype(BF16)) + run[...]
    run[...] = run[...] + jnp.sum(chosen.astype(F32), axis=0, keepdims=True)
    den = sum(jnp.exp(m - tops[0][0]) for m, _ in tops)
    idx_o = jnp.zeros(logits.shape, jnp.int32)
    gate_o = jnp.zeros(logits.shape, F32)
    rank_o = jnp.zeros(logits.shape, jnp.int32)
    for k, (sel, (m, first_idx)) in enumerate(zip(sels, tops)):
        rank = jnp.sum(jnp.where(sel, before, 0.0), axis=-1, keepdims=True).astype(jnp.int32)
        idx_o = jnp.where(lane_i == k, first_idx.astype(jnp.int32), idx_o)
        gate_o = jnp.where(lane_i == k, jnp.exp(m - tops[0][0]) / den, gate_o)
        rank_o = jnp.where(lane_i == k, rank, rank_o)
    idx_ref[0] = idx_o
    gate_ref[0] = gate_o
    rank_ref[0] = rank_o
    cnt_ref[...] = run[...]


def _moe_route(x, g, shift, scale, w_r, b_r, tm=512):
    b, s, d = x.shape
    wr = jnp.zeros((d, LANES), F32).at[:, :N_EXPERTS].set(w_r)
    br = jnp.zeros((1, LANES), F32).at[0, :N_EXPERTS].set(b_r)
    tok = lambda n: pl.BlockSpec((1, tm, n), lambda bi, i: (bi, i, 0))
    return pl.pallas_call(
        _router_kernel,
        grid=(b, s // tm),
        in_specs=[
            tok(d),
            pl.BlockSpec((1, d), lambda bi, i: (0, 0)),
            pl.BlockSpec((1, 1, d), lambda bi, i: (bi, 0, 0)),
            pl.BlockSpec((1, 1, d), lambda bi, i: (bi, 0, 0)),
            pl.BlockSpec((d, LANES), lambda bi, i: (0, 0)),
            pl.BlockSpec((1, LANES), lambda bi, i: (0, 0)),
        ],
        out_specs=[tok(d // 2), tok(LANES), tok(LANES), tok(LANES), pl.BlockSpec((1, LANES), lambda bi, i: (0, 0))],
        out_shape=[
            jax.ShapeDtypeStruct((b, s, d // 2), jnp.uint32),
            jax.ShapeDtypeStruct((b, s, LANES), jnp.int32),
            jax.ShapeDtypeStruct((b, s, LANES), F32),
            jax.ShapeDtypeStruct((b, s, LANES), jnp.int32),
            jax.ShapeDtypeStruct((1, LANES), F32),
        ],
        scratch_shapes=[pltpu.VMEM((1, LANES), F32)],
        compiler_params=_cparams(("arbitrary", "arbitrary")),
        name="moe_route",
    )(x, g.reshape(1, d), shift.reshape(b, 1, d), scale.reshape(b, 1, d), wr, br)


MOE_PITCH = TILE_SUBLANES + 1


def _ffn_kernel(be_ref, na_ref, nv_ref, dst_prv, tok_cur, tok_nxt, hp_ref, w1g_ref, w1u_ref, b1g_ref, b1u_ref, w2_ref, b2_ref, y_ref,
                xg, stg, xb, acc, wb1, wb2, zeros, gsem, usem, zsem):
    i = pl.program_id(0)
    f = pl.program_id(1)
    nb = pl.num_programs(0)
    nf = pl.num_programs(1)
    tm = MOE_ROWS
    sub = hp_ref.shape[1]
    half = sub * LANES
    per = tm // MOE_N_FF
    slot = i % 2
    active = i < na_ref[0]

    def gather(tok_ref, to_slot, r):
        return pltpu.make_async_copy(hp_ref.at[tok_ref[0, 0, r]], xg.at[to_slot, pl.ds(r * MOE_PITCH, sub), :], gsem.at[to_slot])

    def scatter(dst_ref, from_slot, r):
        d = dst_ref[0, 0, r]
        return pltpu.make_async_copy(stg.at[from_slot, pl.ds(r * MOE_PITCH, sub), :], y_ref.at[d // sub, :, d % sub, :],
                                     usem.at[from_slot])

    def wait_rows(buf, sem, s):
        pltpu.make_async_copy(buf.at[1 - s, pl.ds(0, tm * sub), :], buf.at[s, pl.ds(0, tm * sub), :], sem.at[s]).wait()

    @pl.when((i == 0) & (f == 0))
    def _():
        stg[...] = jnp.zeros_like(stg)
        zgroups = zeros.shape[0]
        zeros[...] = jnp.zeros_like(zeros)
        dump0 = y_ref.shape[0] - 2 * tm // sub
        fills = [pltpu.make_async_copy(zeros, y_ref.at[pl.ds(dump0 + c * zgroups, zgroups)], zsem)
                 for c in range(2 * tm // sub // zgroups)]
        for cp in fills:
            cp.start()
        for cp in fills:
            cp.wait()
        for r in range(tm):
            gather(tok_cur, 0, r).start(priority=1)

    @pl.when(f == 0)
    def _():
        wait_rows(xg, gsem, slot)
        for j in range(sub):
            lo, hi = _unpack_pairs(xg[slot, pl.ds(j, tm, stride=MOE_PITCH), :])
            xb[:, j * LANES:(j + 1) * LANES] = lo.astype(BF16)
            xb[:, half + j * LANES:half + (j + 1) * LANES] = hi.astype(BF16)
        acc[...] = jnp.broadcast_to(b2_ref[0, 0], acc.shape)

    def move_rows():
        for rr in range(per):
            r = f * per + rr
            gather(tok_nxt, 1 - slot, r).start(priority=1)
            scatter(dst_prv, 1 - slot, r).start(priority=1)

    tf = w2_ref.shape[2]

    def ffn_pass(h):
        rows = pl.ds(h * MOE_SUB_ROWS, MOE_SUB_ROWS)
        x = xb[rows, :]
        g = _dot(x, wb1[:, :tf]) + b1g_ref[0, 0]
        u = _dot(x, wb1[:, tf:]) + b1u_ref[0, 0]
        g = jnp.minimum(g, SWIGLU_LIMIT)
        u = jnp.clip(u, -SWIGLU_LIMIT, SWIGLU_LIMIT)
        a = (u + 1.0) * g / (1.0 + jnp.exp(-SWIGLU_ALPHA * g))
        acc[rows, :] += _dot(a.astype(BF16), wb2[...])

    @pl.when(active)
    def _():
        move_rows()
        wb1[:, :tf] = w1g_ref[0, 0].astype(BF16)
        wb1[:, tf:] = w1u_ref[0, 0].astype(BF16)
        wb2[...] = w2_ref[0, 0].astype(BF16)
        ffn_pass(0)

    for h in range(1, MOE_ROWS // MOE_SUB_ROWS):
        @pl.when(active & (nv_ref[i] > h * MOE_SUB_ROWS))
        def _():
            ffn_pass(h)

    @pl.when(jnp.logical_not(active))
    def _():
        move_rows()

    @pl.when(f == nf - 1)
    def _():
        @pl.when(i >= 1)
        def _():
            wait_rows(stg, usem, slot)
        for j in range(sub):
            stg[slot, pl.ds(j, tm, stride=MOE_PITCH), :] = _pack_pairs(acc[:, j * LANES:(j + 1) * LANES],
                                                                      acc[:, half + j * LANES:half + (j + 1) * LANES])

    @pl.when((i == nb - 1) & (f == nf - 1))
    def _():
        wait_rows(xg, gsem, 1 - slot)
        wait_rows(stg, usem, 1 - slot)


def _moe_experts(layer, hp_tiles, slot_src, block_e, n_active, n_valid, w1, b1, w2, b2):
    n_tok, sub, _ = hp_tiles.shape
    d = 2 * sub * LANES
    tm, tf = MOE_ROWS, MOE_FF_TILE
    nf = MOE_N_FF
    n_real = slot_src.shape[0] // tm
    nb = n_real + 2
    n_rows = TOP_K * n_tok + 2 * tm
    ext = jnp.full(((nb + 2) * tm,), -1, jnp.int32).at[tm:(n_real + 1) * tm].set(slot_src)
    slot = jnp.arange(ext.shape[0], dtype=jnp.int32)
    tok_ext = (jnp.maximum(ext, 0) >> 2).reshape(nb + 2, 1, tm)
    dump = TOP_K * n_tok + ((slot // tm) % 2) * tm + slot % tm
    dst_ext = jnp.where(ext < 0, dump, (ext & (TOP_K - 1)) * n_tok + (ext >> 2)).reshape(nb + 2, 1, tm)

    def blk(i, na):
        return jnp.minimum(i, na[0] - 1)

    def ff(i, f, na):
        return jnp.where(i < na[0], f, nf - 1)

    smem = lambda off: pl.BlockSpec((1, 1, tm), lambda i, f, be, na, nv: (i + off, 0, 0), memory_space=pltpu.SMEM)
    grid_spec = pltpu.PrefetchScalarGridSpec(
        num_scalar_prefetch=3,
        grid=(nb, nf),
        in_specs=[
            smem(0), smem(1), smem(2),
            pl.BlockSpec(memory_space=pl.ANY),
            pl.BlockSpec((1, 1, d, tf), lambda i, f, be, na, nv: (layer, be[blk(i, na)], 0, ff(i, f, na))),
            pl.BlockSpec((1, 1, d, tf), lambda i, f, be, na, nv: (layer, be[blk(i, na)], 0, nf + ff(i, f, na))),
            pl.BlockSpec((1, 1, 1, tf), lambda i, f, be, na, nv: (layer, be[blk(i, na)], 0, ff(i, f, na))),
            pl.BlockSpec((1, 1, 1, tf), lambda i, f, be, na, nv: (layer, be[blk(i, na)], 0, nf + ff(i, f, na))),
            pl.BlockSpec((1, 1, tf, d), lambda i, f, be, na, nv: (layer, be[blk(i, na)], ff(i, f, na), 0)),
            pl.BlockSpec((1, 1, 1, d), lambda i, f, be, na, nv: (layer, be[blk(i, na)], 0, 0)),
        ],
        out_specs=pl.BlockSpec(memory_space=pl.ANY),
        scratch_shapes=[pltpu.VMEM((2, tm * MOE_PITCH, LANES), jnp.uint32), pltpu.VMEM((2, tm * MOE_PITCH, LANES), jnp.uint32),
                        pltpu.VMEM((tm, d), BF16), pltpu.VMEM((tm, d), F32),
                        pltpu.VMEM((d, 2 * tf), BF16), pltpu.VMEM((tf, d), BF16), pltpu.VMEM((8, sub, sub, LANES), jnp.uint32),
                        pltpu.SemaphoreType.DMA((2,)), pltpu.SemaphoreType.DMA((2,)), pltpu.SemaphoreType.DMA(())],
    )
    depth, ne = w1.shape[:2]
    be_ext = jnp.concatenate([block_e, jnp.full((2,), N_EXPERTS - 1, jnp.int32)])
    nv_ext = jnp.concatenate([n_valid, jnp.zeros((2,), jnp.int32)])
    return pl.pallas_call(
        _ffn_kernel,
        grid_spec=grid_spec,
        out_shape=jax.ShapeDtypeStruct((n_rows // sub, sub, sub, LANES), jnp.uint32),
        compiler_params=_cparams(("arbitrary", "arbitrary")),
        name="moe_experts",
    )(be_ext, n_active, nv_ext, dst_ext, tok_ext, tok_ext, hp_tiles, w1, w1, b1.reshape(depth, ne, 1, -1), b1.reshape(depth, ne, 1, -1), w2,
      b2.reshape(depth, ne, 1, -1))


def _combine_kernel(*refs):
    y_refs, (gate_ref, x_ref, gm_ref, o_ref) = refs[:TOP_K], refs[TOP_K:]
    tc = x_ref.shape[1]
    sub = y_refs[0].shape[1]
    half = sub * LANES
    gates = gate_ref[0]
    gk = [jnp.broadcast_to(gates[:, k:k + 1], (tc, LANES)) for k in range(TOP_K)]
    for j in range(sub):
        lo_acc = jnp.zeros((tc, LANES), F32)
        hi_acc = lo_acc
        for k in range(TOP_K):
            lo, hi = _unpack_pairs(y_refs[k][:, j].reshape(tc, LANES))
            lo_acc = lo_acc + gk[k] * lo
            hi_acc = hi_acc + gk[k] * hi
        c0, c1 = j * LANES, half + j * LANES
        o_ref[0, :, c0:c0 + LANES] = x_ref[0, :, c0:c0 + LANES] + gm_ref[0, :, c0:c0 + LANES] * lo_acc
        o_ref[0, :, c1:c1 + LANES] = x_ref[0, :, c1:c1 + LANES] + gm_ref[0, :, c1:c1 + LANES] * hi_acc


def _moe_combine(y_tok, gates, x, gmod, tc=128):
    b, s, d = x.shape
    nt = s // tc
    sub = y_tok.shape[1]
    plane = lambda k: pl.BlockSpec((tc // sub, sub, sub, LANES), lambda bi, i: (k * b * nt + bi * nt + i, 0, 0, 0))
    return pl.pallas_call(
        _combine_kernel,
        grid=(b, nt),
        in_specs=[plane(k) for k in range(TOP_K)] + [
            pl.BlockSpec((1, tc, LANES), lambda bi, i: (bi, i, 0)),
            pl.BlockSpec((1, tc, d), lambda bi, i: (bi, i, 0)),
            pl.BlockSpec((1, 1, d), lambda bi, i: (bi, 0, 0)),
        ],
        out_specs=pl.BlockSpec((1, tc, d), lambda bi, i: (bi, i, 0)),
        out_shape=jax.ShapeDtypeStruct((b, s, d), F32),
        compiler_params=_cparams(("parallel", "parallel")),
        name="moe_combine",
    )(*([y_tok] * TOP_K), gates, x, gmod.reshape(b, 1, d))


def _moe_ffn(layer, x, g, shift, scale, gmod, w_r, b_r, w1, b1, w2, b2):
    b, s, d = x.shape
    n = b * s
    hp, idx, gates, rank, counts = _moe_route(x, g, shift, scale, w_r, b_r)
    counts = counts[0, :N_EXPERTS].astype(jnp.int32)
    padded = (counts + MOE_ROWS - 1) // MOE_ROWS * MOE_ROWS
    pad_end = jnp.cumsum(padded)
    pad_start = pad_end - padded
    idx4 = idx.reshape(n, LANES)[:, :TOP_K]
    onehot = idx4[:, :, None] == jnp.arange(N_EXPERTS, dtype=jnp.int32)
    pos = jnp.sum(jnp.where(onehot, pad_start, 0), axis=-1) + rank.reshape(n, LANES)[:, :TOP_K]
    n_blocks = n * TOP_K // MOE_ROWS + N_EXPERTS
    slot_ids = jnp.arange(n * TOP_K, dtype=jnp.int32)
    slot_src = jnp.full((n_blocks * MOE_ROWS,), -1, jnp.int32).at[pos.reshape(-1)].set(slot_ids)
    block_start = jnp.arange(n_blocks, dtype=jnp.int32) * MOE_ROWS
    block_e = jnp.minimum(jnp.sum(pad_end[None, :] <= block_start[:, None], axis=1), N_EXPERTS - 1).astype(jnp.int32)
    n_active = (pad_end[-1:] // MOE_ROWS).astype(jnp.int32)
    n_valid = jnp.clip((pad_start + counts)[block_e] - block_start, 0, MOE_ROWS).astype(jnp.int32)
    hp_tiles = hp.reshape(n, d // 2 // LANES, LANES)
    y_tok = _moe_experts(layer, hp_tiles, slot_src, block_e, n_active, n_valid, w1, b1, w2, b2)
    return _moe_combine(y_tok, gates, x, gmod)


def kernel(x, c, ctx, c_ctx, ada_w, ada_b, norm_mix, norm_ffn, ev_w_in, ev_conv_w, ev_conv_b, ev_q_norm, ev_k_norm, ev_rpb, ev_a_log, ev_dt_bias, ev_d_skip, ev_gate_norm, ev_w_out, od_w_in, od_b_in, od_conv_w, od_conv_b, od_filt_w1, od_filt_b1, od_filt_w2, od_filt_b2, od_filt_w3, od_filt_b3, od_filt_w4, od_filt_freq, od_filt_bias, od_w_out, od_b_out, moe_router_w, moe_router_b, moe_w1, moe_b1, moe_w2, moe_b2):
    b, s, d = x.shape
    assert ada_w.shape[0] == 2 and ev_w_in.shape[0] == 1 and od_w_in.shape[0] == 1 and d == D_MODEL
    cc = jnp.zeros((16, d), F32).at[:b].set(c).at[b].set(c_ctx)
    mod = _ada_mod(cc, ada_w, ada_b)
    chunk = lambda m, k: m[:, k * d:(k + 1) * d]
    moe = (moe_w1, moe_b1, moe_w2, moe_b2)

    m0 = mod[0, :b]
    mc0 = jnp.broadcast_to(mod[0, b:b + 1], (b, N_MOD * d))
    w_in = ev_w_in[0]
    n_main = 2 * NA_WIDTH + SSM_D_INNER + NA_WIDTH + SSM_XBC
    kv0 = NA_WIDTH + SSM_D_INNER
    w_main = w_in[:, :n_main].astype(BF16)
    w_ctx = w_in[:, kv0:n_main].astype(BF16)
    w_dt = jnp.zeros((d, LANES), BF16).at[:, :2 * SSM_HEADS].set(w_in[:, n_main:].astype(BF16))
    zb = lambda n: jnp.zeros((n,), F32)
    g0 = norm_mix[0]
    proj = _norm_mod_matmul(x, g0, chunk(m0, 0), chunk(m0, 1), w_main, zb(n_main), BF16, 1024, 1024)
    dt_raw = _norm_mod_matmul(x, g0, chunk(m0, 0), chunk(m0, 1), w_dt, zb(LANES), F32, 1024, LANES)[..., :2 * SSM_HEADS]
    proj_c = _norm_mod_matmul(ctx, g0, chunk(mc0, 0), chunk(mc0, 1), w_ctx, zb(n_main - kv0), BF16, 256, 1024)
    dt_raw_c = _norm_mod_matmul(ctx, g0, chunk(mc0, 0), chunk(mc0, 1), w_dt, zb(LANES), F32, 256, LANES)[..., :2 * SSM_HEADS]
    attn = _attention(proj, proj_c, ev_q_norm[0], ev_k_norm[0], _attn_bias_table(ev_rpb[0], s // GRID_W))
    ssm_args = (ev_conv_w[0], ev_conv_b[0], ev_dt_bias[0], ev_a_log[0])
    x_blk_c = 2 * NA_WIDTH // GROUP_WIDTH
    states = _ssd(proj_c, x_blk_c, None, dt_raw_c, *ssm_args, None, None, None)
    x_blk = (kv0 + 2 * NA_WIDTH) // GROUP_WIDTH
    z_blk = NA_WIDTH // GROUP_WIDTH
    y_ssm = _ssd(proj, x_blk, z_blk, dt_raw, *ssm_args, ev_d_skip[0], ev_gate_norm[0], states)
    x = _proj_residual([attn, y_ssm], ev_w_out[0].astype(BF16), zb(d), chunk(m0, 2), x)
    x = _moe_ffn(0, x, norm_ffn[0], chunk(m0, 3), chunk(m0, 4), chunk(m0, 5), moe_router_w[0], moe_router_b[0], *moe)

    m1 = mod[1, :b]
    proj_h = _norm_mod_matmul(x, norm_mix[1], chunk(m1, 0), chunk(m1, 1), od_w_in[0].astype(BF16), od_b_in[0], BF16, 1024, 1024)
    filt = _hyena_filters(s, od_filt_w1[0], od_filt_b1[0], od_filt_w2[0], od_filt_b2[0], od_filt_w3[0], od_filt_b3[0],
                          od_filt_w4[0], od_filt_freq[0])
    fwd, inv = _dft_matrices(s)
    coef = _filter_spectrum(filt, fwd)
    x0, u = _hyena_conv(proj_h, od_conv_w[0], od_conv_b[0])
    spec = _dft_forward(u, fwd, coef)
    y_h = _dft_inverse(spec.reshape(b, 2 * s, d), inv, x0, u, od_filt_bias[0])
    x = _proj_residual([y_h], od_w_out[0].astype(BF16), od_b_out[0], chunk(m1, 2), x)
    x = _moe_ffn(1, x, norm_ffn[1], chunk(m1, 3), chunk(m1, 4), chunk(m1, 5), moe_router_w[1], moe_router_b[1], *moe)
    return x
```

```python
import functools
import math

import jax
import jax.numpy as jnp
import numpy as np
from jax import lax
from jax.experimental import pallas as pl
from jax.experimental.pallas import tpu as pltpu

F32 = jnp.float32
BF16 = jnp.bfloat16
HI = lax.Precision.HIGHEST

D_MODEL = 2048
N_MOD = 6
EPS = 1e-6
NEG_INF = -1e9
GRID_W = 64
NA_HEADS = 16
NA_HEAD_DIM = 128
NA_WIDTH = NA_HEADS * NA_HEAD_DIM
NA_KH = 8
NA_KW = 16
SSM_D_INNER = 2048
SSM_HEAD_DIM = 64
SSM_HEADS = SSM_D_INNER // SSM_HEAD_DIM
SSM_GROUPS = 8
SSM_STATE = 128
SSM_CHUNK = 128
SSM_XBC = SSM_D_INNER + 2 * SSM_GROUPS * SSM_STATE
HEADS_PER_GROUP = SSM_HEADS // SSM_GROUPS
GROUP_WIDTH = HEADS_PER_GROUP * SSM_HEAD_DIM
HY_EMB = 33
HY_FILTER_WIDTH = 64
HY_TARGET = 1e-2
HY_FAST_PCT = 0.3
HY_SLOW_PCT = 1.5
N_EXPERTS = 32
TOP_K = 4
MOE_FF = 2048
SWIGLU_LIMIT = 7.0
SWIGLU_ALPHA = 1.702
MOE_ROWS = 1024
MOE_SUB_ROWS = 512
MOE_FF_TILE = 256
MOE_N_FF = MOE_FF // MOE_FF_TILE
LANES = 128
MXU_COLS = 256
VMEM_LIMIT = 56 * 1024 * 1024


def _cparams(sem, vmem=VMEM_LIMIT):
    return pltpu.CompilerParams(dimension_semantics=sem, vmem_limit_bytes=vmem)


def _dot(a, b):
    return jnp.dot(a, b, preferred_element_type=F32)


def _dot_nt(a, b):
    return lax.dot_general(a, b, (((1,), (1,)), ((), ())), preferred_element_type=F32)


def _dot_tn(a, b):
    return lax.dot_general(a, b, (((0,), (0,)), ((), ())), preferred_element_type=F32)


def _silu(x):
    return x / (1.0 + jnp.exp(-x))


def _softplus(x):
    return jnp.maximum(x, 0.0) + jnp.log1p(jnp.exp(-jnp.abs(x)))


def _ada_kernel(c_ref, w_ref, b_ref, o_ref):
    sc = _silu(c_ref[...])
    hi = sc.astype(BF16)
    lo = (sc - hi.astype(F32)).astype(BF16)
    w = w_ref[0].astype(BF16)
    o_ref[0] = _dot(hi, w) + _dot(lo, w) + b_ref[0]


def _ada_mod(cc, ada_w, ada_b):
    depth, d, n = ada_w.shape
    tn = 1024
    return pl.pallas_call(
        _ada_kernel,
        grid=(depth, n // tn),
        in_specs=[
            pl.BlockSpec((cc.shape[0], d), lambda i, j: (0, 0)),
            pl.BlockSpec((1, d, tn), lambda i, j: (i, 0, j)),
            pl.BlockSpec((1, 1, tn), lambda i, j: (i, 0, j)),
        ],
        out_specs=pl.BlockSpec((1, cc.shape[0], tn), lambda i, j: (i, 0, j)),
        out_shape=jax.ShapeDtypeStruct((depth, cc.shape[0], n), F32),
        compiler_params=_cparams(("parallel", "parallel")),
        name="ada_mod",
    )(cc, ada_w, ada_b.reshape(depth, 1, n))


def _nmm_kernel(x_ref, g_ref, sh_ref, sc_ref, w_ref, b_ref, o_ref, h_scr):
    @pl.when(pl.program_id(2) == 0)
    def _():
        x = x_ref[0]
        ms = jnp.mean(x * x, axis=-1, keepdims=True)
        xn = x * lax.rsqrt(ms + EPS) * g_ref[...]
        h_scr[...] = (xn * (1.0 + sc_ref[0]) + sh_ref[0]).astype(BF16)

    o_ref[0] = (_dot(h_scr[...], w_ref[...]) + b_ref[...]).astype(o_ref.dtype)


def _norm_mod_matmul(x, g, shift, scale, w, bias, out_dtype, tm, tn):
    b, s, d = x.shape
    n = w.shape[1]
    tm = min(tm, s)
    return pl.pallas_call(
        _nmm_kernel,
        grid=(b, s // tm, n // tn),
        in_specs=[
            pl.BlockSpec((1, tm, d), lambda bi, i, j: (bi, i, 0)),
            pl.BlockSpec((1, d), lambda bi, i, j: (0, 0)),
            pl.BlockSpec((1, 1, d), lambda bi, i, j: (bi, 0, 0)),
            pl.BlockSpec((1, 1, d), lambda bi, i, j: (bi, 0, 0)),
            pl.BlockSpec((d, tn), lambda bi, i, j: (0, j)),
            pl.BlockSpec((1, tn), lambda bi, i, j: (0, j)),
        ],
        out_specs=pl.BlockSpec((1, tm, tn), lambda bi, i, j: (bi, i, j)),
        out_shape=jax.ShapeDtypeStruct((b, s, n), out_dtype),
        scratch_shapes=[pltpu.VMEM((tm, d), BF16)],
        compiler_params=_cparams(("parallel", "parallel", "arbitrary")),
        name="norm_mod_matmul",
    )(x, g.reshape(1, d), shift.reshape(b, 1, d), scale.reshape(b, 1, d), w, bias.reshape(1, n))


NA_PAIR_ROWS = NA_KH + 2
NA_VARIANTS = 5


def _pair_window_start(i, rows):
    return np.clip(2 * i - NA_KH // 2, 0, rows - NA_PAIR_ROWS)


def _attn_bias_table(rpb, rows):
    nh = rpb.shape[0]
    kc = np.arange(GRID_W)[:, None]
    qc = np.arange(GRID_W)[None, :]
    col_off = np.clip(kc - qc, -(NA_KW - 1), NA_KW - 1) + NA_KW - 1
    onehot = (col_off[None] == np.arange(2 * NA_KW - 1)[:, None, None]).astype(np.float32)
    toep = jnp.einsum('hrc,ckq->hrkq', rpb.astype(F32), onehot, precision=HI)
    ws_col = np.clip(qc - NA_KW // 2, 0, GRID_W - NA_KW)
    toep = jnp.where(((kc >= ws_col) & (kc < ws_col + NA_KW))[None, None], toep, NEG_INF)
    dead = jnp.full((nh, GRID_W, GRID_W), NEG_INF, F32)
    rep_pair = [0, 1, 2, rows // 2 - 2, rows // 2 - 1]
    blocks = []
    for i in rep_pair:
        ws = int(_pair_window_start(i, rows))
        for t in range(NA_PAIR_ROWS):
            for e in range(2):
                r = 2 * i + e
                rs = int(np.clip(r - NA_KH // 2, 0, rows - NA_KH))
                kr = ws + t
                blocks.append(toep[:, kr - r + NA_KH - 1] if rs <= kr < rs + NA_KH else dead)
    bias = jnp.stack(blocks, axis=1).reshape(nh, NA_VARIANTS, NA_PAIR_ROWS, 2, GRID_W, GRID_W)
    return bias.transpose(0, 1, 2, 4, 3, 5).reshape(nh, NA_VARIANTS, NA_PAIR_ROWS * GRID_W, 2 * GRID_W)


def _head_rmsnorm(t, g):
    sq = t * t
    hi = sq.astype(BF16)
    lo = (sq - hi.astype(F32)).astype(BF16)
    avg = jnp.full((NA_HEAD_DIM, NA_HEAD_DIM), 1.0 / NA_HEAD_DIM, BF16)
    ms = _dot(hi, avg) + _dot(lo, avg)
    return t * lax.rsqrt(ms + EPS) * g


def _attn_kernel(q_ref, k_ref, v_ref, kc_ref, vc_ref, qg_ref, kg_ref, bias_ref, o_ref, qs, ks, kcs, vt, vct):
    seq, hd = q_ref.shape[1], q_ref.shape[2]
    rows = seq // GRID_W
    pair = 2 * GRID_W
    win_blocks = NA_PAIR_ROWS // 2
    ctx_blocks = kc_ref.shape[1] // pair
    qs[...] = (_head_rmsnorm(q_ref[0].astype(F32), qg_ref[...]) * hd ** -0.5).astype(BF16)
    ks[...] = _head_rmsnorm(k_ref[0].astype(F32), kg_ref[...]).astype(BF16)
    kcs[...] = _head_rmsnorm(kc_ref[0].astype(F32), kg_ref[...]).astype(BF16)
    for blk in range(seq // pair):
        vt[blk] = v_ref[0, blk * pair:(blk + 1) * pair, :].astype(F32).T.astype(BF16)
    for blk in range(ctx_blocks):
        vct[blk] = vc_ref[0, blk * pair:(blk + 1) * pair, :].astype(F32).T.astype(BF16)

    def body(i, carry):
        ws = jnp.clip(2 * i - NA_KH // 2, 0, rows - NA_PAIR_ROWS)
        var = (2 * i - ws) // 2
        wb = ws // 2
        q0 = pl.multiple_of(i * pair, pair)
        k0 = pl.multiple_of(ws * GRID_W, pair)
        q_p = qs[pl.ds(q0, pair), :]
        s_lat = _dot_nt(ks[pl.ds(k0, win_blocks * pair), :], q_p) + bias_ref[0, var]
        s_ctx = _dot_nt(kcs[...], q_p)
        m = jnp.maximum(jnp.max(s_lat, axis=0, keepdims=True), jnp.max(s_ctx, axis=0, keepdims=True))
        p_lat = jnp.exp(s_lat - m)
        p_ctx = jnp.exp(s_ctx - m)
        den = jnp.sum(p_lat, axis=0, keepdims=True) + jnp.sum(p_ctx, axis=0, keepdims=True)
        p_lat = p_lat.astype(BF16)
        p_ctx = p_ctx.astype(BF16)
        o_t = jnp.zeros((hd, pair), F32)
        for blk in range(win_blocks):
            o_t = o_t + _dot(vt[wb + blk], p_lat[blk * pair:(blk + 1) * pair, :])
        for blk in range(ctx_blocks):
            o_t = o_t + _dot(vct[blk], p_ctx[blk * pair:(blk + 1) * pair, :])
        o_ref[0, pl.ds(q0, pair), :] = (o_t / den).T.astype(o_ref.dtype)
        return carry

    lax.fori_loop(0, rows // 2, body, 0, unroll=2)


def _attention(proj, proj_c, q_norm, k_norm, bias_tab):
    b, s, _ = proj.shape
    ctx = proj_c.shape[1]
    hd = NA_HEAD_DIM
    nh = NA_HEADS
    return pl.pallas_call(
        _attn_kernel,
        grid=(b, nh),
        in_specs=[
            pl.BlockSpec((1, s, hd), lambda bi, h: (bi, 0, h)),
            pl.BlockSpec((1, s, hd), lambda bi, h: (bi, 0, 2 * nh + h)),
            pl.BlockSpec((1, s, hd), lambda bi, h: (bi, 0, 3 * nh + h)),
            pl.BlockSpec((1, ctx, hd), lambda bi, h: (bi, 0, h)),
            pl.BlockSpec((1, ctx, hd), lambda bi, h: (bi, 0, nh + h)),
            pl.BlockSpec((1, hd), lambda bi, h: (0, 0)),
            pl.BlockSpec((1, hd), lambda bi, h: (0, 0)),
            pl.BlockSpec((1,) + bias_tab.shape[1:], lambda bi, h: (h, 0, 0, 0)),
        ],
        out_specs=pl.BlockSpec((1, s, hd), lambda bi, h: (bi, 0, h)),
        out_shape=jax.ShapeDtypeStruct((b, s, nh * hd), BF16),
        scratch_shapes=[pltpu.VMEM((s, hd), BF16), pltpu.VMEM((s, hd), BF16), pltpu.VMEM((ctx, hd), BF16),
                        pltpu.VMEM((s // (2 * GRID_W), hd, 2 * GRID_W), BF16), pltpu.VMEM((ctx // (2 * GRID_W), hd, 2 * GRID_W), BF16)],
        compiler_params=_cparams(("parallel", "parallel")),
        name="nbr_attention",
    )(proj, proj, proj, proj_c, proj_c, q_norm.reshape(1, hd), k_norm.reshape(1, hd), bias_tab)


def _split3(v):
    hi = v.astype(BF16)
    r1 = v - hi.astype(F32)
    mid = r1.astype(BF16)
    lo = (r1 - mid.astype(F32)).astype(BF16)
    return hi, mid, lo


def _ssd_kernel(*refs, with_y, nc):
    (xr_ref, br_ref, cr_ref, cwx, cbx, cwb, cbb, cwc, cbc, dt_ref, bias_ref, a_ref), rest = refs[:12], refs[12:]
    if with_y:
        (z_ref, dsk_ref, gn_ref, sf0_ref, sb0_ref, y_ref, xs, bs, cs, contrib, dec, wrow, xf, sfa, sba, drow, dtrow, dsplit, esplit) = rest
    else:
        sf_out, sb_out, xs, bs, cs, contrib, dec, wrow = rest
    ck = SSM_CHUNK
    length = nc * ck
    nh = HEADS_PER_GROUP
    nhd = 2 * nh
    gw = GROUP_WIDTH
    dot_hi = functools.partial(jnp.dot, precision=HI, preferred_element_type=F32)

    halo = 16
    win = ck + 2 * halo
    conv_w = jnp.concatenate([cwx[...], cwb[...], cwc[...]], axis=1)
    conv_b = jnp.concatenate([cbx[...], cbb[...], cbc[...]], axis=1)
    wi = lax.broadcasted_iota(jnp.int32, (ck, win), 1) - lax.broadcasted_iota(jnp.int32, (ck, win), 0)

    def conv_silu_chunk(c):
        l0 = pl.multiple_of(c * ck, ck)
        s0 = pl.multiple_of(jnp.clip(l0 - halo, 0, length - win), halo)
        rel = wi + (s0 - l0)
        window = jnp.concatenate([r[0, pl.ds(s0, win), :] for r in (xr_ref, br_ref, cr_ref)], axis=1)
        cur = jnp.concatenate([r[0, pl.ds(l0, ck), :] for r in (xr_ref, br_ref, cr_ref)], axis=1).astype(F32)
        prev = _dot((rel == -1).astype(BF16), window)
        nxt = _dot((rel == 1).astype(BF16), window)
        y = _silu(conv_b + prev * conv_w[0:1, :] + cur * conv_w[1:2, :] + nxt * conv_w[2:3, :])
        xs[pl.ds(l0, ck), :] = y[:, :gw].astype(BF16)
        if with_y:
            xf[pl.ds(l0, ck), :] = y[:, :gw]
        bs[pl.ds(l0, ck), :] = y[:, gw:gw + SSM_STATE].astype(BF16)
        cs[pl.ds(l0, ck), :] = y[:, gw + SSM_STATE:].astype(BF16)
        return y[:, :gw], y[:, gw:gw + SSM_STATE].astype(BF16)

    ri = lax.broadcasted_iota(jnp.int32, (ck, ck), 0)
    ci = lax.broadcasted_iota(jnp.int32, (ck, ck), 1)
    lower = ci <= ri
    upper = ci >= ri
    eye = (ci == ri).astype(BF16)
    dt_row = _softplus(dt_ref[0, 0] + bias_ref[0])
    a_row = dt_row * a_ref[0]
    cum_f = dot_hi(a_row, upper.astype(F32))
    cum_b = dot_hi(a_row, lower.astype(F32))
    tot = dot_hi(a_row, jnp.ones((ck, ck), F32))
    d_row = jnp.where((ri % nhd) < nh, cum_f, cum_b)
    wrow[...] = jnp.exp(tot - d_row) * dt_row
    texp = jnp.exp(tot)
    row4 = lax.broadcasted_iota(jnp.int32, (ck, 2 * gw), 0)
    lane4 = lax.broadcasted_iota(jnp.int32, (ck, 2 * gw), 1) // SSM_HEAD_DIM
    tmask = jnp.where(row4 % nhd == lane4, jnp.concatenate([texp] * (2 * gw // ck), axis=1), 0.0)
    pick = (lax.broadcasted_iota(jnp.int32, (dec.shape[0], ck), 1) // nhd
            == lax.broadcasted_iota(jnp.int32, (dec.shape[0], ck), 0)).astype(F32)
    dec[...] = dot_hi(pick, tmask)
    if with_y:
        drow[...] = d_row
        dtrow[...] = dt_row
        for k, part in enumerate(_split3(d_row)):
            dsplit[k] = part.astype(F32)
        for k, part in enumerate(_split3(jnp.exp(d_row))[:2]):
            esplit[k] = part.astype(F32)

    def spread(src, r0, width):
        rows = jnp.concatenate([jnp.broadcast_to(src[pl.ds(r0 + hd, 1), :], (width, ck)) for hd in range(nhd)], axis=0)
        return _dot_nt(eye, rows.astype(BF16))

    def chunk_states(c, carry):
        xc, b_c = conv_silu_chunk(c)
        wx = spread(wrow, c * nhd, SSM_HEAD_DIM)
        xw = jnp.concatenate([(xc * wx[:, :gw]).astype(BF16), (xc * wx[:, gw:]).astype(BF16)], axis=1)
        contrib[c] = _dot_tn(b_c, xw)
        return carry

    lax.fori_loop(0, nc, chunk_states, 0, unroll=2)

    def fwd_chain(c, s):
        if with_y:
            sfa[c] = s
        return s * dec[pl.ds(c, 1), :gw] + contrib[c, :, :gw]

    def bwd_chain(i, s):
        c = nc - 1 - i
        if with_y:
            sba[c] = s
        return s * dec[pl.ds(c, 1), gw:] + contrib[c, :, gw:]

    if with_y:
        s_f0 = sf0_ref[0, 0]
        s_b0 = sb0_ref[0, 0]
    else:
        s_f0 = jnp.zeros((SSM_STATE, gw), F32)
        s_b0 = s_f0
    s_f = lax.fori_loop(0, nc, fwd_chain, s_f0)
    s_b = lax.fori_loop(0, nc, bwd_chain, s_b0)
    if not with_y:
        sf_out[0, 0] = s_f
        sb_out[0, 0] = s_b
        return

    lane_head = lax.broadcasted_iota(jnp.int32, (ck, gw), 1) // SSM_HEAD_DIM

    def out_step(c, carry):
        l0 = pl.multiple_of(c * ck, ck)
        r0 = c * nhd
        d_b = spread(dsplit.at[0], r0, ck) + spread(dsplit.at[1], r0, ck) + spread(dsplit.at[2], r0, ck)
        e_b = spread(esplit.at[0], r0, SSM_HEAD_DIM) + spread(esplit.at[1], r0, SSM_HEAD_DIM)
        b_c = bs[pl.ds(l0, ck), :]
        c_c = cs[pl.ds(l0, ck), :]
        x_c = xs[pl.ds(l0, ck), :]
        g = _dot_nt(c_c, b_c)
        acc = jnp.zeros((ck, gw), F32)
        for j in range(nh):
            jb = nh + j
            lf = jnp.exp(jnp.where(lower, d_b[:, j * ck:(j + 1) * ck] - drow[pl.ds(r0 + j, 1), :], -1e30)) * dtrow[pl.ds(r0 + j, 1), :]
            lb = jnp.exp(jnp.where(upper, d_b[:, jb * ck:(jb + 1) * ck] - drow[pl.ds(r0 + jb, 1), :], -1e30)) * dtrow[pl.ds(r0 + jb, 1), :]
            m = (g * (lf + lb)).astype(BF16)
            acc = acc + _dot(m, jnp.where(lane_head == j, x_c, jnp.zeros_like(x_c)))
        acc = acc + _dot(c_c, sfa[c].astype(BF16)) * e_b[:, :gw]
        acc = acc + _dot(c_c, sba[c].astype(BF16)) * e_b[:, gw:]
        y = acc + xf[pl.ds(l0, ck), :] * dsk_ref[...]
        y = y * _silu(z_ref[0, pl.ds(l0, ck), :].astype(F32))
        ms = jnp.mean(y * y, axis=-1, keepdims=True)
        y_ref[0, pl.ds(l0, ck), :] = (y * lax.rsqrt(ms + EPS) * gn_ref[...]).astype(y_ref.dtype)
        return carry

    lax.fori_loop(0, nc, out_step, 0, unroll=2)


def _ssd(proj, x_blk0, z_blk0, dt_raw, conv_w, conv_b, dt_bias, a_log, d_skip, gate_norm, init):
    b, length, _ = proj.shape
    nc = length // SSM_CHUNK
    ng, nh, gw, st = SSM_GROUPS, HEADS_PER_GROUP, GROUP_WIDTH, SSM_STATE
    with_y = init is not None
    b_blk0 = x_blk0 * (gw // st) + SSM_D_INNER // st
    c_blk0 = b_blk0 + ng
    ck = SSM_CHUNK
    nhd = 2 * nh
    assert nc * nhd <= ck
    dtg = dt_raw.reshape(b, nc, ck, 2, ng, nh).transpose(0, 4, 1, 3, 5, 2).reshape(b, ng, nc * nhd, ck)
    dtg = jnp.pad(dtg, ((0, 0), (0, 0), (0, ck - nc * nhd), (0, 0)))
    per_row = lambda t: jnp.tile(t.reshape(2, ng, nh).transpose(1, 0, 2).reshape(ng, nhd), (1, ck // nhd)).reshape(ng, ck, 1)
    bias_r = per_row(dt_bias)
    a_r = per_row(-jnp.exp(a_log.astype(F32)))
    cw_x, cw_b, cw_c = conv_w[:, :SSM_D_INNER], conv_w[:, SSM_D_INNER:SSM_D_INNER + ng * st], conv_w[:, SSM_D_INNER + ng * st:]
    cb = conv_b.reshape(1, -1)
    cb_x, cb_b, cb_c = cb[:, :SSM_D_INNER], cb[:, SSM_D_INNER:SSM_D_INNER + ng * st], cb[:, SSM_D_INNER + ng * st:]
    in_specs = [
        pl.BlockSpec((1, length, gw), lambda bi, g: (bi, 0, x_blk0 + g)),
        pl.BlockSpec((1, length, st), lambda bi, g: (bi, 0, b_blk0 + g)),
        pl.BlockSpec((1, length, st), lambda bi, g: (bi, 0, c_blk0 + g)),
        pl.BlockSpec((3, gw), lambda bi, g: (0, g)),
        pl.BlockSpec((1, gw), lambda bi, g: (0, g)),
        pl.BlockSpec((3, st), lambda bi, g: (0, g)),
        pl.BlockSpec((1, st), lambda bi, g: (0, g)),
        pl.BlockSpec((3, st), lambda bi, g: (0, g)),
        pl.BlockSpec((1, st), lambda bi, g: (0, g)),
        pl.BlockSpec((1, 1, ck, ck), lambda bi, g: (bi, g, 0, 0)),
        pl.BlockSpec((1, ck, 1), lambda bi, g: (g, 0, 0)),
        pl.BlockSpec((1, ck, 1), lambda bi, g: (g, 0, 0)),
    ]
    args = [proj, proj, proj, cw_x, cb_x, cw_b, cb_b, cw_c, cb_c, dtg, bias_r, a_r]
    scratch = [pltpu.VMEM((length, gw), BF16), pltpu.VMEM((length, st), BF16), pltpu.VMEM((length, st), BF16),
               pltpu.VMEM((nc, st, 2 * gw), F32), pltpu.VMEM((ck // nhd, 2 * gw), F32), pltpu.VMEM((ck, ck), F32)]
    state_spec = pl.BlockSpec((1, 1, st, gw), lambda bi, g: (bi, g, 0, 0))
    state_shape = jax.ShapeDtypeStruct((b, ng, st, gw), F32)
    if with_y:
        in_specs += [
            pl.BlockSpec((1, length, gw), lambda bi, g: (bi, 0, z_blk0 + g)),
            pl.BlockSpec((1, gw), lambda bi, g: (0, g)),
            pl.BlockSpec((1, gw), lambda bi, g: (0, g)),
            state_spec, state_spec,
        ]
        args += [proj, jnp.repeat(d_skip, SSM_HEAD_DIM).reshape(1, -1), gate_norm.reshape(1, -1), init[0], init[1]]
        out_specs = pl.BlockSpec((1, length, gw), lambda bi, g: (bi, 0, g))
        out_shape = jax.ShapeDtypeStruct((b, length, SSM_D_INNER), BF16)
        scratch += [pltpu.VMEM((length, gw), F32), pltpu.VMEM((nc, st, gw), F32), pltpu.VMEM((nc, st, gw), F32),
                    pltpu.VMEM((ck, ck), F32), pltpu.VMEM((ck, ck), F32), pltpu.VMEM((3, ck, ck), F32), pltpu.VMEM((2, ck, ck), F32)]
    else:
        out_specs = [state_spec, state_spec]
        out_shape = [state_shape, state_shape]
    return pl.pallas_call(
        functools.partial(_ssd_kernel, with_y=with_y, nc=nc),
        grid=(b, ng),
        in_specs=in_specs,
        out_specs=out_specs,
        out_shape=out_shape,
        scratch_shapes=scratch,
        compiler_params=_cparams(("parallel", "parallel")),
        name="ssd_main" if with_y else "ssd_ctx",
    )(*args)


def _proj_res_kernel(*refs, n_lhs):
    a_refs, w_refs = refs[:n_lhs], refs[n_lhs:2 * n_lhs]
    b_ref, gate_ref, res_ref, o_ref = refs[2 * n_lhs:]
    acc = _dot(a_refs[0][0], w_refs[0][...])
    for a_ref, w_ref in zip(a_refs[1:], w_refs[1:]):
        acc = acc + _dot(a_ref[0], w_ref[...])
    o_ref[0] = res_ref[0] + gate_ref[0] * (acc + b_ref[...])


def _proj_residual(lhs, w, bias, gate, resid, tm=1024, tn=1024):
    b, s, k = lhs[0].shape
    n = w.shape[1]
    n_lhs = len(lhs)
    in_specs = [pl.BlockSpec((1, tm, k), lambda bi, i, j: (bi, i, 0)) for _ in lhs]
    in_specs += [pl.BlockSpec((k, tn), functools.partial(lambda bi, i, j, t: (t, j), t=t)) for t in range(n_lhs)]
    in_specs += [
        pl.BlockSpec((1, tn), lambda bi, i, j: (0, j)),
        pl.BlockSpec((1, 1, tn), lambda bi, i, j: (bi, 0, j)),
        pl.BlockSpec((1, tm, tn), lambda bi, i, j: (bi, i, j)),
    ]
    return pl.pallas_call(
        functools.partial(_proj_res_kernel, n_lhs=n_lhs),
        grid=(b, s // tm, n // tn),
        in_specs=in_specs,
        out_specs=pl.BlockSpec((1, tm, tn), lambda bi, i, j: (bi, i, j)),
        out_shape=jax.ShapeDtypeStruct((b, s, n), F32),
        compiler_params=_cparams(("parallel", "parallel", "parallel")),
        name="proj_residual",
    )(*lhs, *([w] * n_lhs), bias.reshape(1, n), gate.reshape(b, 1, n), resid)


def _hy_filter_kernel(tw_ref, f_ref, w1t, w1c, w1s, b1, w2, b2, w3, b3, fr, w4_ref, dl_ref, o_ref, h_scr):
    dot_hi = lambda a, b: jnp.dot(a, b, precision=HI, preferred_element_type=F32)

    @pl.when(pl.program_id(0) == 0)
    def _():
        t = tw_ref[:, 0:1]
        ang = tw_ref[:, 1:2] * f_ref[...]
        pre = t * w1t[...] + dot_hi(jnp.cos(ang), w1c[...]) + dot_hi(-jnp.sin(ang), w1s[...]) + b1[...]
        h = jnp.sin(fr[...] * pre)
        h = jnp.sin(fr[...] * (dot_hi(h, w2[...]) + b2[...]))
        h_scr[...] = jnp.sin(fr[...] * (dot_hi(h, w3[...]) + b3[...]))

    o_ref[...] = dot_hi(h_scr[...], w4_ref[...]) * jnp.exp(-tw_ref[:, 0:1] * dl_ref[...])


def _hyena_filters(length, w1, b1, w2, b2, w3, b3, w4, freq):
    bands = (HY_EMB - 1) // 2
    fw = HY_FILTER_WIDTH
    t = jnp.linspace(0.0, 1.0, length, dtype=F32)
    w = 2 * math.pi * jnp.arange(length, dtype=F32) / length
    tw = jnp.stack([t, w], axis=1)
    f = jnp.linspace(1e-4, bands - 1, bands, dtype=F32).reshape(1, bands)
    min_decay = math.log(HY_TARGET) / HY_SLOW_PCT
    max_decay = math.log(HY_TARGET) / HY_FAST_PCT
    deltas = jnp.abs(jnp.linspace(min_decay, max_decay, D_MODEL, dtype=F32))
    dl = jnp.concatenate([deltas, deltas]).reshape(1, 2 * D_MODEL)
    tn = 1024
    small = lambda shape: pl.BlockSpec(shape, lambda j: (0, 0))
    row = lambda v: v.reshape(1, -1)
    return pl.pallas_call(
        _hy_filter_kernel,
        grid=(2 * D_MODEL // tn,),
        in_specs=[small((length, 2)), small((1, bands)), small((1, fw)), small((bands, fw)), small((bands, fw)), small((1, fw)),
                  small((fw, fw)), small((1, fw)), small((fw, fw)), small((1, fw)), small((1, fw)),
                  pl.BlockSpec((fw, tn), lambda j: (0, j)), pl.BlockSpec((1, tn), lambda j: (0, j))],
        out_specs=pl.BlockSpec((length, tn), lambda j: (0, j)),
        out_shape=jax.ShapeDtypeStruct((length, 2 * D_MODEL), F32),
        scratch_shapes=[pltpu.VMEM((length, fw), F32)],
        compiler_params=_cparams(("arbitrary",)),
        name="hyena_filters",
    )(tw, f, w1[0:1], w1[1:1 + bands], w1[1 + bands:], row(b1), w2, row(b2), w3, row(b3), row(freq), w4, dl)


def _hy_conv_kernel(x0_ref, x1_ref, v_ref, w0, b0, w1, b1, w2, b2, x0_out, u_out):
    length = x0_ref.shape[1]

    def conv(raw_ref, w_ref, b_ref):
        x = raw_ref[0].astype(F32)
        rows = lax.broadcasted_iota(jnp.int32, x.shape, 0)
        xm1 = jnp.where(rows == 0, 0.0, pltpu.roll(x, 1, 0))
        xp1 = jnp.where(rows == length - 1, 0.0, pltpu.roll(x, length - 1, 0))
        return b_ref[...] + xm1 * w_ref[0:1, :] + x * w_ref[1:2, :] + xp1 * w_ref[2:3, :]

    x0_out[0] = conv(x0_ref, w0, b0).astype(x0_out.dtype)
    u_out[0] = (conv(v_ref, w2, b2) * conv(x1_ref, w1, b1)).astype(u_out.dtype)


def _hyena_conv(proj, conv_w, conv_b):
    b, length, _ = proj.shape
    d = D_MODEL
    tn = 512
    nb = d // tn
    cb = conv_b.reshape(1, -1)
    seg = lambda k: pl.BlockSpec((1, length, tn), lambda bi, j: (bi, 0, k * nb + j))
    wseg = lambda k: pl.BlockSpec((3, tn), lambda bi, j: (0, k * nb + j))
    bseg = lambda k: pl.BlockSpec((1, tn), lambda bi, j: (0, k * nb + j))
    out = pl.BlockSpec((1, length, tn), lambda bi, j: (bi, 0, j))
    return pl.pallas_call(
        _hy_conv_kernel,
        grid=(b, nb),
        in_specs=[seg(0), seg(1), seg(2), wseg(0), bseg(0), wseg(1), bseg(1), wseg(2), bseg(2)],
        out_specs=[out, out],
        out_shape=[jax.ShapeDtypeStruct((b, length, d), BF16)] * 2,
        compiler_params=_cparams(("parallel", "parallel")),
        name="hyena_conv",
    )(proj, proj, proj, conv_w, cb, conv_w, cb, conv_w, cb)


def _dft_matrices(length):
    n = 2 * length
    f = lax.broadcasted_iota(jnp.int32, (length, length), 0)
    t = lax.broadcasted_iota(jnp.int32, (length, length), 1)
    ang = ((f * t) % n).astype(F32) * (2 * math.pi / n)
    sign = (1 - 2 * (t % 2)).astype(F32)
    fwd_c = jnp.cos(ang)
    fwd_s = jnp.where(f == 0, sign, -jnp.sin(ang))
    fwd = jnp.concatenate([fwd_c, fwd_s], axis=0).astype(BF16)
    wgt = jnp.where(f == 0, 1.0, 2.0) / n
    inv = jnp.concatenate([(fwd_c * wgt).T, (fwd_s * wgt).T], axis=1).astype(BF16)
    return fwd, inv


def _mm_kernel(a_ref, b_ref, o_ref):
    o_ref[...] = _dot(a_ref[...], b_ref[...]).astype(o_ref.dtype)


def _matmul(a, b, out_dtype=F32, tm=1024, tn=1024):
    m, k = a.shape
    n = b.shape[1]
    return pl.pallas_call(
        _mm_kernel,
        grid=(n // tn, m // tm),
        in_specs=[pl.BlockSpec((tm, k), lambda j, i: (i, 0)), pl.BlockSpec((k, tn), lambda j, i: (0, j))],
        out_specs=pl.BlockSpec((tm, tn), lambda j, i: (i, j)),
        out_shape=jax.ShapeDtypeStruct((m, n), out_dtype),
        compiler_params=_cparams(("parallel", "parallel")),
        name="matmul",
    )(a, b)


def _filter_spectrum(kk, fwd):
    length = kk.shape[0]
    d = D_MODEL
    k_f, k_b = kk[:, :d], kk[:, d:]
    k_fp = k_f.at[0].add(k_b[0])
    k_bp = k_b.at[0].set(0.0)
    spec = _matmul(fwd, jnp.concatenate([k_fp, k_bp], axis=1).astype(BF16))
    a_re, a_im, b_re, b_im = spec[:length, :d], spec[length:, :d], spec[:length, d:], spec[length:, d:]
    nz = (jnp.arange(length) != 0)[:, None]
    k_re = a_re + b_re
    k_im = jnp.where(nz, a_im - b_im, a_im + b_im)
    return jnp.stack([k_re, jnp.where(nz, k_im, 0.0), jnp.where(nz, k_re, k_im)])


def _dft_fwd_kernel(fc_ref, fs_ref, u_ref, k_ref, y_ref):
    u = u_ref[0]
    re = _dot(fc_ref[...], u)
    im = _dot(fs_ref[...], u)
    y_ref[0, 0] = (re * k_ref[0] - im * k_ref[1]).astype(y_ref.dtype)
    y_ref[0, 1] = (re * k_ref[1] + im * k_ref[2]).astype(y_ref.dtype)


def _dft_forward(u, fwd, coef, tm=1024, tn=512):
    b, length, d = u.shape
    ni = length // tm
    return pl.pallas_call(
        _dft_fwd_kernel,
        grid=(d // tn, ni, b),
        in_specs=[
            pl.BlockSpec((tm, length), lambda j, i, bi: (i, 0)),
            pl.BlockSpec((tm, length), lambda j, i, bi: (ni + i, 0)),
            pl.BlockSpec((1, length, tn), lambda j, i, bi: (bi, 0, j)),
            pl.BlockSpec((3, tm, tn), lambda j, i, bi: (0, i, j)),
        ],
        out_specs=pl.BlockSpec((1, 2, tm, tn), lambda j, i, bi: (bi, 0, i, j)),
        out_shape=jax.ShapeDtypeStruct((b, 2, length, d), BF16),
        compiler_params=_cparams(("parallel", "parallel", "parallel")),
        name="dft_forward",
    )(fwd, fwd, u, coef)


def _dft_inv_kernel(g_ref, y_ref, x0_ref, u_ref, fb_ref, o_ref):
    conv = _dot(g_ref[...], y_ref[0])
    o_ref[0] = (x0_ref[0].astype(F32) * (conv + u_ref[0].astype(F32) * fb_ref[...])).astype(o_ref.dtype)


def _dft_inverse(y, inv, x0, u, filt_bias, tm=512, tn=512):
    b, n2, d = y.shape
    length = n2 // 2
    return pl.pallas_call(
        _dft_inv_kernel,
        grid=(d // tn, length // tm, b),
        in_specs=[
            pl.BlockSpec((tm, n2), lambda j, i, bi: (i, 0)),
            pl.BlockSpec((1, n2, tn), lambda j, i, bi: (bi, 0, j)),
            pl.BlockSpec((1, tm, tn), lambda j, i, bi: (bi, i, j)),
            pl.BlockSpec((1, tm, tn), lambda j, i, bi: (bi, i, j)),
            pl.BlockSpec((1, tn), lambda j, i, bi: (0, j)),
        ],
        out_specs=pl.BlockSpec((1, tm, tn), lambda j, i, bi: (bi, i, j)),
        out_shape=jax.ShapeDtypeStruct((b, length, d), BF16),
        compiler_params=_cparams(("parallel", "parallel", "parallel")),
        name="dft_inverse",
    )(inv, y, x0, u, filt_bias.reshape(1, d))


PACK_HI = 0xFFFF0000
TILE_SUBLANES = 8


def _pack_pairs(lo, hi):
    lo_b = pltpu.bitcast(lo.astype(BF16).astype(F32), jnp.uint32)
    hi_b = pltpu.bitcast(hi.astype(BF16).astype(F32), jnp.uint32)
    return (lo_b >> 16) | (hi_b & jnp.uint32(PACK_HI))


def _unpack_pairs(w):
    return pltpu.bitcast(w << 16, F32), pltpu.bitcast(w & jnp.uint32(PACK_HI), F32)


def _router_kernel(x_ref, g_ref, sh_ref, sc_ref, wr_ref, br_ref, hp_ref, idx_ref, gate_ref, rank_ref, cnt_ref, run):
    tm = x_ref.shape[1]
    half = x_ref.shape[2] // 2
    first = (pl.program_id(0) == 0) & (pl.program_id(1) == 0)

    @pl.when(first)
    def _():
        run[...] = jnp.zeros_like(run)

    x = x_ref[0]
    ms = jnp.mean(x * x, axis=-1, keepdims=True)
    h = x * lax.rsqrt(ms + EPS) * g_ref[...] * (1.0 + sc_ref[0]) + sh_ref[0]
    hp_ref[0] = _pack_pairs(h[:, :half], h[:, half:])

    logits = jnp.dot(h, wr_ref[...], precision=HI, preferred_element_type=F32) + br_ref[...]
    lane_i = lax.broadcasted_iota(jnp.int32, logits.shape, 1)
    lane = lane_i.astype(F32)
    work = jnp.where(lane_i < N_EXPERTS, logits, -jnp.inf)
    li = lax.broadcasted_iota(jnp.int32, (tm, tm), 0)
    ki = lax.broadcasted_iota(jnp.int32, (tm, tm), 1)
    strict_lower = (ki < li).astype(BF16)
    sels, tops = [], []
    for _ in range(TOP_K):
        m = jnp.max(work, axis=-1, keepdims=True)
        first_idx = jnp.min(jnp.where(work == m, lane, float(LANES)), axis=-1, keepdims=True)
        sel = lane == first_idx
        sels.append(sel)
        tops.append((m, first_idx))
        work = jnp.where(sel, -jnp.inf, work)
    chosen = sels[0] | sels[1] | sels[2] | sels[3]
    before = _dot(strict_lower, chosen.astype(BF16)) + run[...]
    run[...] = run[...] + jnp.sum(chosen.astype(F32), axis=0, keepdims=True)
    den = sum(jnp.exp(m - tops[0][0]) for m, _ in tops)
    idx_o = jnp.zeros(logits.shape, jnp.int32)
    gate_o = jnp.zeros(logits.shape, F32)
    rank_o = jnp.zeros(logits.shape, jnp.int32)
    for k, (sel, (m, first_idx)) in enumerate(zip(sels, tops)):
        rank = jnp.sum(jnp.where(sel, before, 0.0), axis=-1, keepdims=True).astype(jnp.int32)
        idx_o = jnp.where(lane_i == k, first_idx.astype(jnp.int32), idx_o)
        gate_o = jnp.where(lane_i == k, jnp.exp(m - tops[0][0]) / den, gate_o)
        rank_o = jnp.where(lane_i == k, rank, rank_o)
    idx_ref[0] = idx_o
    gate_ref[0] = gate_o
    rank_ref[0] = rank_o
    cnt_ref[...] = run[...]


def _moe_route(x, g, shift, scale, w_r, b_r, tm=512):
    b, s, d = x.shape
    wr = jnp.zeros((d, LANES), F32).at[:, :N_EXPERTS].set(w_r)
    br = jnp.zeros((1, LANES), F32).at[0, :N_EXPERTS].set(b_r)
    tok = lambda n: pl.BlockSpec((1, tm, n), lambda bi, i: (bi, i, 0))
    return pl.pallas_call(
        _router_kernel,
        grid=(b, s // tm),
        in_specs=[
            tok(d),
            pl.BlockSpec((1, d), lambda bi, i: (0, 0)),
            pl.BlockSpec((1, 1, d), lambda bi, i: (bi, 0, 0)),
            pl.BlockSpec((1, 1, d), lambda bi, i: (bi, 0, 0)),
            pl.BlockSpec((d, LANES), lambda bi, i: (0, 0)),
            pl.BlockSpec((1, LANES), lambda bi, i: (0, 0)),
        ],
        out_specs=[tok(d // 2), tok(LANES), tok(LANES), tok(LANES), pl.BlockSpec((1, LANES), lambda bi, i: (0, 0))],
        out_shape=[
            jax.ShapeDtypeStruct((b, s, d // 2), jnp.uint32),
            jax.ShapeDtypeStruct((b, s, LANES), jnp.int32),
            jax.ShapeDtypeStruct((b, s, LANES), F32),
            jax.ShapeDtypeStruct((b, s, LANES), jnp.int32),
            jax.ShapeDtypeStruct((1, LANES), F32),
        ],
        scratch_shapes=[pltpu.VMEM((1, LANES), F32)],
        compiler_params=_cparams(("arbitrary", "arbitrary")),
        name="moe_route",
    )(x, g.reshape(1, d), shift.reshape(b, 1, d), scale.reshape(b, 1, d), wr, br)


MOE_PITCH = TILE_SUBLANES + 1


def _ffn_kernel(be_ref, na_ref, nv_ref, dst_prv, tok_cur, tok_nxt, hp_ref, w1g_ref, w1u_ref, b1g_ref, b1u_ref, w2_ref, b2_ref, y_ref,
                xg, stg, xb, acc, wb1, wb2, zeros, gsem, usem, zsem):
    i = pl.program_id(0)
    f = pl.program_id(1)
    nb = pl.num_programs(0)
    nf = pl.num_programs(1)
    tm = MOE_ROWS
    sub = hp_ref.shape[1]
    half = sub * LANES
    per = tm // MOE_N_FF
    slot = i % 2
    active = i < na_ref[0]

    def gather(tok_ref, to_slot, r, shift=0):
        return pltpu.make_async_copy(hp_ref.at[tok_ref[0, 0, r] + shift], xg.at[to_slot, pl.ds(r * MOE_PITCH, sub), :], gsem.at[to_slot])

    def scatter(dst_ref, from_slot, r, shift=0):
        return pltpu.make_async_copy(stg.at[from_slot, pl.ds(r * MOE_PITCH, sub), :], y_ref.at[dst_ref[0, 0, r] + shift], usem.at[from_slot])

    def wait_rows(buf, sem, s):
        pltpu.make_async_copy(buf.at[1 - s, pl.ds(0, tm * sub), :], buf.at[s, pl.ds(0, tm * sub), :], sem.at[s]).wait()

    @pl.when((i == 0) & (f == 0))
    def _():
        stg[...] = jnp.zeros_like(stg)
        zrows = zeros.shape[0]
        zeros[...] = jnp.zeros_like(zeros)
        dump0 = y_ref.shape[0] - 2 * tm
        fills = [pltpu.make_async_copy(zeros, y_ref.at[pl.ds(dump0 + c * zrows, zrows)], zsem) for c in range(2 * tm // zrows)]
        for cp in fills:
            cp.start()
        for cp in fills:
            cp.wait()
        for r in range(tm):
            gather(tok_cur, 0, r).start(priority=1)

    @pl.when(f == 0)
    def _():
        wait_rows(xg, gsem, slot)
        for j in range(sub):
            lo, hi = _unpack_pairs(xg[slot, pl.ds(j, tm, stride=MOE_PITCH), :])
            xb[:, j * LANES:(j + 1) * LANES] = lo.astype(BF16)
            xb[:, half + j * LANES:half + (j + 1) * LANES] = hi.astype(BF16)
        acc[...] = jnp.broadcast_to(b2_ref[0, 0], acc.shape)

    def move_rows(kinds="gs", group=0, n_groups=1, shift=0):
        for rr in range(per * group // n_groups, per * (group + 1) // n_groups):
            r = f * per + rr
            if "g" in kinds:
                gather(tok_nxt, 1 - slot, r, shift).start(priority=1)
            if "s" in kinds:
                scatter(dst_prv, 1 - slot, r, shift).start(priority=1)

    def after(v):
        bits = pltpu.bitcast(v[0:1, 0:1], jnp.int32)[0, 0]
        return (bits & 1) >> 1

    tf = w2_ref.shape[2]

    def ffn_pass(h, kinds):
        rows = pl.ds(h * MOE_SUB_ROWS, MOE_SUB_ROWS)
        pieces = acc.shape[1] // MXU_COLS
        x = xb[rows, :]
        g = _dot(x, wb1[:, :tf]) + b1g_ref[0, 0]
        u = _dot(x, wb1[:, tf:]) + b1u_ref[0, 0]
        g = jnp.minimum(g, SWIGLU_LIMIT)
        u = jnp.clip(u, -SWIGLU_LIMIT, SWIGLU_LIMIT)
        a = ((u + 1.0) * g / (1.0 + jnp.exp(-SWIGLU_ALPHA * g))).astype(BF16)
        for c in range(pieces):
            cols = slice(c * MXU_COLS, (c + 1) * MXU_COLS)
            y = _dot(a, wb2[:, cols])
            acc[rows, cols] += y
            move_rows(kinds, c, pieces, after(y))

    def cast_weights():
        wb1[:, :tf] = w1g_ref[0, 0].astype(BF16)
        wb1[:, tf:] = w1u_ref[0, 0].astype(BF16)
        wb2[...] = w2_ref[0, 0].astype(BF16)

    two_passes = nv_ref[i] > MOE_SUB_ROWS
    assert MOE_ROWS == 2 * MOE_SUB_ROWS

    @pl.when(active & two_passes)
    def _():
        cast_weights()
        ffn_pass(0, "g")
        ffn_pass(1, "s")

    @pl.when(active & jnp.logical_not(two_passes))
    def _():
        cast_weights()
        ffn_pass(0, "gs")

    @pl.when(jnp.logical_not(active))
    def _():
        move_rows()

    @pl.when(f == nf - 1)
    def _():
        @pl.when(i >= 1)
        def _():
            wait_rows(stg, usem, slot)
        for j in range(sub):
            stg[slot, pl.ds(j, tm, stride=MOE_PITCH), :] = _pack_pairs(acc[:, j * LANES:(j + 1) * LANES],
                                                                      acc[:, half + j * LANES:half + (j + 1) * LANES])

    @pl.when((i == nb - 1) & (f == nf - 1))
    def _():
        wait_rows(xg, gsem, 1 - slot)
        wait_rows(stg, usem, 1 - slot)


def _moe_experts(layer, hp_tiles, slot_src, block_e, n_active, n_valid, w1, b1, w2, b2):
    n_tok, sub, _ = hp_tiles.shape
    d = 2 * sub * LANES
    tm, tf = MOE_ROWS, MOE_FF_TILE
    nf = MOE_N_FF
    n_real = slot_src.shape[0] // tm
    nb = n_real + 2
    n_rows = TOP_K * n_tok + 2 * tm
    ext = jnp.full(((nb + 2) * tm,), -1, jnp.int32).at[tm:(n_real + 1) * tm].set(slot_src)
    slot = jnp.arange(ext.shape[0], dtype=jnp.int32)
    tok_ext = (jnp.maximum(ext, 0) >> 2).reshape(nb + 2, 1, tm)
    dump = TOP_K * n_tok + ((slot // tm) % 2) * tm + slot % tm
    dst_ext = jnp.where(ext < 0, dump, (ext & (TOP_K - 1)) * n_tok + (ext >> 2)).reshape(nb + 2, 1, tm)

    def blk(i, na):
        return jnp.minimum(i, na[0] - 1)

    def ff(i, f, na):
        return jnp.where(i < na[0], f, nf - 1)

    smem = lambda off: pl.BlockSpec((1, 1, tm), lambda i, f, be, na, nv: (i + off, 0, 0), memory_space=pltpu.SMEM)
    grid_spec = pltpu.PrefetchScalarGridSpec(
        num_scalar_prefetch=3,
        grid=(nb, nf),
        in_specs=[
            smem(0), smem(1), smem(2),
            pl.BlockSpec(memory_space=pl.ANY),
            pl.BlockSpec((1, 1, d, tf), lambda i, f, be, na, nv: (layer, be[blk(i, na)], 0, ff(i, f, na))),
            pl.BlockSpec((1, 1, d, tf), lambda i, f, be, na, nv: (layer, be[blk(i, na)], 0, nf + ff(i, f, na))),
            pl.BlockSpec((1, 1, 1, tf), lambda i, f, be, na, nv: (layer, be[blk(i, na)], 0, ff(i, f, na))),
            pl.BlockSpec((1, 1, 1, tf), lambda i, f, be, na, nv: (layer, be[blk(i, na)], 0, nf + ff(i, f, na))),
            pl.BlockSpec((1, 1, tf, d), lambda i, f, be, na, nv: (layer, be[blk(i, na)], ff(i, f, na), 0)),
            pl.BlockSpec((1, 1, 1, d), lambda i, f, be, na, nv: (layer, be[blk(i, na)], 0, 0)),
        ],
        out_specs=pl.BlockSpec(memory_space=pl.ANY),
        scratch_shapes=[pltpu.VMEM((2, tm * MOE_PITCH, LANES), jnp.uint32), pltpu.VMEM((2, tm * MOE_PITCH, LANES), jnp.uint32),
                        pltpu.VMEM((tm, d), BF16), pltpu.VMEM((tm, d), F32),
                        pltpu.VMEM((d, 2 * tf), BF16), pltpu.VMEM((tf, d), BF16), pltpu.VMEM((64, sub, LANES), jnp.uint32),
                        pltpu.SemaphoreType.DMA((2,)), pltpu.SemaphoreType.DMA((2,)), pltpu.SemaphoreType.DMA(())],
    )
    depth, ne = w1.shape[:2]
    be_ext = jnp.concatenate([block_e, jnp.full((2,), N_EXPERTS - 1, jnp.int32)])
    nv_ext = jnp.concatenate([n_valid, jnp.zeros((2,), jnp.int32)])
    return pl.pallas_call(
        _ffn_kernel,
        grid_spec=grid_spec,
        out_shape=jax.ShapeDtypeStruct((n_rows, sub, LANES), jnp.uint32),
        compiler_params=_cparams(("arbitrary", "arbitrary")),
        name="moe_experts",
    )(be_ext, n_active, nv_ext, dst_ext, tok_ext, tok_ext, hp_tiles, w1, w1, b1.reshape(depth, ne, 1, -1), b1.reshape(depth, ne, 1, -1), w2,
      b2.reshape(depth, ne, 1, -1))


def _combine_kernel(*refs):
    y_refs, (gate_ref, x_ref, gm_ref, o_ref) = refs[:TOP_K], refs[TOP_K:]
    half = y_refs[0].shape[1]
    gates = gate_ref[0]
    lo_acc = jnp.zeros((x_ref.shape[1], half), F32)
    hi_acc = lo_acc
    for k in range(TOP_K):
        lo, hi = _unpack_pairs(y_refs[k][...])
        lo_acc = lo_acc + gates[:, k:k + 1] * lo
        hi_acc = hi_acc + gates[:, k:k + 1] * hi
    o_ref[0, :, :half] = x_ref[0, :, :half] + gm_ref[0, :, :half] * lo_acc
    o_ref[0, :, half:] = x_ref[0, :, half:] + gm_ref[0, :, half:] * hi_acc


def _moe_combine(y_tok, gates, x, gmod, tc=256):
    b, s, d = x.shape
    nt = s // tc
    plane = lambda k: pl.BlockSpec((tc, d // 2), lambda bi, i: (k * b * nt + bi * nt + i, 0))
    return pl.pallas_call(
        _combine_kernel,
        grid=(b, nt),
        in_specs=[plane(k) for k in range(TOP_K)] + [
            pl.BlockSpec((1, tc, LANES), lambda bi, i: (bi, i, 0)),
            pl.BlockSpec((1, tc, d), lambda bi, i: (bi, i, 0)),
            pl.BlockSpec((1, 1, d), lambda bi, i: (bi, 0, 0)),
        ],
        out_specs=pl.BlockSpec((1, tc, d), lambda bi, i: (bi, i, 0)),
        out_shape=jax.ShapeDtypeStruct((b, s, d), F32),
        compiler_params=_cparams(("parallel", "parallel")),
        name="moe_combine",
    )(*([y_tok] * TOP_K), gates, x, gmod.reshape(b, 1, d))


def _moe_ffn(layer, x, g, shift, scale, gmod, w_r, b_r, w1, b1, w2, b2):
    b, s, d = x.shape
    n = b * s
    hp, idx, gates, rank, counts = _moe_route(x, g, shift, scale, w_r, b_r)
    counts = counts[0, :N_EXPERTS].astype(jnp.int32)
    padded = (counts + MOE_ROWS - 1) // MOE_ROWS * MOE_ROWS
    pad_end = jnp.cumsum(padded)
    pad_start = pad_end - padded
    idx4 = idx.reshape(n, LANES)[:, :TOP_K]
    onehot = idx4[:, :, None] == jnp.arange(N_EXPERTS, dtype=jnp.int32)
    pos = jnp.sum(jnp.where(onehot, pad_start, 0), axis=-1) + rank.reshape(n, LANES)[:, :TOP_K]
    n_blocks = n * TOP_K // MOE_ROWS + N_EXPERTS
    slot_ids = jnp.arange(n * TOP_K, dtype=jnp.int32)
    slot_src = jnp.full((n_blocks * MOE_ROWS,), -1, jnp.int32).at[pos.reshape(-1)].set(slot_ids)
    block_start = jnp.arange(n_blocks, dtype=jnp.int32) * MOE_ROWS
    block_e = jnp.minimum(jnp.sum(pad_end[None, :] <= block_start[:, None], axis=1), N_EXPERTS - 1).astype(jnp.int32)
    n_active = (pad_end[-1:] // MOE_ROWS).astype(jnp.int32)
    n_valid = jnp.clip((pad_start + counts)[block_e] - block_start, 0, MOE_ROWS).astype(jnp.int32)
    hp_tiles = hp.reshape(n, d // 2 // LANES, LANES)
    y_tok = _moe_experts(layer, hp_tiles, slot_src, block_e, n_active, n_valid, w1, b1, w2, b2)
    return _moe_combine(y_tok.reshape(y_tok.shape[0], d // 2), gates, x, gmod)


def kernel(x, c, ctx, c_ctx, ada_w, ada_b, norm_mix, norm_ffn, ev_w_in, ev_conv_w, ev_conv_b, ev_q_norm, ev_k_norm, ev_rpb, ev_a_log, ev_dt_bias, ev_d_skip, ev_gate_norm, ev_w_out, od_w_in, od_b_in, od_conv_w, od_conv_b, od_filt_w1, od_filt_b1, od_filt_w2, od_filt_b2, od_filt_w3, od_filt_b3, od_filt_w4, od_filt_freq, od_filt_bias, od_w_out, od_b_out, moe_router_w, moe_router_b, moe_w1, moe_b1, moe_w2, moe_b2):
    b, s, d = x.shape
    assert ada_w.shape[0] == 2 and ev_w_in.shape[0] == 1 and od_w_in.shape[0] == 1 and d == D_MODEL
    cc = jnp.zeros((16, d), F32).at[:b].set(c).at[b].set(c_ctx)
    mod = _ada_mod(cc, ada_w, ada_b)
    chunk = lambda m, k: m[:, k * d:(k + 1) * d]
    moe = (moe_w1, moe_b1, moe_w2, moe_b2)

    m0 = mod[0, :b]
    mc0 = jnp.broadcast_to(mod[0, b:b + 1], (b, N_MOD * d))
    w_in = ev_w_in[0]
    n_main = 2 * NA_WIDTH + SSM_D_INNER + NA_WIDTH + SSM_XBC
    kv0 = NA_WIDTH + SSM_D_INNER
    w_main = w_in[:, :n_main].astype(BF16)
    w_ctx = w_in[:, kv0:n_main].astype(BF16)
    w_dt = jnp.zeros((d, LANES), BF16).at[:, :2 * SSM_HEADS].set(w_in[:, n_main:].astype(BF16))
    zb = lambda n: jnp.zeros((n,), F32)
    g0 = norm_mix[0]
    proj = _norm_mod_matmul(x, g0, chunk(m0, 0), chunk(m0, 1), w_main, zb(n_main), BF16, 1024, 1024)
    dt_raw = _norm_mod_matmul(x, g0, chunk(m0, 0), chunk(m0, 1), w_dt, zb(LANES), F32, 1024, LANES)[..., :2 * SSM_HEADS]
    proj_c = _norm_mod_matmul(ctx, g0, chunk(mc0, 0), chunk(mc0, 1), w_ctx, zb(n_main - kv0), BF16, 256, 1024)
    dt_raw_c = _norm_mod_matmul(ctx, g0, chunk(mc0, 0), chunk(mc0, 1), w_dt, zb(LANES), F32, 256, LANES)[..., :2 * SSM_HEADS]
    attn = _attention(proj, proj_c, ev_q_norm[0], ev_k_norm[0], _attn_bias_table(ev_rpb[0], s // GRID_W))
    ssm_args = (ev_conv_w[0], ev_conv_b[0], ev_dt_bias[0], ev_a_log[0])
    x_blk_c = 2 * NA_WIDTH // GROUP_WIDTH
    states = _ssd(proj_c, x_blk_c, None, dt_raw_c, *ssm_args, None, None, None)
    x_blk = (kv0 + 2 * NA_WIDTH) // GROUP_WIDTH
    z_blk = NA_WIDTH // GROUP_WIDTH
    y_ssm = _ssd(proj, x_blk, z_blk, dt_raw, *ssm_args, ev_d_skip[0], ev_gate_norm[0], states)
    x = _proj_residual([attn, y_ssm], ev_w_out[0].astype(BF16), zb(d), chunk(m0, 2), x)
    x = _moe_ffn(0, x, norm_ffn[0], chunk(m0, 3), chunk(m0, 4), chunk(m0, 5), moe_router_w[0], moe_router_b[0], *moe)

    m1 = mod[1, :b]
    proj_h = _norm_mod_matmul(x, norm_mix[1], chunk(m1, 0), chunk(m1, 1), od_w_in[0].astype(BF16), od_b_in[0], BF16, 1024, 1024)
    filt = _hyena_filters(s, od_filt_w1[0], od_filt_b1[0], od_filt_w2[0], od_filt_b2[0], od_filt_w3[0], od_filt_b3[0],
                          od_filt_w4[0], od_filt_freq[0])
    fwd, inv = _dft_matrices(s)
    coef = _filter_spectrum(filt, fwd)
    x0, u = _hyena_conv(proj_h, od_conv_w[0], od_conv_b[0])
    spec = _dft_forward(u, fwd, coef)
    y_h = _dft_inverse(spec.reshape(b, 2 * s, d), inv, x0, u, od_filt_bias[0])
    x = _proj_residual([y_h], od_w_out[0].astype(BF16), od_b_out[0], chunk(m1, 2), x)
    x = _moe_ffn(1, x, norm_ffn[1], chunk(m1, 3), chunk(m1, 4), chunk(m1, 5), moe_router_w[1], moe_router_b[1], *moe)
    return x
```

```python
import functools
import math

import jax
import jax.numpy as jnp
import numpy as np
from jax import lax
from jax.experimental import pallas as pl
from jax.experimental.pallas import tpu as pltpu

F32 = jnp.float32
BF16 = jnp.bfloat16
HI = lax.Precision.HIGHEST

D_MODEL = 2048
N_MOD = 6
EPS = 1e-6
NEG_INF = -1e9
GRID_W = 64
NA_HEADS = 16
NA_HEAD_DIM = 128
NA_WIDTH = NA_HEADS * NA_HEAD_DIM
NA_KH = 8
NA_KW = 16
SSM_D_INNER = 2048
SSM_HEAD_DIM = 64
SSM_HEADS = SSM_D_INNER // SSM_HEAD_DIM
SSM_GROUPS = 8
SSM_STATE = 128
SSM_CHUNK = 128
SSM_XBC = SSM_D_INNER + 2 * SSM_GROUPS * SSM_STATE
HEADS_PER_GROUP = SSM_HEADS // SSM_GROUPS
GROUP_WIDTH = HEADS_PER_GROUP * SSM_HEAD_DIM
HY_EMB = 33
HY_FILTER_WIDTH = 64
HY_TARGET = 1e-2
HY_FAST_PCT = 0.3
HY_SLOW_PCT = 1.5
N_EXPERTS = 32
TOP_K = 4
MOE_FF = 2048
SWIGLU_LIMIT = 7.0
SWIGLU_ALPHA = 1.702
MOE_ROWS = 512
MOE_FF_TILE = 512
MOE_N_FF = MOE_FF // MOE_FF_TILE
LANES = 128
MXU_COLS = 256
VMEM_LIMIT = 56 * 1024 * 1024


def _cparams(sem, vmem=VMEM_LIMIT):
    return pltpu.CompilerParams(dimension_semantics=sem, vmem_limit_bytes=vmem)


def _dot(a, b):
    return jnp.dot(a, b, preferred_element_type=F32)


def _dot_nt(a, b):
    return lax.dot_general(a, b, (((1,), (1,)), ((), ())), preferred_element_type=F32)


def _dot_tn(a, b):
    return lax.dot_general(a, b, (((0,), (0,)), ((), ())), preferred_element_type=F32)


def _silu(x):
    return x / (1.0 + jnp.exp(-x))


def _softplus(x):
    return jnp.maximum(x, 0.0) + jnp.log1p(jnp.exp(-jnp.abs(x)))


def _ada_kernel(c_ref, w_ref, b_ref, o_ref):
    sc = _silu(c_ref[...])
    hi = sc.astype(BF16)
    lo = (sc - hi.astype(F32)).astype(BF16)
    w = w_ref[0].astype(BF16)
    o_ref[0] = _dot(hi, w) + _dot(lo, w) + b_ref[0]


def _ada_mod(cc, ada_w, ada_b):
    depth, d, n = ada_w.shape
    tn = 1024
    return pl.pallas_call(
        _ada_kernel,
        grid=(depth, n // tn),
        in_specs=[
            pl.BlockSpec((cc.shape[0], d), lambda i, j: (0, 0)),
            pl.BlockSpec((1, d, tn), lambda i, j: (i, 0, j)),
            pl.BlockSpec((1, 1, tn), lambda i, j: (i, 0, j)),
        ],
        out_specs=pl.BlockSpec((1, cc.shape[0], tn), lambda i, j: (i, 0, j)),
        out_shape=jax.ShapeDtypeStruct((depth, cc.shape[0], n), F32),
        compiler_params=_cparams(("parallel", "parallel")),
        name="ada_mod",
    )(cc, ada_w, ada_b.reshape(depth, 1, n))


def _nmm_kernel(x_ref, g_ref, sh_ref, sc_ref, w_ref, b_ref, o_ref, h_scr):
    @pl.when(pl.program_id(2) == 0)
    def _():
        x = x_ref[0]
        ms = jnp.mean(x * x, axis=-1, keepdims=True)
        xn = x * lax.rsqrt(ms + EPS) * g_ref[...]
        h_scr[...] = (xn * (1.0 + sc_ref[0]) + sh_ref[0]).astype(BF16)

    o_ref[0] = (_dot(h_scr[...], w_ref[...]) + b_ref[...]).astype(o_ref.dtype)


def _norm_mod_matmul(x, g, shift, scale, w, bias, out_dtype, tm, tn):
    b, s, d = x.shape
    n = w.shape[1]
    tm = min(tm, s)
    return pl.pallas_call(
        _nmm_kernel,
        grid=(b, s // tm, n // tn),
        in_specs=[
            pl.BlockSpec((1, tm, d), lambda bi, i, j: (bi, i, 0)),
            pl.BlockSpec((1, d), lambda bi, i, j: (0, 0)),
            pl.BlockSpec((1, 1, d), lambda bi, i, j: (bi, 0, 0)),
            pl.BlockSpec((1, 1, d), lambda bi, i, j: (bi, 0, 0)),
            pl.BlockSpec((d, tn), lambda bi, i, j: (0, j)),
            pl.BlockSpec((1, tn), lambda bi, i, j: (0, j)),
        ],
        out_specs=pl.BlockSpec((1, tm, tn), lambda bi, i, j: (bi, i, j)),
        out_shape=jax.ShapeDtypeStruct((b, s, n), out_dtype),
        scratch_shapes=[pltpu.VMEM((tm, d), BF16)],
        compiler_params=_cparams(("parallel", "parallel", "arbitrary")),
        name="norm_mod_matmul",
    )(x, g.reshape(1, d), shift.reshape(b, 1, d), scale.reshape(b, 1, d), w, bias.reshape(1, n))


NA_PAIR_ROWS = NA_KH + 2
NA_VARIANTS = 5


def _pair_window_start(i, rows):
    return np.clip(2 * i - NA_KH // 2, 0, rows - NA_PAIR_ROWS)


def _attn_bias_table(rpb, rows):
    nh = rpb.shape[0]
    kc = np.arange(GRID_W)[:, None]
    qc = np.arange(GRID_W)[None, :]
    col_off = np.clip(kc - qc, -(NA_KW - 1), NA_KW - 1) + NA_KW - 1
    onehot = (col_off[None] == np.arange(2 * NA_KW - 1)[:, None, None]).astype(np.float32)
    toep = jnp.einsum('hrc,ckq->hrkq', rpb.astype(F32), onehot, precision=HI)
    ws_col = np.clip(qc - NA_KW // 2, 0, GRID_W - NA_KW)
    toep = jnp.where(((kc >= ws_col) & (kc < ws_col + NA_KW))[None, None], toep, NEG_INF)
    dead = jnp.full((nh, GRID_W, GRID_W), NEG_INF, F32)
    rep_pair = [0, 1, 2, rows // 2 - 2, rows // 2 - 1]
    blocks = []
    for i in rep_pair:
        ws = int(_pair_window_start(i, rows))
        for t in range(NA_PAIR_ROWS):
            for e in range(2):
                r = 2 * i + e
                rs = int(np.clip(r - NA_KH // 2, 0, rows - NA_KH))
                kr = ws + t
                blocks.append(toep[:, kr - r + NA_KH - 1] if rs <= kr < rs + NA_KH else dead)
    bias = jnp.stack(blocks, axis=1).reshape(nh, NA_VARIANTS, NA_PAIR_ROWS, 2, GRID_W, GRID_W)
    return bias.transpose(0, 1, 2, 4, 3, 5).reshape(nh, NA_VARIANTS, NA_PAIR_ROWS * GRID_W, 2 * GRID_W)


def _head_rmsnorm(t, g):
    sq = t * t
    hi = sq.astype(BF16)
    lo = (sq - hi.astype(F32)).astype(BF16)
    avg = jnp.full((NA_HEAD_DIM, NA_HEAD_DIM), 1.0 / NA_HEAD_DIM, BF16)
    ms = _dot(hi, avg) + _dot(lo, avg)
    return t * lax.rsqrt(ms + EPS) * g


def _attn_kernel(q_ref, k_ref, v_ref, kc_ref, vc_ref, qg_ref, kg_ref, bias_ref, o_ref, qs, ks, kcs, vt, vct):
    seq, hd = q_ref.shape[1], q_ref.shape[2]
    rows = seq // GRID_W
    pair = 2 * GRID_W
    win_blocks = NA_PAIR_ROWS // 2
    ctx_blocks = kc_ref.shape[1] // pair
    qs[...] = (_head_rmsnorm(q_ref[0].astype(F32), qg_ref[...]) * hd ** -0.5).astype(BF16)
    ks[...] = _head_rmsnorm(k_ref[0].astype(F32), kg_ref[...]).astype(BF16)
    kcs[...] = _head_rmsnorm(kc_ref[0].astype(F32), kg_ref[...]).astype(BF16)
    for blk in range(seq // pair):
        vt[blk] = v_ref[0, blk * pair:(blk + 1) * pair, :].astype(F32).T.astype(BF16)
    for blk in range(ctx_blocks):
        vct[blk] = vc_ref[0, blk * pair:(blk + 1) * pair, :].astype(F32).T.astype(BF16)

    def body(i, carry):
        ws = jnp.clip(2 * i - NA_KH // 2, 0, rows - NA_PAIR_ROWS)
        var = (2 * i - ws) // 2
        wb = ws // 2
        q0 = pl.multiple_of(i * pair, pair)
        k0 = pl.multiple_of(ws * GRID_W, pair)
        q_p = qs[pl.ds(q0, pair), :]
        s_lat = _dot_nt(ks[pl.ds(k0, win_blocks * pair), :], q_p) + bias_ref[0, var]
        s_ctx = _dot_nt(kcs[...], q_p)
        m = jnp.maximum(jnp.max(s_lat, axis=0, keepdims=True), jnp.max(s_ctx, axis=0, keepdims=True))
        p_lat = jnp.exp(s_lat - m)
        p_ctx = jnp.exp(s_ctx - m)
        den = jnp.sum(p_lat, axis=0, keepdims=True) + jnp.sum(p_ctx, axis=0, keepdims=True)
        p_lat = p_lat.astype(BF16)
        p_ctx = p_ctx.astype(BF16)
        o_t = jnp.zeros((hd, pair), F32)
        for blk in range(win_blocks):
            o_t = o_t + _dot(vt[wb + blk], p_lat[blk * pair:(blk + 1) * pair, :])
        for blk in range(ctx_blocks):
            o_t = o_t + _dot(vct[blk], p_ctx[blk * pair:(blk + 1) * pair, :])
        o_ref[0, pl.ds(q0, pair), :] = (o_t / den).T.astype(o_ref.dtype)
        return carry

    lax.fori_loop(0, rows // 2, body, 0, unroll=2)


def _attention(proj, proj_c, q_norm, k_norm, bias_tab):
    b, s, _ = proj.shape
    ctx = proj_c.shape[1]
    hd = NA_HEAD_DIM
    nh = NA_HEADS
    return pl.pallas_call(
        _attn_kernel,
        grid=(b, nh),
        in_specs=[
            pl.BlockSpec((1, s, hd), lambda bi, h: (bi, 0, h)),
            pl.BlockSpec((1, s, hd), lambda bi, h: (bi, 0, 2 * nh + h)),
            pl.BlockSpec((1, s, hd), lambda bi, h: (bi, 0, 3 * nh + h)),
            pl.BlockSpec((1, ctx, hd), lambda bi, h: (bi, 0, h)),
            pl.BlockSpec((1, ctx, hd), lambda bi, h: (bi, 0, nh + h)),
            pl.BlockSpec((1, hd), lambda bi, h: (0, 0)),
            pl.BlockSpec((1, hd), lambda bi, h: (0, 0)),
            pl.BlockSpec((1,) + bias_tab.shape[1:], lambda bi, h: (h, 0, 0, 0)),
        ],
        out_specs=pl.BlockSpec((1, s, hd), lambda bi, h: (bi, 0, h)),
        out_shape=jax.ShapeDtypeStruct((b, s, nh * hd), BF16),
        scratch_shapes=[pltpu.VMEM((s, hd), BF16), pltpu.VMEM((s, hd), BF16), pltpu.VMEM((ctx, hd), BF16),
                        pltpu.VMEM((s // (2 * GRID_W), hd, 2 * GRID_W), BF16), pltpu.VMEM((ctx // (2 * GRID_W), hd, 2 * GRID_W), BF16)],
        compiler_params=_cparams(("parallel", "parallel")),
        name="nbr_attention",
    )(proj, proj, proj, proj_c, proj_c, q_norm.reshape(1, hd), k_norm.reshape(1, hd), bias_tab)


def _split3(v):
    hi = v.astype(BF16)
    r1 = v - hi.astype(F32)
    mid = r1.astype(BF16)
    lo = (r1 - mid.astype(F32)).astype(BF16)
    return hi, mid, lo


def _ssd_kernel(*refs, with_y, nc):
    (xr_ref, br_ref, cr_ref, cwx, cbx, cwb, cbb, cwc, cbc, dt_ref, bias_ref, a_ref), rest = refs[:12], refs[12:]
    if with_y:
        (z_ref, dsk_ref, gn_ref, sf0_ref, sb0_ref, y_ref, xs, bs, cs, contrib, dec, wrow, xf, sfa, sba, drow, dtrow, dsplit, esplit) = rest
    else:
        sf_out, sb_out, xs, bs, cs, contrib, dec, wrow = rest
    ck = SSM_CHUNK
    length = nc * ck
    nh = HEADS_PER_GROUP
    nhd = 2 * nh
    gw = GROUP_WIDTH
    dot_hi = functools.partial(jnp.dot, precision=HI, preferred_element_type=F32)

    halo = 16
    win = ck + 2 * halo
    conv_w = jnp.concatenate([cwx[...], cwb[...], cwc[...]], axis=1)
    conv_b = jnp.concatenate([cbx[...], cbb[...], cbc[...]], axis=1)
    wi = lax.broadcasted_iota(jnp.int32, (ck, win), 1) - lax.broadcasted_iota(jnp.int32, (ck, win), 0)

    def conv_silu_chunk(c):
        l0 = pl.multiple_of(c * ck, ck)
        s0 = pl.multiple_of(jnp.clip(l0 - halo, 0, length - win), halo)
        rel = wi + (s0 - l0)
        window = jnp.concatenate([r[0, pl.ds(s0, win), :] for r in (xr_ref, br_ref, cr_ref)], axis=1)
        cur = jnp.concatenate([r[0, pl.ds(l0, ck), :] for r in (xr_ref, br_ref, cr_ref)], axis=1).astype(F32)
        prev = _dot((rel == -1).astype(BF16), window)
        nxt = _dot((rel == 1).astype(BF16), window)
        y = _silu(conv_b + prev * conv_w[0:1, :] + cur * conv_w[1:2, :] + nxt * conv_w[2:3, :])
        xs[pl.ds(l0, ck), :] = y[:, :gw].astype(BF16)
        if with_y:
            xf[pl.ds(l0, ck), :] = y[:, :gw]
        bs[pl.ds(l0, ck), :] = y[:, gw:gw + SSM_STATE].astype(BF16)
        cs[pl.ds(l0, ck), :] = y[:, gw + SSM_STATE:].astype(BF16)
        return y[:, :gw], y[:, gw:gw + SSM_STATE].astype(BF16)

    ri = lax.broadcasted_iota(jnp.int32, (ck, ck), 0)
    ci = lax.broadcasted_iota(jnp.int32, (ck, ck), 1)
    lower = ci <= ri
    upper = ci >= ri
    eye = (ci == ri).astype(BF16)
    dt_row = _softplus(dt_ref[0, 0] + bias_ref[0])
    a_row = dt_row * a_ref[0]
    cum_f = dot_hi(a_row, upper.astype(F32))
    cum_b = dot_hi(a_row, lower.astype(F32))
    tot = dot_hi(a_row, jnp.ones((ck, ck), F32))
    d_row = jnp.where((ri % nhd) < nh, cum_f, cum_b)
    wrow[...] = jnp.exp(tot - d_row) * dt_row
    texp = jnp.exp(tot)
    row4 = lax.broadcasted_iota(jnp.int32, (ck, 2 * gw), 0)
    lane4 = lax.broadcasted_iota(jnp.int32, (ck, 2 * gw), 1) // SSM_HEAD_DIM
    tmask = jnp.where(row4 % nhd == lane4, jnp.concatenate([texp] * (2 * gw // ck), axis=1), 0.0)
    pick = (lax.broadcasted_iota(jnp.int32, (dec.shape[0], ck), 1) // nhd
            == lax.broadcasted_iota(jnp.int32, (dec.shape[0], ck), 0)).astype(F32)
    dec[...] = dot_hi(pick, tmask)
    if with_y:
        drow[...] = d_row
        dtrow[...] = dt_row
        for k, part in enumerate(_split3(d_row)):
            dsplit[k] = part.astype(F32)
        for k, part in enumerate(_split3(jnp.exp(d_row))[:2]):
            esplit[k] = part.astype(F32)

    def spread(src, r0, width):
        rows = jnp.concatenate([jnp.broadcast_to(src[pl.ds(r0 + hd, 1), :], (width, ck)) for hd in range(nhd)], axis=0)
        return _dot_nt(eye, rows.astype(BF16))

    def chunk_states(c, carry):
        xc, b_c = conv_silu_chunk(c)
        wx = spread(wrow, c * nhd, SSM_HEAD_DIM)
        xw = jnp.concatenate([(xc * wx[:, :gw]).astype(BF16), (xc * wx[:, gw:]).astype(BF16)], axis=1)
        contrib[c] = _dot_tn(b_c, xw)
        return carry

    lax.fori_loop(0, nc, chunk_states, 0, unroll=2)

    def fwd_chain(c, s):
        if with_y:
            sfa[c] = s
        return s * dec[pl.ds(c, 1), :gw] + contrib[c, :, :gw]

    def bwd_chain(i, s):
        c = nc - 1 - i
        if with_y:
            sba[c] = s
        return s * dec[pl.ds(c, 1), gw:] + contrib[c, :, gw:]

    if with_y:
        s_f0 = sf0_ref[0, 0]
        s_b0 = sb0_ref[0, 0]
    else:
        s_f0 = jnp.zeros((SSM_STATE, gw), F32)
        s_b0 = s_f0
    s_f = lax.fori_loop(0, nc, fwd_chain, s_f0)
    s_b = lax.fori_loop(0, nc, bwd_chain, s_b0)
    if not with_y:
        sf_out[0, 0] = s_f
        sb_out[0, 0] = s_b
        return

    lane_head = lax.broadcasted_iota(jnp.int32, (ck, gw), 1) // SSM_HEAD_DIM

    def out_step(c, carry):
        l0 = pl.multiple_of(c * ck, ck)
        r0 = c * nhd
        d_b = spread(dsplit.at[0], r0, ck) + spread(dsplit.at[1], r0, ck) + spread(dsplit.at[2], r0, ck)
        e_b = spread(esplit.at[0], r0, SSM_HEAD_DIM) + spread(esplit.at[1], r0, SSM_HEAD_DIM)
        b_c = bs[pl.ds(l0, ck), :]
        c_c = cs[pl.ds(l0, ck), :]
        x_c = xs[pl.ds(l0, ck), :]
        g = _dot_nt(c_c, b_c)
        acc = jnp.zeros((ck, gw), F32)
        for j in range(nh):
            jb = nh + j
            lf = jnp.exp(jnp.where(lower, d_b[:, j * ck:(j + 1) * ck] - drow[pl.ds(r0 + j, 1), :], -1e30)) * dtrow[pl.ds(r0 + j, 1), :]
            lb = jnp.exp(jnp.where(upper, d_b[:, jb * ck:(jb + 1) * ck] - drow[pl.ds(r0 + jb, 1), :], -1e30)) * dtrow[pl.ds(r0 + jb, 1), :]
            m = (g * (lf + lb)).astype(BF16)
            acc = acc + _dot(m, jnp.where(lane_head == j, x_c, jnp.zeros_like(x_c)))
        acc = acc + _dot(c_c, sfa[c].astype(BF16)) * e_b[:, :gw]
        acc = acc + _dot(c_c, sba[c].astype(BF16)) * e_b[:, gw:]
        y = acc + xf[pl.ds(l0, ck), :] * dsk_ref[...]
        y = y * _silu(z_ref[0, pl.ds(l0, ck), :].astype(F32))
        ms = jnp.mean(y * y, axis=-1, keepdims=True)
        y_ref[0, pl.ds(l0, ck), :] = (y * lax.rsqrt(ms + EPS) * gn_ref[...]).astype(y_ref.dtype)
        return carry

    lax.fori_loop(0, nc, out_step, 0, unroll=2)


def _ssd(proj, x_blk0, z_blk0, dt_raw, conv_w, conv_b, dt_bias, a_log, d_skip, gate_norm, init):
    b, length, _ = proj.shape
    nc = length // SSM_CHUNK
    ng, nh, gw, st = SSM_GROUPS, HEADS_PER_GROUP, GROUP_WIDTH, SSM_STATE
    with_y = init is not None
    b_blk0 = x_blk0 * (gw // st) + SSM_D_INNER // st
    c_blk0 = b_blk0 + ng
    ck = SSM_CHUNK
    nhd = 2 * nh
    assert nc * nhd <= ck
    dtg = dt_raw.reshape(b, nc, ck, 2, ng, nh).transpose(0, 4, 1, 3, 5, 2).reshape(b, ng, nc * nhd, ck)
    dtg = jnp.pad(dtg, ((0, 0), (0, 0), (0, ck - nc * nhd), (0, 0)))
    per_row = lambda t: jnp.tile(t.reshape(2, ng, nh).transpose(1, 0, 2).reshape(ng, nhd), (1, ck // nhd)).reshape(ng, ck, 1)
    bias_r = per_row(dt_bias)
    a_r = per_row(-jnp.exp(a_log.astype(F32)))
    cw_x, cw_b, cw_c = conv_w[:, :SSM_D_INNER], conv_w[:, SSM_D_INNER:SSM_D_INNER + ng * st], conv_w[:, SSM_D_INNER + ng * st:]
    cb = conv_b.reshape(1, -1)
    cb_x, cb_b, cb_c = cb[:, :SSM_D_INNER], cb[:, SSM_D_INNER:SSM_D_INNER + ng * st], cb[:, SSM_D_INNER + ng * st:]
    in_specs = [
        pl.BlockSpec((1, length, gw), lambda bi, g: (bi, 0, x_blk0 + g)),
        pl.BlockSpec((1, length, st), lambda bi, g: (bi, 0, b_blk0 + g)),
        pl.BlockSpec((1, length, st), lambda bi, g: (bi, 0, c_blk0 + g)),
        pl.BlockSpec((3, gw), lambda bi, g: (0, g)),
        pl.BlockSpec((1, gw), lambda bi, g: (0, g)),
        pl.BlockSpec((3, st), lambda bi, g: (0, g)),
        pl.BlockSpec((1, st), lambda bi, g: (0, g)),
        pl.BlockSpec((3, st), lambda bi, g: (0, g)),
        pl.BlockSpec((1, st), lambda bi, g: (0, g)),
        pl.BlockSpec((1, 1, ck, ck), lambda bi, g: (bi, g, 0, 0)),
        pl.BlockSpec((1, ck, 1), lambda bi, g: (g, 0, 0)),
        pl.BlockSpec((1, ck, 1), lambda bi, g: (g, 0, 0)),
    ]
    args = [proj, proj, proj, cw_x, cb_x, cw_b, cb_b, cw_c, cb_c, dtg, bias_r, a_r]
    scratch = [pltpu.VMEM((length, gw), BF16), pltpu.VMEM((length, st), BF16), pltpu.VMEM((length, st), BF16),
               pltpu.VMEM((nc, st, 2 * gw), F32), pltpu.VMEM((ck // nhd, 2 * gw), F32), pltpu.VMEM((ck, ck), F32)]
    state_spec = pl.BlockSpec((1, 1, st, gw), lambda bi, g: (bi, g, 0, 0))
    state_shape = jax.ShapeDtypeStruct((b, ng, st, gw), F32)
    if with_y:
        in_specs += [
            pl.BlockSpec((1, length, gw), lambda bi, g: (bi, 0, z_blk0 + g)),
            pl.BlockSpec((1, gw), lambda bi, g: (0, g)),
            pl.BlockSpec((1, gw), lambda bi, g: (0, g)),
            state_spec, state_spec,
        ]
        args += [proj, jnp.repeat(d_skip, SSM_HEAD_DIM).reshape(1, -1), gate_norm.reshape(1, -1), init[0], init[1]]
        out_specs = pl.BlockSpec((1, length, gw), lambda bi, g: (bi, 0, g))
        out_shape = jax.ShapeDtypeStruct((b, length, SSM_D_INNER), BF16)
        scratch += [pltpu.VMEM((length, gw), F32), pltpu.VMEM((nc, st, gw), F32), pltpu.VMEM((nc, st, gw), F32),
                    pltpu.VMEM((ck, ck), F32), pltpu.VMEM((ck, ck), F32), pltpu.VMEM((3, ck, ck), F32), pltpu.VMEM((2, ck, ck), F32)]
    else:
        out_specs = [state_spec, state_spec]
        out_shape = [state_shape, state_shape]
    return pl.pallas_call(
        functools.partial(_ssd_kernel, with_y=with_y, nc=nc),
        grid=(b, ng),
        in_specs=in_specs,
        out_specs=out_specs,
        out_shape=out_shape,
        scratch_shapes=scratch,
        compiler_params=_cparams(("parallel", "parallel")),
        name="ssd_main" if with_y else "ssd_ctx",
    )(*args)


def _proj_res_kernel(*refs, n_lhs):
    a_refs, w_refs = refs[:n_lhs], refs[n_lhs:2 * n_lhs]
    b_ref, gate_ref, res_ref, o_ref = refs[2 * n_lhs:]
    acc = _dot(a_refs[0][0], w_refs[0][...])
    for a_ref, w_ref in zip(a_refs[1:], w_refs[1:]):
        acc = acc + _dot(a_ref[0], w_ref[...])
    o_ref[0] = res_ref[0] + gate_ref[0] * (acc + b_ref[...])


def _proj_residual(lhs, w, bias, gate, resid, tm=1024, tn=1024):
    b, s, k = lhs[0].shape
    n = w.shape[1]
    n_lhs = len(lhs)
    in_specs = [pl.BlockSpec((1, tm, k), lambda bi, i, j: (bi, i, 0)) for _ in lhs]
    in_specs += [pl.BlockSpec((k, tn), functools.partial(lambda bi, i, j, t: (t, j), t=t)) for t in range(n_lhs)]
    in_specs += [
        pl.BlockSpec((1, tn), lambda bi, i, j: (0, j)),
        pl.BlockSpec((1, 1, tn), lambda bi, i, j: (bi, 0, j)),
        pl.BlockSpec((1, tm, tn), lambda bi, i, j: (bi, i, j)),
    ]
    return pl.pallas_call(
        functools.partial(_proj_res_kernel, n_lhs=n_lhs),
        grid=(b, s // tm, n // tn),
        in_specs=in_specs,
        out_specs=pl.BlockSpec((1, tm, tn), lambda bi, i, j: (bi, i, j)),
        out_shape=jax.ShapeDtypeStruct((b, s, n), F32),
        compiler_params=_cparams(("parallel", "parallel", "parallel")),
        name="proj_residual",
    )(*lhs, *([w] * n_lhs), bias.reshape(1, n), gate.reshape(b, 1, n), resid)


def _hy_filter_kernel(tw_ref, f_ref, w1t, w1c, w1s, b1, w2, b2, w3, b3, fr, w4_ref, dl_ref, o_ref, h_scr):
    dot_hi = lambda a, b: jnp.dot(a, b, precision=HI, preferred_element_type=F32)

    @pl.when(pl.program_id(0) == 0)
    def _():
        t = tw_ref[:, 0:1]
        ang = tw_ref[:, 1:2] * f_ref[...]
        pre = t * w1t[...] + dot_hi(jnp.cos(ang), w1c[...]) + dot_hi(-jnp.sin(ang), w1s[...]) + b1[...]
        h = jnp.sin(fr[...] * pre)
        h = jnp.sin(fr[...] * (dot_hi(h, w2[...]) + b2[...]))
        h_scr[...] = jnp.sin(fr[...] * (dot_hi(h, w3[...]) + b3[...]))

    o_ref[...] = dot_hi(h_scr[...], w4_ref[...]) * jnp.exp(-tw_ref[:, 0:1] * dl_ref[...])


def _hyena_filters(length, w1, b1, w2, b2, w3, b3, w4, freq):
    bands = (HY_EMB - 1) // 2
    fw = HY_FILTER_WIDTH
    t = jnp.linspace(0.0, 1.0, length, dtype=F32)
    w = 2 * math.pi * jnp.arange(length, dtype=F32) / length
    tw = jnp.stack([t, w], axis=1)
    f = jnp.linspace(1e-4, bands - 1, bands, dtype=F32).reshape(1, bands)
    min_decay = math.log(HY_TARGET) / HY_SLOW_PCT
    max_decay = math.log(HY_TARGET) / HY_FAST_PCT
    deltas = jnp.abs(jnp.linspace(min_decay, max_decay, D_MODEL, dtype=F32))
    dl = jnp.concatenate([deltas, deltas]).reshape(1, 2 * D_MODEL)
    tn = 1024
    small = lambda shape: pl.BlockSpec(shape, lambda j: (0, 0))
    row = lambda v: v.reshape(1, -1)
    return pl.pallas_call(
        _hy_filter_kernel,
        grid=(2 * D_MODEL // tn,),
        in_specs=[small((length, 2)), small((1, bands)), small((1, fw)), small((bands, fw)), small((bands, fw)), small((1, fw)),
                  small((fw, fw)), small((1, fw)), small((fw, fw)), small((1, fw)), small((1, fw)),
                  pl.BlockSpec((fw, tn), lambda j: (0, j)), pl.BlockSpec((1, tn), lambda j: (0, j))],
        out_specs=pl.BlockSpec((length, tn), lambda j: (0, j)),
        out_shape=jax.ShapeDtypeStruct((length, 2 * D_MODEL), F32),
        scratch_shapes=[pltpu.VMEM((length, fw), F32)],
        compiler_params=_cparams(("arbitrary",)),
        name="hyena_filters",
    )(tw, f, w1[0:1], w1[1:1 + bands], w1[1 + bands:], row(b1), w2, row(b2), w3, row(b3), row(freq), w4, dl)


def _hy_conv_kernel(x0_ref, x1_ref, v_ref, w0, b0, w1, b1, w2, b2, x0_out, u_out):
    length = x0_ref.shape[1]

    def conv(raw_ref, w_ref, b_ref):
        x = raw_ref[0].astype(F32)
        rows = lax.broadcasted_iota(jnp.int32, x.shape, 0)
        xm1 = jnp.where(rows == 0, 0.0, pltpu.roll(x, 1, 0))
        xp1 = jnp.where(rows == length - 1, 0.0, pltpu.roll(x, length - 1, 0))
        return b_ref[...] + xm1 * w_ref[0:1, :] + x * w_ref[1:2, :] + xp1 * w_ref[2:3, :]

    x0_out[0] = conv(x0_ref, w0, b0).astype(x0_out.dtype)
    u_out[0] = (conv(v_ref, w2, b2) * conv(x1_ref, w1, b1)).astype(u_out.dtype)


def _hyena_conv(proj, conv_w, conv_b):
    b, length, _ = proj.shape
    d = D_MODEL
    tn = 512
    nb = d // tn
    cb = conv_b.reshape(1, -1)
    seg = lambda k: pl.BlockSpec((1, length, tn), lambda bi, j: (bi, 0, k * nb + j))
    wseg = lambda k: pl.BlockSpec((3, tn), lambda bi, j: (0, k * nb + j))
    bseg = lambda k: pl.BlockSpec((1, tn), lambda bi, j: (0, k * nb + j))
    out = pl.BlockSpec((1, length, tn), lambda bi, j: (bi, 0, j))
    return pl.pallas_call(
        _hy_conv_kernel,
        grid=(b, nb),
        in_specs=[seg(0), seg(1), seg(2), wseg(0), bseg(0), wseg(1), bseg(1), wseg(2), bseg(2)],
        out_specs=[out, out],
        out_shape=[jax.ShapeDtypeStruct((b, length, d), BF16)] * 2,
        compiler_params=_cparams(("parallel", "parallel")),
        name="hyena_conv",
    )(proj, proj, proj, conv_w, cb, conv_w, cb, conv_w, cb)


def _dft_matrices(length):
    n = 2 * length
    f = lax.broadcasted_iota(jnp.int32, (length, length), 0)
    t = lax.broadcasted_iota(jnp.int32, (length, length), 1)
    ang = ((f * t) % n).astype(F32) * (2 * math.pi / n)
    sign = (1 - 2 * (t % 2)).astype(F32)
    fwd_c = jnp.cos(ang)
    fwd_s = jnp.where(f == 0, sign, -jnp.sin(ang))
    fwd = jnp.concatenate([fwd_c, fwd_s], axis=0).astype(BF16)
    wgt = jnp.where(f == 0, 1.0, 2.0) / n
    inv = jnp.concatenate([(fwd_c * wgt).T, (fwd_s * wgt).T], axis=1).astype(BF16)
    return fwd, inv


def _mm_kernel(a_ref, b_ref, o_ref):
    o_ref[...] = _dot(a_ref[...], b_ref[...]).astype(o_ref.dtype)


def _matmul(a, b, out_dtype=F32, tm=1024, tn=1024):
    m, k = a.shape
    n = b.shape[1]
    return pl.pallas_call(
        _mm_kernel,
        grid=(n // tn, m // tm),
        in_specs=[pl.BlockSpec((tm, k), lambda j, i: (i, 0)), pl.BlockSpec((k, tn), lambda j, i: (0, j))],
        out_specs=pl.BlockSpec((tm, tn), lambda j, i: (i, j)),
        out_shape=jax.ShapeDtypeStruct((m, n), out_dtype),
        compiler_params=_cparams(("parallel", "parallel")),
        name="matmul",
    )(a, b)


def _filter_spectrum(kk, fwd):
    length = kk.shape[0]
    d = D_MODEL
    k_f, k_b = kk[:, :d], kk[:, d:]
    k_fp = k_f.at[0].add(k_b[0])
    k_bp = k_b.at[0].set(0.0)
    spec = _matmul(fwd, jnp.concatenate([k_fp, k_bp], axis=1).astype(BF16))
    a_re, a_im, b_re, b_im = spec[:length, :d], spec[length:, :d], spec[:length, d:], spec[length:, d:]
    nz = (jnp.arange(length) != 0)[:, None]
    k_re = a_re + b_re
    k_im = jnp.where(nz, a_im - b_im, a_im + b_im)
    return jnp.stack([k_re, jnp.where(nz, k_im, 0.0), jnp.where(nz, k_re, k_im)])


def _dft_fwd_kernel(fc_ref, fs_ref, u_ref, k_ref, y_ref):
    u = u_ref[0]
    re = _dot(fc_ref[...], u)
    im = _dot(fs_ref[...], u)
    y_ref[0, 0] = (re * k_ref[0] - im * k_ref[1]).astype(y_ref.dtype)
    y_ref[0, 1] = (re * k_ref[1] + im * k_ref[2]).astype(y_ref.dtype)


def _dft_forward(u, fwd, coef, tm=1024, tn=512):
    b, length, d = u.shape
    ni = length // tm
    return pl.pallas_call(
        _dft_fwd_kernel,
        grid=(d // tn, ni, b),
        in_specs=[
            pl.BlockSpec((tm, length), lambda j, i, bi: (i, 0)),
            pl.BlockSpec((tm, length), lambda j, i, bi: (ni + i, 0)),
            pl.BlockSpec((1, length, tn), lambda j, i, bi: (bi, 0, j)),
            pl.BlockSpec((3, tm, tn), lambda j, i, bi: (0, i, j)),
        ],
        out_specs=pl.BlockSpec((1, 2, tm, tn), lambda j, i, bi: (bi, 0, i, j)),
        out_shape=jax.ShapeDtypeStruct((b, 2, length, d), BF16),
        compiler_params=_cparams(("parallel", "parallel", "parallel")),
        name="dft_forward",
    )(fwd, fwd, u, coef)


def _dft_inv_kernel(g_ref, y_ref, x0_ref, u_ref, fb_ref, o_ref):
    conv = _dot(g_ref[...], y_ref[0])
    o_ref[0] = (x0_ref[0].astype(F32) * (conv + u_ref[0].astype(F32) * fb_ref[...])).astype(o_ref.dtype)


def _dft_inverse(y, inv, x0, u, filt_bias, tm=512, tn=512):
    b, n2, d = y.shape
    length = n2 // 2
    return pl.pallas_call(
        _dft_inv_kernel,
        grid=(d // tn, length // tm, b),
        in_specs=[
            pl.BlockSpec((tm, n2), lambda j, i, bi: (i, 0)),
            pl.BlockSpec((1, n2, tn), lambda j, i, bi: (bi, 0, j)),
            pl.BlockSpec((1, tm, tn), lambda j, i, bi: (bi, i, j)),
            pl.BlockSpec((1, tm, tn), lambda j, i, bi: (bi, i, j)),
            pl.BlockSpec((1, tn), lambda j, i, bi: (0, j)),
        ],
        out_specs=pl.BlockSpec((1, tm, tn), lambda j, i, bi: (bi, i, j)),
        out_shape=jax.ShapeDtypeStruct((b, length, d), BF16),
        compiler_params=_cparams(("parallel", "parallel", "parallel")),
        name="dft_inverse",
    )(inv, y, x0, u, filt_bias.reshape(1, d))


PACK_HI = 0xFFFF0000
TILE_SUBLANES = 8


def _pack_pairs(lo, hi):
    lo_b = pltpu.bitcast(lo.astype(BF16).astype(F32), jnp.uint32)
    hi_b = pltpu.bitcast(hi.astype(BF16).astype(F32), jnp.uint32)
    return (lo_b >> 16) | (hi_b & jnp.uint32(PACK_HI))


def _unpack_pairs(w):
    return pltpu.bitcast(w << 16, F32), pltpu.bitcast(w & jnp.uint32(PACK_HI), F32)


def _router_kernel(x_ref, g_ref, sh_ref, sc_ref, wr_ref, br_ref, hp_ref, idx_ref, gate_ref, rank_ref, cnt_ref, run):
    tm = x_ref.shape[1]
    half = x_ref.shape[2] // 2
    first = (pl.program_id(0) == 0) & (pl.program_id(1) == 0)

    @pl.when(first)
    def _():
        run[...] = jnp.zeros_like(run)

    x = x_ref[0]
    ms = jnp.mean(x * x, axis=-1, keepdims=True)
    h = x * lax.rsqrt(ms + EPS) * g_ref[...] * (1.0 + sc_ref[0]) + sh_ref[0]
    hp_ref[0] = _pack_pairs(h[:, :half], h[:, half:])

    logits = jnp.dot(h, wr_ref[...], precision=HI, preferred_element_type=F32) + br_ref[...]
    lane_i = lax.broadcasted_iota(jnp.int32, logits.shape, 1)
    lane = lane_i.astype(F32)
    work = jnp.where(lane_i < N_EXPERTS, logits, -jnp.inf)
    li = lax.broadcasted_iota(jnp.int32, (tm, tm), 0)
    ki = lax.broadcasted_iota(jnp.int32, (tm, tm), 1)
    strict_lower = (ki < li).astype(BF16)
    sels, tops = [], []
    for _ in range(TOP_K):
        m = jnp.max(work, axis=-1, keepdims=True)
        first_idx = jnp.min(jnp.where(work == m, lane, float(LANES)), axis=-1, keepdims=True)
        sel = lane == first_idx
        sels.append(sel)
        tops.append((m, first_idx))
        work = jnp.where(sel, -jnp.inf, work)
    chosen = sels[0] | sels[1] | sels[2] | sels[3]
    before = _dot(strict_lower, chosen.astype(BF16)) + run[...]
    run[...] = run[...] + jnp.sum(chosen.astype(F32), axis=0, keepdims=True)
    den = sum(jnp.exp(m - tops[0][0]) for m, _ in tops)
    idx_o = jnp.zeros(logits.shape, jnp.int32)
    gate_o = jnp.zeros(logits.shape, F32)
    rank_o = jnp.zeros(logits.shape, jnp.int32)
    for k, (sel, (m, first_idx)) in enumerate(zip(sels, tops)):
        rank = jnp.sum(jnp.where(sel, before, 0.0), axis=-1, keepdims=True).astype(jnp.int32)
        idx_o = jnp.where(lane_i == k, first_idx.astype(jnp.int32), idx_o)
        gate_o = jnp.where(lane_i == k, jnp.exp(m - tops[0][0]) / den, gate_o)
        rank_o = jnp.where(lane_i == k, rank, rank_o)
    idx_ref[0] = idx_o
    gate_ref[0] = gate_o
    rank_ref[0] = rank_o
    cnt_ref[...] = run[...]


def _moe_route(x, g, shift, scale, w_r, b_r, tm=512):
    b, s, d = x.shape
    wr = jnp.zeros((d, LANES), F32).at[:, :N_EXPERTS].set(w_r)
    br = jnp.zeros((1, LANES), F32).at[0, :N_EXPERTS].set(b_r)
    tok = lambda n: pl.BlockSpec((1, tm, n), lambda bi, i: (bi, i, 0))
    return pl.pallas_call(
        _router_kernel,
        grid=(b, s // tm),
        in_specs=[
            tok(d),
            pl.BlockSpec((1, d), lambda bi, i: (0, 0)),
            pl.BlockSpec((1, 1, d), lambda bi, i: (bi, 0, 0)),
            pl.BlockSpec((1, 1, d), lambda bi, i: (bi, 0, 0)),
            pl.BlockSpec((d, LANES), lambda bi, i: (0, 0)),
            pl.BlockSpec((1, LANES), lambda bi, i: (0, 0)),
        ],
        out_specs=[tok(d // 2), tok(LANES), tok(LANES), tok(LANES), pl.BlockSpec((1, LANES), lambda bi, i: (0, 0))],
        out_shape=[
            jax.ShapeDtypeStruct((b, s, d // 2), jnp.uint32),
            jax.ShapeDtypeStruct((b, s, LANES), jnp.int32),
            jax.ShapeDtypeStruct((b, s, LANES), F32),
            jax.ShapeDtypeStruct((b, s, LANES), jnp.int32),
            jax.ShapeDtypeStruct((1, LANES), F32),
        ],
        scratch_shapes=[pltpu.VMEM((1, LANES), F32)],
        compiler_params=_cparams(("arbitrary", "arbitrary")),
        name="moe_route",
    )(x, g.reshape(1, d), shift.reshape(b, 1, d), scale.reshape(b, 1, d), wr, br)


MOE_PITCH = TILE_SUBLANES + 1


def _ffn_kernel(be_ref, na_ref, dst_prv, tok_cur, tok_nxt, hp_ref, w1g_ref, w1u_ref, b1g_ref, b1u_ref, w2_ref, b2_ref, y_ref,
                xg, stg, xb, acc, gu, zeros, gsem, usem, zsem):
    i = pl.program_id(0)
    f = pl.program_id(1)
    nb = pl.num_programs(0)
    nf = pl.num_programs(1)
    tm = MOE_ROWS
    sub = hp_ref.shape[1]
    half = sub * LANES
    per = tm // MOE_N_FF
    slot = i % 2
    active = i < na_ref[0]

    def gather(tok_ref, to_slot, r, shift=0):
        return pltpu.make_async_copy(hp_ref.at[tok_ref[0, 0, r] + shift], xg.at[to_slot, pl.ds(r * MOE_PITCH, sub), :], gsem.at[to_slot])

    def scatter(dst_ref, from_slot, r, shift=0):
        return pltpu.make_async_copy(stg.at[from_slot, pl.ds(r * MOE_PITCH, sub), :], y_ref.at[dst_ref[0, 0, r] + shift], usem.at[from_slot])

    def wait_rows(buf, sem, s):
        pltpu.make_async_copy(buf.at[1 - s, pl.ds(0, tm * sub), :], buf.at[s, pl.ds(0, tm * sub), :], sem.at[s]).wait()

    @pl.when((i == 0) & (f == 0))
    def _():
        stg[...] = jnp.zeros_like(stg)
        zrows = zeros.shape[0]
        zeros[...] = jnp.zeros_like(zeros)
        dump0 = y_ref.shape[0] - 2 * tm
        fills = [pltpu.make_async_copy(zeros, y_ref.at[pl.ds(dump0 + c * zrows, zrows)], zsem) for c in range(2 * tm // zrows)]
        for cp in fills:
            cp.start()
        for cp in fills:
            cp.wait()
        for r in range(tm):
            gather(tok_cur, 0, r).start()

    @pl.when(f == 0)
    def _():
        wait_rows(xg, gsem, slot)
        for j in range(sub):
            lo, hi = _unpack_pairs(xg[slot, pl.ds(j, tm, stride=MOE_PITCH), :])
            xb[:, j * LANES:(j + 1) * LANES] = lo.astype(BF16)
            xb[:, half + j * LANES:half + (j + 1) * LANES] = hi.astype(BF16)
        acc[...] = jnp.broadcast_to(b2_ref[0, 0], acc.shape)

    def move_rows(group=0, n_groups=1, shift=0):
        for rr in range(per * group // n_groups, per * (group + 1) // n_groups):
            r = f * per + rr
            gather(tok_nxt, 1 - slot, r, shift).start()
            scatter(dst_prv, 1 - slot, r, shift).start()

    def after(v):
        bits = pltpu.bitcast(v[0:1, 0:1], jnp.int32)[0, 0]
        return (bits & 1) >> 1

    @pl.when(active)
    def _():
        pieces = acc.shape[1] // MXU_COLS
        n_groups = pieces + 2
        x = xb[...]
        g = _dot(x, w1g_ref[0, 0].astype(BF16)) + b1g_ref[0, 0]
        move_rows(0, n_groups, after(g))
        gu[0] = g
        u = _dot(x, w1u_ref[0, 0].astype(BF16)) + b1u_ref[0, 0]
        move_rows(1, n_groups, after(u))
        gu[1] = u
        g = jnp.minimum(gu[0], SWIGLU_LIMIT)
        u = jnp.clip(gu[1], -SWIGLU_LIMIT, SWIGLU_LIMIT)
        a = ((u + 1.0) * g / (1.0 + jnp.exp(-SWIGLU_ALPHA * g))).astype(BF16)
        w2 = w2_ref[0, 0]
        for c in range(pieces):
            cols = slice(c * MXU_COLS, (c + 1) * MXU_COLS)
            y = _dot(a, w2[:, cols].astype(BF16))
            acc[:, cols] += y
            move_rows(c + 2, n_groups, after(y))

    @pl.when(jnp.logical_not(active))
    def _():
        move_rows()

    @pl.when(f == nf - 1)
    def _():
        @pl.when(i >= 1)
        def _():
            wait_rows(stg, usem, slot)
        for j in range(sub):
            stg[slot, pl.ds(j, tm, stride=MOE_PITCH), :] = _pack_pairs(acc[:, j * LANES:(j + 1) * LANES],
                                                                      acc[:, half + j * LANES:half + (j + 1) * LANES])

    @pl.when((i == nb - 1) & (f == nf - 1))
    def _():
        wait_rows(xg, gsem, 1 - slot)
        wait_rows(stg, usem, 1 - slot)


def _moe_experts(layer, hp_tiles, slot_src, block_e, n_active, w1, b1, w2, b2):
    n_tok, sub, _ = hp_tiles.shape
    d = 2 * sub * LANES
    tm, tf = MOE_ROWS, MOE_FF_TILE
    nf = MOE_N_FF
    n_real = slot_src.shape[0] // tm
    nb = n_real + 2
    n_rows = TOP_K * n_tok + 2 * tm
    ext = jnp.full(((nb + 2) * tm,), -1, jnp.int32).at[tm:(n_real + 1) * tm].set(slot_src)
    slot = jnp.arange(ext.shape[0], dtype=jnp.int32)
    tok_ext = (jnp.maximum(ext, 0) >> 2).reshape(nb + 2, 1, tm)
    dump = TOP_K * n_tok + ((slot // tm) % 2) * tm + slot % tm
    dst_ext = jnp.where(ext < 0, dump, (ext & (TOP_K - 1)) * n_tok + (ext >> 2)).reshape(nb + 2, 1, tm)

    def blk(i, na):
        return jnp.minimum(i, na[0] - 1)

    def ff(i, f, na):
        return jnp.where(i < na[0], f, nf - 1)

    smem = lambda off: pl.BlockSpec((1, 1, tm), lambda i, f, be, na: (i + off, 0, 0), memory_space=pltpu.SMEM)
    grid_spec = pltpu.PrefetchScalarGridSpec(
        num_scalar_prefetch=2,
        grid=(nb, nf),
        in_specs=[
            smem(0), smem(1), smem(2),
            pl.BlockSpec(memory_space=pl.ANY),
            pl.BlockSpec((1, 1, d, tf), lambda i, f, be, na: (layer, be[blk(i, na)], 0, ff(i, f, na))),
            pl.BlockSpec((1, 1, d, tf), lambda i, f, be, na: (layer, be[blk(i, na)], 0, nf + ff(i, f, na))),
            pl.BlockSpec((1, 1, 1, tf), lambda i, f, be, na: (layer, be[blk(i, na)], 0, ff(i, f, na))),
            pl.BlockSpec((1, 1, 1, tf), lambda i, f, be, na: (layer, be[blk(i, na)], 0, nf + ff(i, f, na))),
            pl.BlockSpec((1, 1, tf, d), lambda i, f, be, na: (layer, be[blk(i, na)], ff(i, f, na), 0)),
            pl.BlockSpec((1, 1, 1, d), lambda i, f, be, na: (layer, be[blk(i, na)], 0, 0)),
        ],
        out_specs=pl.BlockSpec(memory_space=pl.ANY),
        scratch_shapes=[pltpu.VMEM((2, tm * MOE_PITCH, LANES), jnp.uint32), pltpu.VMEM((2, tm * MOE_PITCH, LANES), jnp.uint32),
                        pltpu.VMEM((tm, d), BF16), pltpu.VMEM((tm, d), F32), pltpu.VMEM((2, tm, tf), F32),
                        pltpu.VMEM((64, sub, LANES), jnp.uint32),
                        pltpu.SemaphoreType.DMA((2,)), pltpu.SemaphoreType.DMA((2,)), pltpu.SemaphoreType.DMA(())],
    )
    depth, ne = w1.shape[:2]
    be_ext = jnp.concatenate([block_e, jnp.full((2,), N_EXPERTS - 1, jnp.int32)])
    return pl.pallas_call(
        _ffn_kernel,
        grid_spec=grid_spec,
        out_shape=jax.ShapeDtypeStruct((n_rows, sub, LANES), jnp.uint32),
        compiler_params=_cparams(("arbitrary", "arbitrary")),
        name="moe_experts",
    )(be_ext, n_active, dst_ext, tok_ext, tok_ext, hp_tiles, w1, w1, b1.reshape(depth, ne, 1, -1), b1.reshape(depth, ne, 1, -1), w2,
      b2.reshape(depth, ne, 1, -1))


def _combine_kernel(*refs):
    y_refs, (gate_ref, x_ref, gm_ref, o_ref) = refs[:TOP_K], refs[TOP_K:]
    half = y_refs[0].shape[1]
    gates = gate_ref[0]
    lo_acc = jnp.zeros((x_ref.shape[1], half), F32)
    hi_acc = lo_acc
    for k in range(TOP_K):
        lo, hi = _unpack_pairs(y_refs[k][...])
        lo_acc = lo_acc + gates[:, k:k + 1] * lo
        hi_acc = hi_acc + gates[:, k:k + 1] * hi
    o_ref[0, :, :half] = x_ref[0, :, :half] + gm_ref[0, :, :half] * lo_acc
    o_ref[0, :, half:] = x_ref[0, :, half:] + gm_ref[0, :, half:] * hi_acc


def _moe_combine(y_tok, gates, x, gmod, tc=256):
    b, s, d = x.shape
    nt = s // tc
    plane = lambda k: pl.BlockSpec((tc, d // 2), lambda bi, i: (k * b * nt + bi * nt + i, 0))
    return pl.pallas_call(
        _combine_kernel,
        grid=(b, nt),
        in_specs=[plane(k) for k in range(TOP_K)] + [
            pl.BlockSpec((1, tc, LANES), lambda bi, i: (bi, i, 0)),
            pl.BlockSpec((1, tc, d), lambda bi, i: (bi, i, 0)),
            pl.BlockSpec((1, 1, d), lambda bi, i: (bi, 0, 0)),
        ],
        out_specs=pl.BlockSpec((1, tc, d), lambda bi, i: (bi, i, 0)),
        out_shape=jax.ShapeDtypeStruct((b, s, d), F32),
        compiler_params=_cparams(("parallel", "parallel")),
        name="moe_combine",
    )(*([y_tok] * TOP_K), gates, x, gmod.reshape(b, 1, d))


def _moe_ffn(layer, x, g, shift, scale, gmod, w_r, b_r, w1, b1, w2, b2):
    b, s, d = x.shape
    n = b * s
    hp, idx, gates, rank, counts = _moe_route(x, g, shift, scale, w_r, b_r)
    counts = counts[0, :N_EXPERTS].astype(jnp.int32)
    padded = (counts + MOE_ROWS - 1) // MOE_ROWS * MOE_ROWS
    pad_end = jnp.cumsum(padded)
    pad_start = pad_end - padded
    idx4 = idx.reshape(n, LANES)[:, :TOP_K]
    onehot = idx4[:, :, None] == jnp.arange(N_EXPERTS, dtype=jnp.int32)
    pos = jnp.sum(jnp.where(onehot, pad_start, 0), axis=-1) + rank.reshape(n, LANES)[:, :TOP_K]
    n_blocks = n * TOP_K // MOE_ROWS + N_EXPERTS
    slot_ids = jnp.arange(n * TOP_K, dtype=jnp.int32)
    slot_src = jnp.full((n_blocks * MOE_ROWS,), -1, jnp.int32).at[pos.reshape(-1)].set(slot_ids)
    block_start = jnp.arange(n_blocks, dtype=jnp.int32) * MOE_ROWS
    block_e = jnp.minimum(jnp.sum(pad_end[None, :] <= block_start[:, None], axis=1), N_EXPERTS - 1).astype(jnp.int32)
    n_active = (pad_end[-1:] // MOE_ROWS).astype(jnp.int32)
    hp_tiles = hp.reshape(n, d // 2 // LANES, LANES)
    y_tok = _moe_experts(layer, hp_tiles, slot_src, block_e, n_active, w1, b1, w2, b2)
    return _moe_combine(y_tok.reshape(y_tok.shape[0], d // 2), gates, x, gmod)


def kernel(x, c, ctx, c_ctx, ada_w, ada_b, norm_mix, norm_ffn, ev_w_in, ev_conv_w, ev_conv_b, ev_q_norm, ev_k_norm, ev_rpb, ev_a_log, ev_dt_bias, ev_d_skip, ev_gate_norm, ev_w_out, od_w_in, od_b_in, od_conv_w, od_conv_b, od_filt_w1, od_filt_b1, od_filt_w2, od_filt_b2, od_filt_w3, od_filt_b3, od_filt_w4, od_filt_freq, od_filt_bias, od_w_out, od_b_out, moe_router_w, moe_router_b, moe_w1, moe_b1, moe_w2, moe_b2):
    b, s, d = x.shape
    assert ada_w.shape[0] == 2 and ev_w_in.shape[0] == 1 and od_w_in.shape[0] == 1 and d == D_MODEL
    cc = jnp.zeros((16, d), F32).at[:b].set(c).at[b].set(c_ctx)
    mod = _ada_mod(cc, ada_w, ada_b)
    chunk = lambda m, k: m[:, k * d:(k + 1) * d]
    moe = (moe_w1, moe_b1, moe_w2, moe_b2)

    m0 = mod[0, :b]
    mc0 = jnp.broadcast_to(mod[0, b:b + 1], (b, N_MOD * d))
    w_in = ev_w_in[0]
    n_main = 2 * NA_WIDTH + SSM_D_INNER + NA_WIDTH + SSM_XBC
    kv0 = NA_WIDTH + SSM_D_INNER
    w_main = w_in[:, :n_main].astype(BF16)
    w_ctx = w_in[:, kv0:n_main].astype(BF16)
    w_dt = jnp.zeros((d, LANES), BF16).at[:, :2 * SSM_HEADS].set(w_in[:, n_main:].astype(BF16))
    zb = lambda n: jnp.zeros((n,), F32)
    g0 = norm_mix[0]
    proj = _norm_mod_matmul(x, g0, chunk(m0, 0), chunk(m0, 1), w_main, zb(n_main), BF16, 1024, 1024)
    dt_raw = _norm_mod_matmul(x, g0, chunk(m0, 0), chunk(m0, 1), w_dt, zb(LANES), F32, 1024, LANES)[..., :2 * SSM_HEADS]
    proj_c = _norm_mod_matmul(ctx, g0, chunk(mc0, 0), chunk(mc0, 1), w_ctx, zb(n_main - kv0), BF16, 256, 1024)
    dt_raw_c = _norm_mod_matmul(ctx, g0, chunk(mc0, 0), chunk(mc0, 1), w_dt, zb(LANES), F32, 256, LANES)[..., :2 * SSM_HEADS]
    attn = _attention(proj, proj_c, ev_q_norm[0], ev_k_norm[0], _attn_bias_table(ev_rpb[0], s // GRID_W))
    ssm_args = (ev_conv_w[0], ev_conv_b[0], ev_dt_bias[0], ev_a_log[0])
    x_blk_c = 2 * NA_WIDTH // GROUP_WIDTH
    states = _ssd(proj_c, x_blk_c, None, dt_raw_c, *ssm_args, None, None, None)
    x_blk = (kv0 + 2 * NA_WIDTH) // GROUP_WIDTH
    z_blk = NA_WIDTH // GROUP_WIDTH
    y_ssm = _ssd(proj, x_blk, z_blk, dt_raw, *ssm_args, ev_d_skip[0], ev_gate_norm[0], states)
    x = _proj_residual([attn, y_ssm], ev_w_out[0].astype(BF16), zb(d), chunk(m0, 2), x)
    x = _moe_ffn(0, x, norm_ffn[0], chunk(m0, 3), chunk(m0, 4), chunk(m0, 5), moe_router_w[0], moe_router_b[0], *moe)

    m1 = mod[1, :b]
    proj_h = _norm_mod_matmul(x, norm_mix[1], chunk(m1, 0), chunk(m1, 1), od_w_in[0].astype(BF16), od_b_in[0], BF16, 1024, 1024)
    filt = _hyena_filters(s, od_filt_w1[0], od_filt_b1[0], od_filt_w2[0], od_filt_b2[0], od_filt_w3[0], od_filt_b3[0],
                          od_filt_w4[0], od_filt_freq[0])
    fwd, inv = _dft_matrices(s)
    coef = _filter_spectrum(filt, fwd)
    x0, u = _hyena_conv(proj_h, od_conv_w[0], od_conv_b[0])
    spec = _dft_forward(u, fwd, coef)
    y_h = _dft_inverse(spec.reshape(b, 2 * s, d), inv, x0, u, od_filt_bias[0])
    x = _proj_residual([y_h], od_w_out[0].astype(BF16), od_b_out[0], chunk(m1, 2), x)
    x = _moe_ffn(1, x, norm_ffn[1], chunk(m1, 3), chunk(m1, 4), chunk(m1, 5), moe_router_w[1], moe_router_b[1], *moe)
    return x
```

```python
import functools
import math

import jax
import jax.numpy as jnp
import numpy as np
from jax import lax
from jax.experimental import pallas as pl
from jax.experimental.pallas import tpu as pltpu

F32 = jnp.float32
BF16 = jnp.bfloat16
HI = lax.Precision.HIGHEST

D_MODEL = 2048
N_MOD = 6
EPS = 1e-6
NEG_INF = -1e9
GRID_W = 64
NA_HEADS = 16
NA_HEAD_DIM = 128
NA_WIDTH = NA_HEADS * NA_HEAD_DIM
NA_KH = 8
NA_KW = 16
SSM_D_INNER = 2048
SSM_HEAD_DIM = 64
SSM_HEADS = SSM_D_INNER // SSM_HEAD_DIM
SSM_GROUPS = 8
SSM_STATE = 128
SSM_CHUNK = 128
SSM_XBC = SSM_D_INNER + 2 * SSM_GROUPS * SSM_STATE
HEADS_PER_GROUP = SSM_HEADS // SSM_GROUPS
GROUP_WIDTH = HEADS_PER_GROUP * SSM_HEAD_DIM
HY_EMB = 33
HY_FILTER_WIDTH = 64
HY_TARGET = 1e-2
HY_FAST_PCT = 0.3
HY_SLOW_PCT = 1.5
N_EXPERTS = 32
TOP_K = 4
MOE_FF = 2048
SWIGLU_LIMIT = 7.0
SWIGLU_ALPHA = 1.702
MOE_ROWS = 512
MOE_FF_TILE = 512
MOE_N_FF = MOE_FF // MOE_FF_TILE
LANES = 128
MXU_COLS = 256
VMEM_LIMIT = 56 * 1024 * 1024


def _cparams(sem, vmem=VMEM_LIMIT):
    return pltpu.CompilerParams(dimension_semantics=sem, vmem_limit_bytes=vmem)


def _dot(a, b):
    return jnp.dot(a, b, preferred_element_type=F32)


def _dot_nt(a, b):
    return lax.dot_general(a, b, (((1,), (1,)), ((), ())), preferred_element_type=F32)


def _dot_tn(a, b):
    return lax.dot_general(a, b, (((0,), (0,)), ((), ())), preferred_element_type=F32)


def _silu(x):
    return x / (1.0 + jnp.exp(-x))


def _softplus(x):
    return jnp.maximum(x, 0.0) + jnp.log1p(jnp.exp(-jnp.abs(x)))


def _ada_kernel(c_ref, w_ref, b_ref, o_ref):
    sc = _silu(c_ref[...])
    hi = sc.astype(BF16)
    lo = (sc - hi.astype(F32)).astype(BF16)
    w = w_ref[0].astype(BF16)
    o_ref[0] = _dot(hi, w) + _dot(lo, w) + b_ref[0]


def _ada_mod(cc, ada_w, ada_b):
    depth, d, n = ada_w.shape
    tn = 1024
    return pl.pallas_call(
        _ada_kernel,
        grid=(depth, n // tn),
        in_specs=[
            pl.BlockSpec((cc.shape[0], d), lambda i, j: (0, 0)),
            pl.BlockSpec((1, d, tn), lambda i, j: (i, 0, j)),
            pl.BlockSpec((1, 1, tn), lambda i, j: (i, 0, j)),
        ],
        out_specs=pl.BlockSpec((1, cc.shape[0], tn), lambda i, j: (i, 0, j)),
        out_shape=jax.ShapeDtypeStruct((depth, cc.shape[0], n), F32),
        compiler_params=_cparams(("parallel", "parallel")),
        name="ada_mod",
    )(cc, ada_w, ada_b.reshape(depth, 1, n))


def _nmm_kernel(x_ref, g_ref, sh_ref, sc_ref, w_ref, b_ref, o_ref, h_scr):
    @pl.when(pl.program_id(2) == 0)
    def _():
        x = x_ref[0]
        ms = jnp.mean(x * x, axis=-1, keepdims=True)
        xn = x * lax.rsqrt(ms + EPS) * g_ref[...]
        h_scr[...] = (xn * (1.0 + sc_ref[0]) + sh_ref[0]).astype(BF16)

    o_ref[0] = (_dot(h_scr[...], w_ref[...]) + b_ref[...]).astype(o_ref.dtype)


def _norm_mod_matmul(x, g, shift, scale, w, bias, out_dtype, tm, tn):
    b, s, d = x.shape
    n = w.shape[1]
    tm = min(tm, s)
    return pl.pallas_call(
        _nmm_kernel,
        grid=(b, s // tm, n // tn),
        in_specs=[
            pl.BlockSpec((1, tm, d), lambda bi, i, j: (bi, i, 0)),
            pl.BlockSpec((1, d), lambda bi, i, j: (0, 0)),
            pl.BlockSpec((1, 1, d), lambda bi, i, j: (bi, 0, 0)),
            pl.BlockSpec((1, 1, d), lambda bi, i, j: (bi, 0, 0)),
            pl.BlockSpec((d, tn), lambda bi, i, j: (0, j)),
            pl.BlockSpec((1, tn), lambda bi, i, j: (0, j)),
        ],
        out_specs=pl.BlockSpec((1, tm, tn), lambda bi, i, j: (bi, i, j)),
        out_shape=jax.ShapeDtypeStruct((b, s, n), out_dtype),
        scratch_shapes=[pltpu.VMEM((tm, d), BF16)],
        compiler_params=_cparams(("parallel", "parallel", "arbitrary")),
        name="norm_mod_matmul",
    )(x, g.reshape(1, d), shift.reshape(b, 1, d), scale.reshape(b, 1, d), w, bias.reshape(1, n))


NA_PAIR_ROWS = NA_KH + 2
NA_VARIANTS = 5


def _pair_window_start(i, rows):
    return np.clip(2 * i - NA_KH // 2, 0, rows - NA_PAIR_ROWS)


def _attn_bias_table(rpb, rows):
    nh = rpb.shape[0]
    kc = np.arange(GRID_W)[:, None]
    qc = np.arange(GRID_W)[None, :]
    col_off = np.clip(kc - qc, -(NA_KW - 1), NA_KW - 1) + NA_KW - 1
    onehot = (col_off[None] == np.arange(2 * NA_KW - 1)[:, None, None]).astype(np.float32)
    toep = jnp.einsum('hrc,ckq->hrkq', rpb.astype(F32), onehot, precision=HI)
    ws_col = np.clip(qc - NA_KW // 2, 0, GRID_W - NA_KW)
    toep = jnp.where(((kc >= ws_col) & (kc < ws_col + NA_KW))[None, None], toep, NEG_INF)
    dead = jnp.full((nh, GRID_W, GRID_W), NEG_INF, F32)
    rep_pair = [0, 1, 2, rows // 2 - 2, rows // 2 - 1]
    blocks = []
    for i in rep_pair:
        ws = int(_pair_window_start(i, rows))
        for t in range(NA_PAIR_ROWS):
            for e in range(2):
                r = 2 * i + e
                rs = int(np.clip(r - NA_KH // 2, 0, rows - NA_KH))
                kr = ws + t
                blocks.append(toep[:, kr - r + NA_KH - 1] if rs <= kr < rs + NA_KH else dead)
    bias = jnp.stack(blocks, axis=1).reshape(nh, NA_VARIANTS, NA_PAIR_ROWS, 2, GRID_W, GRID_W)
    return bias.transpose(0, 1, 2, 4, 3, 5).reshape(nh, NA_VARIANTS, NA_PAIR_ROWS * GRID_W, 2 * GRID_W)


def _head_rmsnorm(t, g):
    sq = t * t
    hi = sq.astype(BF16)
    lo = (sq - hi.astype(F32)).astype(BF16)
    avg = jnp.full((NA_HEAD_DIM, NA_HEAD_DIM), 1.0 / NA_HEAD_DIM, BF16)
    ms = _dot(hi, avg) + _dot(lo, avg)
    return t * lax.rsqrt(ms + EPS) * g


def _attn_kernel(q_ref, k_ref, v_ref, kc_ref, vc_ref, qg_ref, kg_ref, bias_ref, o_ref, qs, ks, kcs, vt, vct):
    seq, hd = q_ref.shape[1], q_ref.shape[2]
    rows = seq // GRID_W
    pair = 2 * GRID_W
    win_blocks = NA_PAIR_ROWS // 2
    ctx_blocks = kc_ref.shape[1] // pair
    qs[...] = (_head_rmsnorm(q_ref[0].astype(F32), qg_ref[...]) * hd ** -0.5).astype(BF16)
    ks[...] = _head_rmsnorm(k_ref[0].astype(F32), kg_ref[...]).astype(BF16)
    kcs[...] = _head_rmsnorm(kc_ref[0].astype(F32), kg_ref[...]).astype(BF16)
    for blk in range(seq // pair):
        vt[blk] = v_ref[0, blk * pair:(blk + 1) * pair, :].astype(F32).T.astype(BF16)
    for blk in range(ctx_blocks):
        vct[blk] = vc_ref[0, blk * pair:(blk + 1) * pair, :].astype(F32).T.astype(BF16)

    def body(i, carry):
        ws = jnp.clip(2 * i - NA_KH // 2, 0, rows - NA_PAIR_ROWS)
        var = (2 * i - ws) // 2
        wb = ws // 2
        q0 = pl.multiple_of(i * pair, pair)
        k0 = pl.multiple_of(ws * GRID_W, pair)
        q_p = qs[pl.ds(q0, pair), :]
        s_lat = _dot_nt(ks[pl.ds(k0, win_blocks * pair), :], q_p) + bias_ref[0, var]
        s_ctx = _dot_nt(kcs[...], q_p)
        m = jnp.maximum(jnp.max(s_lat, axis=0, keepdims=True), jnp.max(s_ctx, axis=0, keepdims=True))
        p_lat = jnp.exp(s_lat - m)
        p_ctx = jnp.exp(s_ctx - m)
        den = jnp.sum(p_lat, axis=0, keepdims=True) + jnp.sum(p_ctx, axis=0, keepdims=True)
        p_lat = p_lat.astype(BF16)
        p_ctx = p_ctx.astype(BF16)
        o_t = jnp.zeros((hd, pair), F32)
        for blk in range(win_blocks):
            o_t = o_t + _dot(vt[wb + blk], p_lat[blk * pair:(blk + 1) * pair, :])
        for blk in range(ctx_blocks):
            o_t = o_t + _dot(vct[blk], p_ctx[blk * pair:(blk + 1) * pair, :])
        o_ref[0, pl.ds(q0, pair), :] = (o_t / den).T.astype(o_ref.dtype)
        return carry

    lax.fori_loop(0, rows // 2, body, 0, unroll=2)


def _attention(proj, proj_c, q_norm, k_norm, bias_tab):
    b, s, _ = proj.shape
    ctx = proj_c.shape[1]
    hd = NA_HEAD_DIM
    nh = NA_HEADS
    return pl.pallas_call(
        _attn_kernel,
        grid=(b, nh),
        in_specs=[
            pl.BlockSpec((1, s, hd), lambda bi, h: (bi, 0, h)),
            pl.BlockSpec((1, s, hd), lambda bi, h: (bi, 0, 2 * nh + h)),
            pl.BlockSpec((1, s, hd), lambda bi, h: (bi, 0, 3 * nh + h)),
            pl.BlockSpec((1, ctx, hd), lambda bi, h: (bi, 0, h)),
            pl.BlockSpec((1, ctx, hd), lambda bi, h: (bi, 0, nh + h)),
            pl.BlockSpec((1, hd), lambda bi, h: (0, 0)),
            pl.BlockSpec((1, hd), lambda bi, h: (0, 0)),
            pl.BlockSpec((1,) + bias_tab.shape[1:], lambda bi, h: (h, 0, 0, 0)),
        ],
        out_specs=pl.BlockSpec((1, s, hd), lambda bi, h: (bi, 0, h)),
        out_shape=jax.ShapeDtypeStruct((b, s, nh * hd), BF16),
        scratch_shapes=[pltpu.VMEM((s, hd), BF16), pltpu.VMEM((s, hd), BF16), pltpu.VMEM((ctx, hd), BF16),
                        pltpu.VMEM((s // (2 * GRID_W), hd, 2 * GRID_W), BF16), pltpu.VMEM((ctx // (2 * GRID_W), hd, 2 * GRID_W), BF16)],
        compiler_params=_cparams(("parallel", "parallel")),
        name="nbr_attention",
    )(proj, proj, proj, proj_c, proj_c, q_norm.reshape(1, hd), k_norm.reshape(1, hd), bias_tab)


def _split3(v):
    hi = v.astype(BF16)
    r1 = v - hi.astype(F32)
    mid = r1.astype(BF16)
    lo = (r1 - mid.astype(F32)).astype(BF16)
    return hi, mid, lo


def _ssd_kernel(*refs, with_y, nc):
    (xr_ref, br_ref, cr_ref, cwx, cbx, cwb, cbb, cwc, cbc, dt_ref, bias_ref, a_ref), rest = refs[:12], refs[12:]
    if with_y:
        (z_ref, dsk_ref, gn_ref, sf0_ref, sb0_ref, y_ref, xs, bs, cs, contrib, dec, wrow, xf, sfa, sba, drow, dtrow, dsplit, esplit) = rest
    else:
        sf_out, sb_out, xs, bs, cs, contrib, dec, wrow = rest
    ck = SSM_CHUNK
    length = nc * ck
    nh = HEADS_PER_GROUP
    nhd = 2 * nh
    gw = GROUP_WIDTH
    dot_hi = functools.partial(jnp.dot, precision=HI, preferred_element_type=F32)

    halo = 16
    win = ck + 2 * halo
    conv_w = jnp.concatenate([cwx[...], cwb[...], cwc[...]], axis=1)
    conv_b = jnp.concatenate([cbx[...], cbb[...], cbc[...]], axis=1)
    wi = lax.broadcasted_iota(jnp.int32, (ck, win), 1) - lax.broadcasted_iota(jnp.int32, (ck, win), 0)

    def conv_silu_chunk(c):
        l0 = pl.multiple_of(c * ck, ck)
        s0 = pl.multiple_of(jnp.clip(l0 - halo, 0, length - win), halo)
        rel = wi + (s0 - l0)
        window = jnp.concatenate([r[0, pl.ds(s0, win), :] for r in (xr_ref, br_ref, cr_ref)], axis=1)
        cur = jnp.concatenate([r[0, pl.ds(l0, ck), :] for r in (xr_ref, br_ref, cr_ref)], axis=1).astype(F32)
        prev = _dot((rel == -1).astype(BF16), window)
        nxt = _dot((rel == 1).astype(BF16), window)
        y = _silu(conv_b + prev * conv_w[0:1, :] + cur * conv_w[1:2, :] + nxt * conv_w[2:3, :])
        xs[pl.ds(l0, ck), :] = y[:, :gw].astype(BF16)
        if with_y:
            xf[pl.ds(l0, ck), :] = y[:, :gw]
        bs[pl.ds(l0, ck), :] = y[:, gw:gw + SSM_STATE].astype(BF16)
        cs[pl.ds(l0, ck), :] = y[:, gw + SSM_STATE:].astype(BF16)
        return y[:, :gw], y[:, gw:gw + SSM_STATE].astype(BF16)

    ri = lax.broadcasted_iota(jnp.int32, (ck, ck), 0)
    ci = lax.broadcasted_iota(jnp.int32, (ck, ck), 1)
    lower = ci <= ri
    upper = ci >= ri
    eye = (ci == ri).astype(BF16)
    dt_row = _softplus(dt_ref[0, 0] + bias_ref[0])
    a_row = dt_row * a_ref[0]
    cum_f = dot_hi(a_row, upper.astype(F32))
    cum_b = dot_hi(a_row, lower.astype(F32))
    tot = dot_hi(a_row, jnp.ones((ck, ck), F32))
    d_row = jnp.where((ri % nhd) < nh, cum_f, cum_b)
    wrow[...] = jnp.exp(tot - d_row) * dt_row
    texp = jnp.exp(tot)
    row4 = lax.broadcasted_iota(jnp.int32, (ck, 2 * gw), 0)
    lane4 = lax.broadcasted_iota(jnp.int32, (ck, 2 * gw), 1) // SSM_HEAD_DIM
    tmask = jnp.where(row4 % nhd == lane4, jnp.concatenate([texp] * (2 * gw // ck), axis=1), 0.0)
    pick = (lax.broadcasted_iota(jnp.int32, (dec.shape[0], ck), 1) // nhd
            == lax.broadcasted_iota(jnp.int32, (dec.shape[0], ck), 0)).astype(F32)
    dec[...] = dot_hi(pick, tmask)
    if with_y:
        drow[...] = d_row
        dtrow[...] = dt_row
        for k, part in enumerate(_split3(d_row)):
            dsplit[k] = part.astype(F32)
        for k, part in enumerate(_split3(jnp.exp(d_row))[:2]):
            esplit[k] = part.astype(F32)

    def spread(src, r0, width):
        rows = jnp.concatenate([jnp.broadcast_to(src[pl.ds(r0 + hd, 1), :], (width, ck)) for hd in range(nhd)], axis=0)
        return _dot_nt(eye, rows.astype(BF16))

    def chunk_states(c, carry):
        xc, b_c = conv_silu_chunk(c)
        wx = spread(wrow, c * nhd, SSM_HEAD_DIM)
        xw = jnp.concatenate([(xc * wx[:, :gw]).astype(BF16), (xc * wx[:, gw:]).astype(BF16)], axis=1)
        contrib[c] = _dot_tn(b_c, xw)
        return carry

    lax.fori_loop(0, nc, chunk_states, 0, unroll=2)

    def fwd_chain(c, s):
        if with_y:
            sfa[c] = s
        return s * dec[pl.ds(c, 1), :gw] + contrib[c, :, :gw]

    def bwd_chain(i, s):
        c = nc - 1 - i
        if with_y:
            sba[c] = s
        return s * dec[pl.ds(c, 1), gw:] + contrib[c, :, gw:]

    if with_y:
        s_f0 = sf0_ref[0, 0]
        s_b0 = sb0_ref[0, 0]
    else:
        s_f0 = jnp.zeros((SSM_STATE, gw), F32)
        s_b0 = s_f0
    s_f = lax.fori_loop(0, nc, fwd_chain, s_f0)
    s_b = lax.fori_loop(0, nc, bwd_chain, s_b0)
    if not with_y:
        sf_out[0, 0] = s_f
        sb_out[0, 0] = s_b
        return

    lane_head = lax.broadcasted_iota(jnp.int32, (ck, gw), 1) // SSM_HEAD_DIM

    def out_step(c, carry):
        l0 = pl.multiple_of(c * ck, ck)
        r0 = c * nhd
        d_b = spread(dsplit.at[0], r0, ck) + spread(dsplit.at[1], r0, ck) + spread(dsplit.at[2], r0, ck)
        e_b = spread(esplit.at[0], r0, SSM_HEAD_DIM) + spread(esplit.at[1], r0, SSM_HEAD_DIM)
        b_c = bs[pl.ds(l0, ck), :]
        c_c = cs[pl.ds(l0, ck), :]
        x_c = xs[pl.ds(l0, ck), :]
        g = _dot_nt(c_c, b_c)
        acc = jnp.zeros((ck, gw), F32)
        for j in range(nh):
            jb = nh + j
            lf = jnp.exp(jnp.where(lower, d_b[:, j * ck:(j + 1) * ck] - drow[pl.ds(r0 + j, 1), :], -1e30)) * dtrow[pl.ds(r0 + j, 1), :]
            lb = jnp.exp(jnp.where(upper, d_b[:, jb * ck:(jb + 1) * ck] - drow[pl.ds(r0 + jb, 1), :], -1e30)) * dtrow[pl.ds(r0 + jb, 1), :]
            m = (g * (lf + lb)).astype(BF16)
            acc = acc + _dot(m, jnp.where(lane_head == j, x_c, jnp.zeros_like(x_c)))
        acc = acc + _dot(c_c, sfa[c].astype(BF16)) * e_b[:, :gw]
        acc = acc + _dot(c_c, sba[c].astype(BF16)) * e_b[:, gw:]
        y = acc + xf[pl.ds(l0, ck), :] * dsk_ref[...]
        y = y * _silu(z_ref[0, pl.ds(l0, ck), :].astype(F32))
        ms = jnp.mean(y * y, axis=-1, keepdims=True)
        y_ref[0, pl.ds(l0, ck), :] = (y * lax.rsqrt(ms + EPS) * gn_ref[...]).astype(y_ref.dtype)
        return carry

    lax.fori_loop(0, nc, out_step, 0, unroll=2)


def _ssd(proj, x_blk0, z_blk0, dt_raw, conv_w, conv_b, dt_bias, a_log, d_skip, gate_norm, init):
    b, length, _ = proj.shape
    nc = length // SSM_CHUNK
    ng, nh, gw, st = SSM_GROUPS, HEADS_PER_GROUP, GROUP_WIDTH, SSM_STATE
    with_y = init is not None
    b_blk0 = x_blk0 * (gw // st) + SSM_D_INNER // st
    c_blk0 = b_blk0 + ng
    ck = SSM_CHUNK
    nhd = 2 * nh
    assert nc * nhd <= ck
    dtg = dt_raw.reshape(b, nc, ck, 2, ng, nh).transpose(0, 4, 1, 3, 5, 2).reshape(b, ng, nc * nhd, ck)
    dtg = jnp.pad(dtg, ((0, 0), (0, 0), (0, ck - nc * nhd), (0, 0)))
    per_row = lambda t: jnp.tile(t.reshape(2, ng, nh).transpose(1, 0, 2).reshape(ng, nhd), (1, ck // nhd)).reshape(ng, ck, 1)
    bias_r = per_row(dt_bias)
    a_r = per_row(-jnp.exp(a_log.astype(F32)))
    cw_x, cw_b, cw_c = conv_w[:, :SSM_D_INNER], conv_w[:, SSM_D_INNER:SSM_D_INNER + ng * st], conv_w[:, SSM_D_INNER + ng * st:]
    cb = conv_b.reshape(1, -1)
    cb_x, cb_b, cb_c = cb[:, :SSM_D_INNER], cb[:, SSM_D_INNER:SSM_D_INNER + ng * st], cb[:, SSM_D_INNER + ng * st:]
    in_specs = [
        pl.BlockSpec((1, length, gw), lambda bi, g: (bi, 0, x_blk0 + g)),
        pl.BlockSpec((1, length, st), lambda bi, g: (bi, 0, b_blk0 + g)),
        pl.BlockSpec((1, length, st), lambda bi, g: (bi, 0, c_blk0 + g)),
        pl.BlockSpec((3, gw), lambda bi, g: (0, g)),
        pl.BlockSpec((1, gw), lambda bi, g: (0, g)),
        pl.BlockSpec((3, st), lambda bi, g: (0, g)),
        pl.BlockSpec((1, st), lambda bi, g: (0, g)),
        pl.BlockSpec((3, st), lambda bi, g: (0, g)),
        pl.BlockSpec((1, st), lambda bi, g: (0, g)),
        pl.BlockSpec((1, 1, ck, ck), lambda bi, g: (bi, g, 0, 0)),
        pl.BlockSpec((1, ck, 1), lambda bi, g: (g, 0, 0)),
        pl.BlockSpec((1, ck, 1), lambda bi, g: (g, 0, 0)),
    ]
    args = [proj, proj, proj, cw_x, cb_x, cw_b, cb_b, cw_c, cb_c, dtg, bias_r, a_r]
    scratch = [pltpu.VMEM((length, gw), BF16), pltpu.VMEM((length, st), BF16), pltpu.VMEM((length, st), BF16),
               pltpu.VMEM((nc, st, 2 * gw), F32), pltpu.VMEM((ck // nhd, 2 * gw), F32), pltpu.VMEM((ck, ck), F32)]
    state_spec = pl.BlockSpec((1, 1, st, gw), lambda bi, g: (bi, g, 0, 0))
    state_shape = jax.ShapeDtypeStruct((b, ng, st, gw), F32)
    if with_y:
        in_specs += [
            pl.BlockSpec((1, length, gw), lambda bi, g: (bi, 0, z_blk0 + g)),
            pl.BlockSpec((1, gw), lambda bi, g: (0, g)),
            pl.BlockSpec((1, gw), lambda bi, g: (0, g)),
            state_spec, state_spec,
        ]
        args += [proj, jnp.repeat(d_skip, SSM_HEAD_DIM).reshape(1, -1), gate_norm.reshape(1, -1), init[0], init[1]]
        out_specs = pl.BlockSpec((1, length, gw), lambda bi, g: (bi, 0, g))
        out_shape = jax.ShapeDtypeStruct((b, length, SSM_D_INNER), BF16)
        scratch += [pltpu.VMEM((length, gw), F32), pltpu.VMEM((nc, st, gw), F32), pltpu.VMEM((nc, st, gw), F32),
                    pltpu.VMEM((ck, ck), F32), pltpu.VMEM((ck, ck), F32), pltpu.VMEM((3, ck, ck), F32), pltpu.VMEM((2, ck, ck), F32)]
    else:
        out_specs = [state_spec, state_spec]
        out_shape = [state_shape, state_shape]
    return pl.pallas_call(
        functools.partial(_ssd_kernel, with_y=with_y, nc=nc),
        grid=(b, ng),
        in_specs=in_specs,
        out_specs=out_specs,
        out_shape=out_shape,
        scratch_shapes=scratch,
        compiler_params=_cparams(("parallel", "parallel")),
        name="ssd_main" if with_y else "ssd_ctx",
    )(*args)


def _proj_res_kernel(*refs, n_lhs):
    a_refs, w_refs = refs[:n_lhs], refs[n_lhs:2 * n_lhs]
    b_ref, gate_ref, res_ref, o_ref = refs[2 * n_lhs:]
    acc = _dot(a_refs[0][0], w_refs[0][...])
    for a_ref, w_ref in zip(a_refs[1:], w_refs[1:]):
        acc = acc + _dot(a_ref[0], w_ref[...])
    o_ref[0] = res_ref[0] + gate_ref[0] * (acc + b_ref[...])


def _proj_residual(lhs, w, bias, gate, resid, tm=1024, tn=1024):
    b, s, k = lhs[0].shape
    n = w.shape[1]
    n_lhs = len(lhs)
    in_specs = [pl.BlockSpec((1, tm, k), lambda bi, i, j: (bi, i, 0)) for _ in lhs]
    in_specs += [pl.BlockSpec((k, tn), functools.partial(lambda bi, i, j, t: (t, j), t=t)) for t in range(n_lhs)]
    in_specs += [
        pl.BlockSpec((1, tn), lambda bi, i, j: (0, j)),
        pl.BlockSpec((1, 1, tn), lambda bi, i, j: (bi, 0, j)),
        pl.BlockSpec((1, tm, tn), lambda bi, i, j: (bi, i, j)),
    ]
    return pl.pallas_call(
        functools.partial(_proj_res_kernel, n_lhs=n_lhs),
        grid=(b, s // tm, n // tn),
        in_specs=in_specs,
        out_specs=pl.BlockSpec((1, tm, tn), lambda bi, i, j: (bi, i, j)),
        out_shape=jax.ShapeDtypeStruct((b, s, n), F32),
        compiler_params=_cparams(("parallel", "parallel", "parallel")),
        name="proj_residual",
    )(*lhs, *([w] * n_lhs), bias.reshape(1, n), gate.reshape(b, 1, n), resid)


def _hy_filter_kernel(tw_ref, f_ref, w1t, w1c, w1s, b1, w2, b2, w3, b3, fr, w4_ref, dl_ref, o_ref, h_scr):
    dot_hi = lambda a, b: jnp.dot(a, b, precision=HI, preferred_element_type=F32)

    @pl.when(pl.program_id(0) == 0)
    def _():
        t = tw_ref[:, 0:1]
        ang = tw_ref[:, 1:2] * f_ref[...]
        pre = t * w1t[...] + dot_hi(jnp.cos(ang), w1c[...]) + dot_hi(-jnp.sin(ang), w1s[...]) + b1[...]
        h = jnp.sin(fr[...] * pre)
        h = jnp.sin(fr[...] * (dot_hi(h, w2[...]) + b2[...]))
        h_scr[...] = jnp.sin(fr[...] * (dot_hi(h, w3[...]) + b3[...]))

    o_ref[...] = dot_hi(h_scr[...], w4_ref[...]) * jnp.exp(-tw_ref[:, 0:1] * dl_ref[...])


def _hyena_filters(length, w1, b1, w2, b2, w3, b3, w4, freq):
    bands = (HY_EMB - 1) // 2
    fw = HY_FILTER_WIDTH
    t = jnp.linspace(0.0, 1.0, length, dtype=F32)
    w = 2 * math.pi * jnp.arange(length, dtype=F32) / length
    tw = jnp.stack([t, w], axis=1)
    f = jnp.linspace(1e-4, bands - 1, bands, dtype=F32).reshape(1, bands)
    min_decay = math.log(HY_TARGET) / HY_SLOW_PCT
    max_decay = math.log(HY_TARGET) / HY_FAST_PCT
    deltas = jnp.abs(jnp.linspace(min_decay, max_decay, D_MODEL, dtype=F32))
    dl = jnp.concatenate([deltas, deltas]).reshape(1, 2 * D_MODEL)
    tn = 1024
    small = lambda shape: pl.BlockSpec(shape, lambda j: (0, 0))
    row = lambda v: v.reshape(1, -1)
    return pl.pallas_call(
        _hy_filter_kernel,
        grid=(2 * D_MODEL // tn,),
        in_specs=[small((length, 2)), small((1, bands)), small((1, fw)), small((bands, fw)), small((bands, fw)), small((1, fw)),
                  small((fw, fw)), small((1, fw)), small((fw, fw)), small((1, fw)), small((1, fw)),
                  pl.BlockSpec((fw, tn), lambda j: (0, j)), pl.BlockSpec((1, tn), lambda j: (0, j))],
        out_specs=pl.BlockSpec((length, tn), lambda j: (0, j)),
        out_shape=jax.ShapeDtypeStruct((length, 2 * D_MODEL), F32),
        scratch_shapes=[pltpu.VMEM((length, fw), F32)],
        compiler_params=_cparams(("arbitrary",)),
        name="hyena_filters",
    )(tw, f, w1[0:1], w1[1:1 + bands], w1[1 + bands:], row(b1), w2, row(b2), w3, row(b3), row(freq), w4, dl)


def _hy_conv_kernel(x0_ref, x1_ref, v_ref, w0, b0, w1, b1, w2, b2, x0_out, u_out):
    length = x0_ref.shape[1]

    def conv(raw_ref, w_ref, b_ref):
        x = raw_ref[0].astype(F32)
        rows = lax.broadcasted_iota(jnp.int32, x.shape, 0)
        xm1 = jnp.where(rows == 0, 0.0, pltpu.roll(x, 1, 0))
        xp1 = jnp.where(rows == length - 1, 0.0, pltpu.roll(x, length - 1, 0))
        return b_ref[...] + xm1 * w_ref[0:1, :] + x * w_ref[1:2, :] + xp1 * w_ref[2:3, :]

    x0_out[0] = conv(x0_ref, w0, b0).astype(x0_out.dtype)
    u_out[0] = (conv(v_ref, w2, b2) * conv(x1_ref, w1, b1)).astype(u_out.dtype)


def _hyena_conv(proj, conv_w, conv_b):
    b, length, _ = proj.shape
    d = D_MODEL
    tn = 512
    nb = d // tn
    cb = conv_b.reshape(1, -1)
    seg = lambda k: pl.BlockSpec((1, length, tn), lambda bi, j: (bi, 0, k * nb + j))
    wseg = lambda k: pl.BlockSpec((3, tn), lambda bi, j: (0, k * nb + j))
    bseg = lambda k: pl.BlockSpec((1, tn), lambda bi, j: (0, k * nb + j))
    out = pl.BlockSpec((1, length, tn), lambda bi, j: (bi, 0, j))
    return pl.pallas_call(
        _hy_conv_kernel,
        grid=(b, nb),
        in_specs=[seg(0), seg(1), seg(2), wseg(0), bseg(0), wseg(1), bseg(1), wseg(2), bseg(2)],
        out_specs=[out, out],
        out_shape=[jax.ShapeDtypeStruct((b, length, d), BF16)] * 2,
        compiler_params=_cparams(("parallel", "parallel")),
        name="hyena_conv",
    )(proj, proj, proj, conv_w, cb, conv_w, cb, conv_w, cb)


def _dft_matrices(length):
    n = 2 * length
    f = lax.broadcasted_iota(jnp.int32, (length, length), 0)
    t = lax.broadcasted_iota(jnp.int32, (length, length), 1)
    ang = ((f * t) % n).astype(F32) * (2 * math.pi / n)
    sign = (1 - 2 * (t % 2)).astype(F32)
    fwd_c = jnp.cos(ang)
    fwd_s = jnp.where(f == 0, sign, -jnp.sin(ang))
    fwd = jnp.concatenate([fwd_c, fwd_s], axis=0).astype(BF16)
    wgt = jnp.where(f == 0, 1.0, 2.0) / n
    inv = jnp.concatenate([(fwd_c * wgt).T, (fwd_s * wgt).T], axis=1).astype(BF16)
    return fwd, inv


def _mm_kernel(a_ref, b_ref, o_ref):
    o_ref[...] = _dot(a_ref[...], b_ref[...]).astype(o_ref.dtype)


def _matmul(a, b, out_dtype=F32, tm=1024, tn=1024):
    m, k = a.shape
    n = b.shape[1]
    return pl.pallas_call(
        _mm_kernel,
        grid=(n // tn, m // tm),
        in_specs=[pl.BlockSpec((tm, k), lambda j, i: (i, 0)), pl.BlockSpec((k, tn), lambda j, i: (0, j))],
        out_specs=pl.BlockSpec((tm, tn), lambda j, i: (i, j)),
        out_shape=jax.ShapeDtypeStruct((m, n), out_dtype),
        compiler_params=_cparams(("parallel", "parallel")),
        name="matmul",
    )(a, b)


def _filter_spectrum(kk, fwd):
    length = kk.shape[0]
    d = D_MODEL
    k_f, k_b = kk[:, :d], kk[:, d:]
    k_fp = k_f.at[0].add(k_b[0])
    k_bp = k_b.at[0].set(0.0)
    spec = _matmul(fwd, jnp.concatenate([k_fp, k_bp], axis=1).astype(BF16))
    a_re, a_im, b_re, b_im = spec[:length, :d], spec[length:, :d], spec[:length, d:], spec[length:, d:]
    nz = (jnp.arange(length) != 0)[:, None]
    k_re = a_re + b_re
    k_im = jnp.where(nz, a_im - b_im, a_im + b_im)
    return jnp.stack([k_re, jnp.where(nz, k_im, 0.0), jnp.where(nz, k_re, k_im)])


def _dft_fwd_kernel(fc_ref, fs_ref, u_ref, k_ref, y_ref):
    u = u_ref[0]
    re = _dot(fc_ref[...], u)
    im = _dot(fs_ref[...], u)
    y_ref[0, 0] = (re * k_ref[0] - im * k_ref[1]).astype(y_ref.dtype)
    y_ref[0, 1] = (re * k_ref[1] + im * k_ref[2]).astype(y_ref.dtype)


def _dft_forward(u, fwd, coef, tm=1024, tn=512):
    b, length, d = u.shape
    ni = length // tm
    return pl.pallas_call(
        _dft_fwd_kernel,
        grid=(d // tn, ni, b),
        in_specs=[
            pl.BlockSpec((tm, length), lambda j, i, bi: (i, 0)),
            pl.BlockSpec((tm, length), lambda j, i, bi: (ni + i, 0)),
            pl.BlockSpec((1, length, tn), lambda j, i, bi: (bi, 0, j)),
            pl.BlockSpec((3, tm, tn), lambda j, i, bi: (0, i, j)),
        ],
        out_specs=pl.BlockSpec((1, 2, tm, tn), lambda j, i, bi: (bi, 0, i, j)),
        out_shape=jax.ShapeDtypeStruct((b, 2, length, d), BF16),
        compiler_params=_cparams(("parallel", "parallel", "parallel")),
        name="dft_forward",
    )(fwd, fwd, u, coef)


def _dft_inv_kernel(g_ref, y_ref, x0_ref, u_ref, fb_ref, o_ref):
    conv = _dot(g_ref[...], y_ref[0])
    o_ref[0] = (x0_ref[0].astype(F32) * (conv + u_ref[0].astype(F32) * fb_ref[...])).astype(o_ref.dtype)


def _dft_inverse(y, inv, x0, u, filt_bias, tm=512, tn=512):
    b, n2, d = y.shape
    length = n2 // 2
    return pl.pallas_call(
        _dft_inv_kernel,
        grid=(d // tn, length // tm, b),
        in_specs=[
            pl.BlockSpec((tm, n2), lambda j, i, bi: (i, 0)),
            pl.BlockSpec((1, n2, tn), lambda j, i, bi: (bi, 0, j)),
            pl.BlockSpec((1, tm, tn), lambda j, i, bi: (bi, i, j)),
            pl.BlockSpec((1, tm, tn), lambda j, i, bi: (bi, i, j)),
            pl.BlockSpec((1, tn), lambda j, i, bi: (0, j)),
        ],
        out_specs=pl.BlockSpec((1, tm, tn), lambda j, i, bi: (bi, i, j)),
        out_shape=jax.ShapeDtypeStruct((b, length, d), BF16),
        compiler_params=_cparams(("parallel", "parallel", "parallel")),
        name="dft_inverse",
    )(inv, y, x0, u, filt_bias.reshape(1, d))


PACK_HI = 0xFFFF0000
TILE_SUBLANES = 8


def _pack_pairs(lo, hi):
    lo_b = pltpu.bitcast(lo.astype(BF16).astype(F32), jnp.uint32)
    hi_b = pltpu.bitcast(hi.astype(BF16).astype(F32), jnp.uint32)
    return (lo_b >> 16) | (hi_b & jnp.uint32(PACK_HI))


def _unpack_pairs(w):
    return pltpu.bitcast(w << 16, F32), pltpu.bitcast(w & jnp.uint32(PACK_HI), F32)


def _router_kernel(x_ref, g_ref, sh_ref, sc_ref, wr_ref, br_ref, hp_ref, idx_ref, gate_ref, rank_ref, cnt_ref, run):
    tm = x_ref.shape[1]
    half = x_ref.shape[2] // 2
    first = (pl.program_id(0) == 0) & (pl.program_id(1) == 0)

    @pl.when(first)
    def _():
        run[...] = jnp.zeros_like(run)

    x = x_ref[0]
    ms = jnp.mean(x * x, axis=-1, keepdims=True)
    h = x * lax.rsqrt(ms + EPS) * g_ref[...] * (1.0 + sc_ref[0]) + sh_ref[0]
    hp_ref[0] = _pack_pairs(h[:, :half], h[:, half:])

    logits = jnp.dot(h, wr_ref[...], precision=HI, preferred_element_type=F32) + br_ref[...]
    lane_i = lax.broadcasted_iota(jnp.int32, logits.shape, 1)
    lane = lane_i.astype(F32)
    work = jnp.where(lane_i < N_EXPERTS, logits, -jnp.inf)
    li = lax.broadcasted_iota(jnp.int32, (tm, tm), 0)
    ki = lax.broadcasted_iota(jnp.int32, (tm, tm), 1)
    strict_lower = (ki < li).astype(BF16)
    sels, tops = [], []
    for _ in range(TOP_K):
        m = jnp.max(work, axis=-1, keepdims=True)
        first_idx = jnp.min(jnp.where(work == m, lane, float(LANES)), axis=-1, keepdims=True)
        sel = lane == first_idx
        sels.append(sel)
        tops.append((m, first_idx))
        work = jnp.where(sel, -jnp.inf, work)
    chosen = sels[0] | sels[1] | sels[2] | sels[3]
    before = _dot(strict_lower, chosen.astype(BF16)) + run[...]
    run[...] = run[...] + jnp.sum(chosen.astype(F32), axis=0, keepdims=True)
    den = sum(jnp.exp(m - tops[0][0]) for m, _ in tops)
    idx_o = jnp.zeros(logits.shape, jnp.int32)
    gate_o = jnp.zeros(logits.shape, F32)
    rank_o = jnp.zeros(logits.shape, jnp.int32)
    for k, (sel, (m, first_idx)) in enumerate(zip(sels, tops)):
        rank = jnp.sum(jnp.where(sel, before, 0.0), axis=-1, keepdims=True).astype(jnp.int32)
        idx_o = jnp.where(lane_i == k, first_idx.astype(jnp.int32), idx_o)
        gate_o = jnp.where(lane_i == k, jnp.exp(m - tops[0][0]) / den, gate_o)
        rank_o = jnp.where(lane_i == k, rank, rank_o)
    idx_ref[0] = idx_o
    gate_ref[0] = gate_o
    rank_ref[0] = rank_o
    cnt_ref[...] = run[...]


def _moe_route(x, g, shift, scale, w_r, b_r, tm=512):
    b, s, d = x.shape
    wr = jnp.zeros((d, LANES), F32).at[:, :N_EXPERTS].set(w_r)
    br = jnp.zeros((1, LANES), F32).at[0, :N_EXPERTS].set(b_r)
    tok = lambda n: pl.BlockSpec((1, tm, n), lambda bi, i: (bi, i, 0))
    return pl.pallas_call(
        _router_kernel,
        grid=(b, s // tm),
        in_specs=[
            tok(d),
            pl.BlockSpec((1, d), lambda bi, i: (0, 0)),
            pl.BlockSpec((1, 1, d), lambda bi, i: (bi, 0, 0)),
            pl.BlockSpec((1, 1, d), lambda bi, i: (bi, 0, 0)),
            pl.BlockSpec((d, LANES), lambda bi, i: (0, 0)),
            pl.BlockSpec((1, LANES), lambda bi, i: (0, 0)),
        ],
        out_specs=[tok(d // 2), tok(LANES), tok(LANES), tok(LANES), pl.BlockSpec((1, LANES), lambda bi, i: (0, 0))],
        out_shape=[
            jax.ShapeDtypeStruct((b, s, d // 2), jnp.uint32),
            jax.ShapeDtypeStruct((b, s, LANES), jnp.int32),
            jax.ShapeDtypeStruct((b, s, LANES), F32),
            jax.ShapeDtypeStruct((b, s, LANES), jnp.int32),
            jax.ShapeDtypeStruct((1, LANES), F32),
        ],
        scratch_shapes=[pltpu.VMEM((1, LANES), F32)],
        compiler_params=_cparams(("arbitrary", "arbitrary")),
        name="moe_route",
    )(x, g.reshape(1, d), shift.reshape(b, 1, d), scale.reshape(b, 1, d), wr, br)


ROW_COPY_PRIORITY = 1
MOE_PITCH = TILE_SUBLANES + 1


def _ffn_kernel(be_ref, na_ref, dst_prv, tok_cur, tok_nxt, hp_ref, w1g_ref, w1u_ref, b1g_ref, b1u_ref, w2_ref, b2_ref, y_ref,
                xg, stg, xb, acc, gu, zeros, gsem, usem, zsem):
    i = pl.program_id(0)
    f = pl.program_id(1)
    nb = pl.num_programs(0)
    nf = pl.num_programs(1)
    tm = MOE_ROWS
    sub = hp_ref.shape[1]
    half = sub * LANES
    per = tm // MOE_N_FF
    slot = i % 2
    active = i < na_ref[0]

    def gather(tok_ref, to_slot, r, shift=0):
        return pltpu.make_async_copy(hp_ref.at[tok_ref[0, 0, r] + shift], xg.at[to_slot, pl.ds(r * MOE_PITCH, sub), :], gsem.at[to_slot])

    def scatter(dst_ref, from_slot, r, shift=0):
        return pltpu.make_async_copy(stg.at[from_slot, pl.ds(r * MOE_PITCH, sub), :], y_ref.at[dst_ref[0, 0, r] + shift], usem.at[from_slot])

    def wait_rows(buf, sem, s):
        pltpu.make_async_copy(buf.at[1 - s, pl.ds(0, tm * sub), :], buf.at[s, pl.ds(0, tm * sub), :], sem.at[s]).wait()

    @pl.when((i == 0) & (f == 0))
    def _():
        stg[...] = jnp.zeros_like(stg)
        zrows = zeros.shape[0]
        zeros[...] = jnp.zeros_like(zeros)
        dump0 = y_ref.shape[0] - 2 * tm
        fills = [pltpu.make_async_copy(zeros, y_ref.at[pl.ds(dump0 + c * zrows, zrows)], zsem) for c in range(2 * tm // zrows)]
        for cp in fills:
            cp.start()
        for cp in fills:
            cp.wait()
        for r in range(tm):
            gather(tok_cur, 0, r).start(priority=ROW_COPY_PRIORITY)

    @pl.when(f == 0)
    def _():
        wait_rows(xg, gsem, slot)
        for j in range(sub):
            lo, hi = _unpack_pairs(xg[slot, pl.ds(j, tm, stride=MOE_PITCH), :])
            xb[:, j * LANES:(j + 1) * LANES] = lo.astype(BF16)
            xb[:, half + j * LANES:half + (j + 1) * LANES] = hi.astype(BF16)
        acc[...] = jnp.broadcast_to(b2_ref[0, 0], acc.shape)

    def move_rows(group=0, n_groups=1, shift=0):
        for rr in range(per * group // n_groups, per * (group + 1) // n_groups):
            r = f * per + rr
            gather(tok_nxt, 1 - slot, r, shift).start(priority=ROW_COPY_PRIORITY)
            scatter(dst_prv, 1 - slot, r, shift).start(priority=ROW_COPY_PRIORITY)

    def after(v):
        bits = pltpu.bitcast(v[0:1, 0:1], jnp.int32)[0, 0]
        return (bits & 1) >> 1

    @pl.when(active)
    def _():
        pieces = acc.shape[1] // MXU_COLS
        n_groups = pieces + 2
        x = xb[...]
        g = _dot(x, w1g_ref[0, 0].astype(BF16)) + b1g_ref[0, 0]
        move_rows(0, n_groups, after(g))
        gu[0] = g
        u = _dot(x, w1u_ref[0, 0].astype(BF16)) + b1u_ref[0, 0]
        move_rows(1, n_groups, after(u))
        gu[1] = u
        g = jnp.minimum(gu[0], SWIGLU_LIMIT)
        u = jnp.clip(gu[1], -SWIGLU_LIMIT, SWIGLU_LIMIT)
        a = ((u + 1.0) * g / (1.0 + jnp.exp(-SWIGLU_ALPHA * g))).astype(BF16)
        w2 = w2_ref[0, 0]
        for c in range(pieces):
            cols = slice(c * MXU_COLS, (c + 1) * MXU_COLS)
            y = _dot(a, w2[:, cols].astype(BF16))
            acc[:, cols] += y
            move_rows(c + 2, n_groups, after(y))

    @pl.when(jnp.logical_not(active))
    def _():
        move_rows()

    @pl.when(f == nf - 1)
    def _():
        @pl.when(i >= 1)
        def _():
            wait_rows(stg, usem, slot)
        for j in range(sub):
            stg[slot, pl.ds(j, tm, stride=MOE_PITCH), :] = _pack_pairs(acc[:, j * LANES:(j + 1) * LANES],
                                                                      acc[:, half + j * LANES:half + (j + 1) * LANES])

    @pl.when((i == nb - 1) & (f == nf - 1))
    def _():
        wait_rows(xg, gsem, 1 - slot)
        wait_rows(stg, usem, 1 - slot)


def _moe_experts(layer, hp_tiles, slot_src, block_e, n_active, w1, b1, w2, b2):
    n_tok, sub, _ = hp_tiles.shape
    d = 2 * sub * LANES
    tm, tf = MOE_ROWS, MOE_FF_TILE
    nf = MOE_N_FF
    n_real = slot_src.shape[0] // tm
    nb = n_real + 2
    n_rows = TOP_K * n_tok + 2 * tm
    ext = jnp.full(((nb + 2) * tm,), -1, jnp.int32).at[tm:(n_real + 1) * tm].set(slot_src)
    slot = jnp.arange(ext.shape[0], dtype=jnp.int32)
    tok_ext = (jnp.maximum(ext, 0) >> 2).reshape(nb + 2, 1, tm)
    dump = TOP_K * n_tok + ((slot // tm) % 2) * tm + slot % tm
    dst_ext = jnp.where(ext < 0, dump, (ext & (TOP_K - 1)) * n_tok + (ext >> 2)).reshape(nb + 2, 1, tm)

    def blk(i, na):
        return jnp.minimum(i, na[0] - 1)

    def ff(i, f, na):
        return jnp.where(i < na[0], f, nf - 1)

    smem = lambda off: pl.BlockSpec((1, 1, tm), lambda i, f, be, na: (i + off, 0, 0), memory_space=pltpu.SMEM)
    grid_spec = pltpu.PrefetchScalarGridSpec(
        num_scalar_prefetch=2,
        grid=(nb, nf),
        in_specs=[
            smem(0), smem(1), smem(2),
            pl.BlockSpec(memory_space=pl.ANY),
            pl.BlockSpec((1, 1, d, tf), lambda i, f, be, na: (layer, be[blk(i, na)], 0, ff(i, f, na))),
            pl.BlockSpec((1, 1, d, tf), lambda i, f, be, na: (layer, be[blk(i, na)], 0, nf + ff(i, f, na))),
            pl.BlockSpec((1, 1, 1, tf), lambda i, f, be, na: (layer, be[blk(i, na)], 0, ff(i, f, na))),
            pl.BlockSpec((1, 1, 1, tf), lambda i, f, be, na: (layer, be[blk(i, na)], 0, nf + ff(i, f, na))),
            pl.BlockSpec((1, 1, tf, d), lambda i, f, be, na: (layer, be[blk(i, na)], ff(i, f, na), 0)),
            pl.BlockSpec((1, 1, 1, d), lambda i, f, be, na: (layer, be[blk(i, na)], 0, 0)),
        ],
        out_specs=pl.BlockSpec(memory_space=pl.ANY),
        scratch_shapes=[pltpu.VMEM((2, tm * MOE_PITCH, LANES), jnp.uint32), pltpu.VMEM((2, tm * MOE_PITCH, LANES), jnp.uint32),
                        pltpu.VMEM((tm, d), BF16), pltpu.VMEM((tm, d), F32), pltpu.VMEM((2, tm, tf), F32),
                        pltpu.VMEM((64, sub, LANES), jnp.uint32),
                        pltpu.SemaphoreType.DMA((2,)), pltpu.SemaphoreType.DMA((2,)), pltpu.SemaphoreType.DMA(())],
    )
    depth, ne = w1.shape[:2]
    be_ext = jnp.concatenate([block_e, jnp.full((2,), N_EXPERTS - 1, jnp.int32)])
    return pl.pallas_call(
        _ffn_kernel,
        grid_spec=grid_spec,
        out_shape=jax.ShapeDtypeStruct((n_rows, sub, LANES), jnp.uint32),
        compiler_params=_cparams(("arbitrary", "arbitrary")),
        name="moe_experts",
    )(be_ext, n_active, dst_ext, tok_ext, tok_ext, hp_tiles, w1, w1, b1.reshape(depth, ne, 1, -1), b1.reshape(depth, ne, 1, -1), w2,
      b2.reshape(depth, ne, 1, -1))


def _combine_kernel(*refs):
    y_refs, (gate_ref, x_ref, gm_ref, o_ref) = refs[:TOP_K], refs[TOP_K:]
    half = y_refs[0].shape[1]
    gates = gate_ref[0]
    lo_acc = jnp.zeros((x_ref.shape[1], half), F32)
    hi_acc = lo_acc
    for k in range(TOP_K):
        lo, hi = _unpack_pairs(y_refs[k][...])
        lo_acc = lo_acc + gates[:, k:k + 1] * lo
        hi_acc = hi_acc + gates[:, k:k + 1] * hi
    o_ref[0, :, :half] = x_ref[0, :, :half] + gm_ref[0, :, :half] * lo_acc
    o_ref[0, :, half:] = x_ref[0, :, half:] + gm_ref[0, :, half:] * hi_acc


def _moe_combine(y_tok, gates, x, gmod, tc=256):
    b, s, d = x.shape
    nt = s // tc
    plane = lambda k: pl.BlockSpec((tc, d // 2), lambda bi, i: (k * b * nt + bi * nt + i, 0))
    return pl.pallas_call(
        _combine_kernel,
        grid=(b, nt),
        in_specs=[plane(k) for k in range(TOP_K)] + [
            pl.BlockSpec((1, tc, LANES), lambda bi, i: (bi, i, 0)),
            pl.BlockSpec((1, tc, d), lambda bi, i: (bi, i, 0)),
            pl.BlockSpec((1, 1, d), lambda bi, i: (bi, 0, 0)),
        ],
        out_specs=pl.BlockSpec((1, tc, d), lambda bi, i: (bi, i, 0)),
        out_shape=jax.ShapeDtypeStruct((b, s, d), F32),
        compiler_params=_cparams(("parallel", "parallel")),
        name="moe_combine",
    )(*([y_tok] * TOP_K), gates, x, gmod.reshape(b, 1, d))


def _moe_ffn(layer, x, g, shift, scale, gmod, w_r, b_r, w1, b1, w2, b2):
    b, s, d = x.shape
    n = b * s
    hp, idx, gates, rank, counts = _moe_route(x, g, shift, scale, w_r, b_r)
    counts = counts[0, :N_EXPERTS].astype(jnp.int32)
    padded = (counts + MOE_ROWS - 1) // MOE_ROWS * MOE_ROWS
    pad_end = jnp.cumsum(padded)
    pad_start = pad_end - padded
    idx4 = idx.reshape(n, LANES)[:, :TOP_K]
    onehot = idx4[:, :, None] == jnp.arange(N_EXPERTS, dtype=jnp.int32)
    pos = jnp.sum(jnp.where(onehot, pad_start, 0), axis=-1) + rank.reshape(n, LANES)[:, :TOP_K]
    n_blocks = n * TOP_K // MOE_ROWS + N_EXPERTS
    slot_ids = jnp.arange(n * TOP_K, dtype=jnp.int32)
    slot_src = jnp.full((n_blocks * MOE_ROWS,), -1, jnp.int32).at[pos.reshape(-1)].set(slot_ids)
    block_start = jnp.arange(n_blocks, dtype=jnp.int32) * MOE_ROWS
    block_e = jnp.minimum(jnp.sum(pad_end[None, :] <= block_start[:, None], axis=1), N_EXPERTS - 1).astype(jnp.int32)
    n_active = (pad_end[-1:] // MOE_ROWS).astype(jnp.int32)
    hp_tiles = hp.reshape(n, d // 2 // LANES, LANES)
    y_tok = _moe_experts(layer, hp_tiles, slot_src, block_e, n_active, w1, b1, w2, b2)
    return _moe_combine(y_tok.reshape(y_tok.shape[0], d // 2), gates, x, gmod)


def kernel(x, c, ctx, c_ctx, ada_w, ada_b, norm_mix, norm_ffn, ev_w_in, ev_conv_w, ev_conv_b, ev_q_norm, ev_k_norm, ev_rpb, ev_a_log, ev_dt_bias, ev_d_skip, ev_gate_norm, ev_w_out, od_w_in, od_b_in, od_conv_w, od_conv_b, od_filt_w1, od_filt_b1, od_filt_w2, od_filt_b2, od_filt_w3, od_filt_b3, od_filt_w4, od_filt_freq, od_filt_bias, od_w_out, od_b_out, moe_router_w, moe_router_b, moe_w1, moe_b1, moe_w2, moe_b2):
    b, s, d = x.shape
    assert ada_w.shape[0] == 2 and ev_w_in.shape[0] == 1 and od_w_in.shape[0] == 1 and d == D_MODEL
    cc = jnp.zeros((16, d), F32).at[:b].set(c).at[b].set(c_ctx)
    mod = _ada_mod(cc, ada_w, ada_b)
    chunk = lambda m, k: m[:, k * d:(k + 1) * d]
    moe = (moe_w1, moe_b1, moe_w2, moe_b2)

    m0 = mod[0, :b]
    mc0 = jnp.broadcast_to(mod[0, b:b + 1], (b, N_MOD * d))
    w_in = ev_w_in[0]
    n_main = 2 * NA_WIDTH + SSM_D_INNER + NA_WIDTH + SSM_XBC
    kv0 = NA_WIDTH + SSM_D_INNER
    w_main = w_in[:, :n_main].astype(BF16)
    w_ctx = w_in[:, kv0:n_main].astype(BF16)
    w_dt = jnp.zeros((d, LANES), BF16).at[:, :2 * SSM_HEADS].set(w_in[:, n_main:].astype(BF16))
    zb = lambda n: jnp.zeros((n,), F32)
    g0 = norm_mix[0]
    proj = _norm_mod_matmul(x, g0, chunk(m0, 0), chunk(m0, 1), w_main, zb(n_main), BF16, 1024, 1024)
    dt_raw = _norm_mod_matmul(x, g0, chunk(m0, 0), chunk(m0, 1), w_dt, zb(LANES), F32, 1024, LANES)[..., :2 * SSM_HEADS]
    proj_c = _norm_mod_matmul(ctx, g0, chunk(mc0, 0), chunk(mc0, 1), w_ctx, zb(n_main - kv0), BF16, 256, 1024)
    dt_raw_c = _norm_mod_matmul(ctx, g0, chunk(mc0, 0), chunk(mc0, 1), w_dt, zb(LANES), F32, 256, LANES)[..., :2 * SSM_HEADS]
    attn = _attention(proj, proj_c, ev_q_norm[0], ev_k_norm[0], _attn_bias_table(ev_rpb[0], s // GRID_W))
    ssm_args = (ev_conv_w[0], ev_conv_b[0], ev_dt_bias[0], ev_a_log[0])
    x_blk_c = 2 * NA_WIDTH // GROUP_WIDTH
    states = _ssd(proj_c, x_blk_c, None, dt_raw_c, *ssm_args, None, None, None)
    x_blk = (kv0 + 2 * NA_WIDTH) // GROUP_WIDTH
    z_blk = NA_WIDTH // GROUP_WIDTH
    y_ssm = _ssd(proj, x_blk, z_blk, dt_raw, *ssm_args, ev_d_skip[0], ev_gate_norm[0], states)
    x = _proj_residual([attn, y_ssm], ev_w_out[0].astype(BF16), zb(d), chunk(m0, 2), x)
    x = _moe_ffn(0, x, norm_ffn[0], chunk(m0, 3), chunk(m0, 4), chunk(m0, 5), moe_router_w[0], moe_router_b[0], *moe)

    m1 = mod[1, :b]
    proj_h = _norm_mod_matmul(x, norm_mix[1], chunk(m1, 0), chunk(m1, 1), od_w_in[0].astype(BF16), od_b_in[0], BF16, 1024, 1024)
    filt = _hyena_filters(s, od_filt_w1[0], od_filt_b1[0], od_filt_w2[0], od_filt_b2[0], od_filt_w3[0], od_filt_b3[0],
                          od_filt_w4[0], od_filt_freq[0])
    fwd, inv = _dft_matrices(s)
    coef = _filter_spectrum(filt, fwd)
    x0, u = _hyena_conv(proj_h, od_conv_w[0], od_conv_b[0])
    spec = _dft_forward(u, fwd, coef)
    y_h = _dft_inverse(spec.reshape(b, 2 * s, d), inv, x0, u, od_filt_bias[0])
    x = _proj_residual([y_h], od_w_out[0].astype(BF16), od_b_out[0], chunk(m1, 2), x)
    x = _moe_ffn(1, x, norm_ffn[1], chunk(m1, 3), chunk(m1, 4), chunk(m1, 5), moe_router_w[1], moe_router_b[1], *moe)
    return x
```

```python
import functools
import math

import jax
import jax.numpy as jnp
import numpy as np
from jax import lax
from jax.experimental import pallas as pl
from jax.experimental.pallas import tpu as pltpu

F32 = jnp.float32
BF16 = jnp.bfloat16
HI = lax.Precision.HIGHEST

D_MODEL = 2048
N_MOD = 6
EPS = 1e-6
NEG_INF = -1e9
GRID_W = 64
NA_HEADS = 16
NA_HEAD_DIM = 128
NA_WIDTH = NA_HEADS * NA_HEAD_DIM
NA_KH = 8
NA_KW = 16
SSM_D_INNER = 2048
SSM_HEAD_DIM = 64
SSM_HEADS = SSM_D_INNER // SSM_HEAD_DIM
SSM_GROUPS = 8
SSM_STATE = 128
SSM_CHUNK = 128
SSM_XBC = SSM_D_INNER + 2 * SSM_GROUPS * SSM_STATE
HEADS_PER_GROUP = SSM_HEADS // SSM_GROUPS
GROUP_WIDTH = HEADS_PER_GROUP * SSM_HEAD_DIM
HY_EMB = 33
HY_FILTER_WIDTH = 64
HY_TARGET = 1e-2
HY_FAST_PCT = 0.3
HY_SLOW_PCT = 1.5
N_EXPERTS = 32
TOP_K = 4
MOE_FF = 2048
SWIGLU_LIMIT = 7.0
SWIGLU_ALPHA = 1.702
MOE_ROWS = 512
MOE_FF_TILE = 512
MOE_N_FF = MOE_FF // MOE_FF_TILE
LANES = 128
MXU_COLS = 256
VMEM_LIMIT = 56 * 1024 * 1024


def _cparams(sem, vmem=VMEM_LIMIT):
    return pltpu.CompilerParams(dimension_semantics=sem, vmem_limit_bytes=vmem)


def _dot(a, b):
    return jnp.dot(a, b, preferred_element_type=F32)


def _dot_nt(a, b):
    return lax.dot_general(a, b, (((1,), (1,)), ((), ())), preferred_element_type=F32)


def _dot_tn(a, b):
    return lax.dot_general(a, b, (((0,), (0,)), ((), ())), preferred_element_type=F32)


def _silu(x):
    return x / (1.0 + jnp.exp(-x))


def _softplus(x):
    return jnp.maximum(x, 0.0) + jnp.log1p(jnp.exp(-jnp.abs(x)))


def _ada_kernel(c_ref, w_ref, b_ref, o_ref):
    sc = _silu(c_ref[...])
    hi = sc.astype(BF16)
    lo = (sc - hi.astype(F32)).astype(BF16)
    w = w_ref[0].astype(BF16)
    o_ref[0] = _dot(hi, w) + _dot(lo, w) + b_ref[0]


def _ada_mod(cc, ada_w, ada_b):
    depth, d, n = ada_w.shape
    tn = 1024
    return pl.pallas_call(
        _ada_kernel,
        grid=(depth, n // tn),
        in_specs=[
            pl.BlockSpec((cc.shape[0], d), lambda i, j: (0, 0)),
            pl.BlockSpec((1, d, tn), lambda i, j: (i, 0, j)),
            pl.BlockSpec((1, 1, tn), lambda i, j: (i, 0, j)),
        ],
        out_specs=pl.BlockSpec((1, cc.shape[0], tn), lambda i, j: (i, 0, j)),
        out_shape=jax.ShapeDtypeStruct((depth, cc.shape[0], n), F32),
        compiler_params=_cparams(("parallel", "parallel")),
        name="ada_mod",
    )(cc, ada_w, ada_b.reshape(depth, 1, n))


def _nmm_kernel(x_ref, g_ref, sh_ref, sc_ref, w_ref, b_ref, o_ref, h_scr):
    @pl.when(pl.program_id(2) == 0)
    def _():
        x = x_ref[0]
        ms = jnp.mean(x * x, axis=-1, keepdims=True)
        xn = x * lax.rsqrt(ms + EPS) * g_ref[...]
        h_scr[...] = (xn * (1.0 + sc_ref[0]) + sh_ref[0]).astype(BF16)

    o_ref[0] = (_dot(h_scr[...], w_ref[...]) + b_ref[...]).astype(o_ref.dtype)


def _norm_mod_matmul(x, g, shift, scale, w, bias, out_dtype, tm, tn):
    b, s, d = x.shape
    n = w.shape[1]
    tm = min(tm, s)
    return pl.pallas_call(
        _nmm_kernel,
        grid=(b, s // tm, n // tn),
        in_specs=[
            pl.BlockSpec((1, tm, d), lambda bi, i, j: (bi, i, 0)),
            pl.BlockSpec((1, d), lambda bi, i, j: (0, 0)),
            pl.BlockSpec((1, 1, d), lambda bi, i, j: (bi, 0, 0)),
            pl.BlockSpec((1, 1, d), lambda bi, i, j: (bi, 0, 0)),
            pl.BlockSpec((d, tn), lambda bi, i, j: (0, j)),
            pl.BlockSpec((1, tn), lambda bi, i, j: (0, j)),
        ],
        out_specs=pl.BlockSpec((1, tm, tn), lambda bi, i, j: (bi, i, j)),
        out_shape=jax.ShapeDtypeStruct((b, s, n), out_dtype),
        scratch_shapes=[pltpu.VMEM((tm, d), BF16)],
        compiler_params=_cparams(("parallel", "parallel", "arbitrary")),
        name="norm_mod_matmul",
    )(x, g.reshape(1, d), shift.reshape(b, 1, d), scale.reshape(b, 1, d), w, bias.reshape(1, n))


NA_PAIR_ROWS = NA_KH + 2
NA_VARIANTS = 5


def _pair_window_start(i, rows):
    return np.clip(2 * i - NA_KH // 2, 0, rows - NA_PAIR_ROWS)


def _attn_bias_table(rpb, rows):
    nh = rpb.shape[0]
    kc = np.arange(GRID_W)[:, None]
    qc = np.arange(GRID_W)[None, :]
    col_off = np.clip(kc - qc, -(NA_KW - 1), NA_KW - 1) + NA_KW - 1
    onehot = (col_off[None] == np.arange(2 * NA_KW - 1)[:, None, None]).astype(np.float32)
    toep = jnp.einsum('hrc,ckq->hrkq', rpb.astype(F32), onehot, precision=HI)
    ws_col = np.clip(qc - NA_KW // 2, 0, GRID_W - NA_KW)
    toep = jnp.where(((kc >= ws_col) & (kc < ws_col + NA_KW))[None, None], toep, NEG_INF)
    dead = jnp.full((nh, GRID_W, GRID_W), NEG_INF, F32)
    rep_pair = [0, 1, 2, rows // 2 - 2, rows // 2 - 1]
    blocks = []
    for i in rep_pair:
        ws = int(_pair_window_start(i, rows))
        for t in range(NA_PAIR_ROWS):
            for e in range(2):
                r = 2 * i + e
                rs = int(np.clip(r - NA_KH // 2, 0, rows - NA_KH))
                kr = ws + t
                blocks.append(toep[:, kr - r + NA_KH - 1] if rs <= kr < rs + NA_KH else dead)
    bias = jnp.stack(blocks, axis=1).reshape(nh, NA_VARIANTS, NA_PAIR_ROWS, 2, GRID_W, GRID_W)
    return bias.transpose(0, 1, 2, 4, 3, 5).reshape(nh, NA_VARIANTS, NA_PAIR_ROWS * GRID_W, 2 * GRID_W)


def _head_rmsnorm(t, g):
    sq = t * t
    hi = sq.astype(BF16)
    lo = (sq - hi.astype(F32)).astype(BF16)
    avg = jnp.full((NA_HEAD_DIM, NA_HEAD_DIM), 1.0 / NA_HEAD_DIM, BF16)
    ms = _dot(hi, avg) + _dot(lo, avg)
    return t * lax.rsqrt(ms + EPS) * g


def _attn_kernel(q_ref, k_ref, v_ref, kc_ref, vc_ref, qg_ref, kg_ref, bias_ref, o_ref, qs, ks, kcs, vt, vct):
    seq, hd = q_ref.shape[1], q_ref.shape[2]
    rows = seq // GRID_W
    pair = 2 * GRID_W
    win_blocks = NA_PAIR_ROWS // 2
    ctx_blocks = kc_ref.shape[1] // pair
    qs[...] = (_head_rmsnorm(q_ref[0].astype(F32), qg_ref[...]) * hd ** -0.5).astype(BF16)
    ks[...] = _head_rmsnorm(k_ref[0].astype(F32), kg_ref[...]).astype(BF16)
    kcs[...] = _head_rmsnorm(kc_ref[0].astype(F32), kg_ref[...]).astype(BF16)
    for blk in range(seq // pair):
        vt[blk] = v_ref[0, blk * pair:(blk + 1) * pair, :].astype(F32).T.astype(BF16)
    for blk in range(ctx_blocks):
        vct[blk] = vc_ref[0, blk * pair:(blk + 1) * pair, :].astype(F32).T.astype(BF16)

    def body(i, carry):
        ws = jnp.clip(2 * i - NA_KH // 2, 0, rows - NA_PAIR_ROWS)
        var = (2 * i - ws) // 2
        wb = ws // 2
        q0 = pl.multiple_of(i * pair, pair)
        k0 = pl.multiple_of(ws * GRID_W, pair)
        q_p = qs[pl.ds(q0, pair), :]
        s_lat = _dot_nt(ks[pl.ds(k0, win_blocks * pair), :], q_p) + bias_ref[0, var]
        s_ctx = _dot_nt(kcs[...], q_p)
        m = jnp.maximum(jnp.max(s_lat, axis=0, keepdims=True), jnp.max(s_ctx, axis=0, keepdims=True))
        p_lat = jnp.exp(s_lat - m)
        p_ctx = jnp.exp(s_ctx - m)
        den = jnp.sum(p_lat, axis=0, keepdims=True) + jnp.sum(p_ctx, axis=0, keepdims=True)
        p_lat = p_lat.astype(BF16)
        p_ctx = p_ctx.astype(BF16)
        o_t = jnp.zeros((hd, pair), F32)
        for blk in range(win_blocks):
            o_t = o_t + _dot(vt[wb + blk], p_lat[blk * pair:(blk + 1) * pair, :])
        for blk in range(ctx_blocks):
            o_t = o_t + _dot(vct[blk], p_ctx[blk * pair:(blk + 1) * pair, :])
        o_ref[0, pl.ds(q0, pair), :] = (o_t / den).T.astype(o_ref.dtype)
        return carry

    lax.fori_loop(0, rows // 2, body, 0, unroll=2)


def _attention(proj, proj_c, q_norm, k_norm, bias_tab):
    b, s, _ = proj.shape
    ctx = proj_c.shape[1]
    hd = NA_HEAD_DIM
    nh = NA_HEADS
    return pl.pallas_call(
        _attn_kernel,
        grid=(b, nh),
        in_specs=[
            pl.BlockSpec((1, s, hd), lambda bi, h: (bi, 0, h)),
            pl.BlockSpec((1, s, hd), lambda bi, h: (bi, 0, 2 * nh + h)),
            pl.BlockSpec((1, s, hd), lambda bi, h: (bi, 0, 3 * nh + h)),
            pl.BlockSpec((1, ctx, hd), lambda bi, h: (bi, 0, h)),
            pl.BlockSpec((1, ctx, hd), lambda bi, h: (bi, 0, nh + h)),
            pl.BlockSpec((1, hd), lambda bi, h: (0, 0)),
            pl.BlockSpec((1, hd), lambda bi, h: (0, 0)),
            pl.BlockSpec((1,) + bias_tab.shape[1:], lambda bi, h: (h, 0, 0, 0)),
        ],
        out_specs=pl.BlockSpec((1, s, hd), lambda bi, h: (bi, 0, h)),
        out_shape=jax.ShapeDtypeStruct((b, s, nh * hd), BF16),
        scratch_shapes=[pltpu.VMEM((s, hd), BF16), pltpu.VMEM((s, hd), BF16), pltpu.VMEM((ctx, hd), BF16),
                        pltpu.VMEM((s // (2 * GRID_W), hd, 2 * GRID_W), BF16), pltpu.VMEM((ctx // (2 * GRID_W), hd, 2 * GRID_W), BF16)],
        compiler_params=_cparams(("parallel", "parallel")),
        name="nbr_attention",
    )(proj, proj, proj, proj_c, proj_c, q_norm.reshape(1, hd), k_norm.reshape(1, hd), bias_tab)


def _split3(v):
    hi = v.astype(BF16)
    r1 = v - hi.astype(F32)
    mid = r1.astype(BF16)
    lo = (r1 - mid.astype(F32)).astype(BF16)
    return hi, mid, lo


def _ssd_kernel(*refs, with_y, nc):
    (xr_ref, br_ref, cr_ref, cwx, cbx, cwb, cbb, cwc, cbc, dt_ref, bias_ref, a_ref), rest = refs[:12], refs[12:]
    if with_y:
        (z_ref, dsk_ref, gn_ref, sf0_ref, sb0_ref, y_ref, xs, bs, cs, contrib, dec, wrow, xf, sfa, sba, drow, dtrow, dsplit, esplit) = rest
    else:
        sf_out, sb_out, xs, bs, cs, contrib, dec, wrow = rest
    ck = SSM_CHUNK
    length = nc * ck
    nh = HEADS_PER_GROUP
    nhd = 2 * nh
    gw = GROUP_WIDTH
    dot_hi = functools.partial(jnp.dot, precision=HI, preferred_element_type=F32)

    halo = 16
    win = ck + 2 * halo
    conv_w = jnp.concatenate([cwx[...], cwb[...], cwc[...]], axis=1)
    conv_b = jnp.concatenate([cbx[...], cbb[...], cbc[...]], axis=1)
    wi = lax.broadcasted_iota(jnp.int32, (ck, win), 1) - lax.broadcasted_iota(jnp.int32, (ck, win), 0)

    def conv_silu_chunk(c):
        l0 = pl.multiple_of(c * ck, ck)
        s0 = pl.multiple_of(jnp.clip(l0 - halo, 0, length - win), halo)
        rel = wi + (s0 - l0)
        window = jnp.concatenate([r[0, pl.ds(s0, win), :] for r in (xr_ref, br_ref, cr_ref)], axis=1)
        cur = jnp.concatenate([r[0, pl.ds(l0, ck), :] for r in (xr_ref, br_ref, cr_ref)], axis=1).astype(F32)
        prev = _dot((rel == -1).astype(BF16), window)
        nxt = _dot((rel == 1).astype(BF16), window)
        y = _silu(conv_b + prev * conv_w[0:1, :] + cur * conv_w[1:2, :] + nxt * conv_w[2:3, :])
        xs[pl.ds(l0, ck), :] = y[:, :gw].astype(BF16)
        if with_y:
            xf[pl.ds(l0, ck), :] = y[:, :gw]
        bs[pl.ds(l0, ck), :] = y[:, gw:gw + SSM_STATE].astype(BF16)
        cs[pl.ds(l0, ck), :] = y[:, gw + SSM_STATE:].astype(BF16)
        return y[:, :gw], y[:, gw:gw + SSM_STATE].astype(BF16)

    ri = lax.broadcasted_iota(jnp.int32, (ck, ck), 0)
    ci = lax.broadcasted_iota(jnp.int32, (ck, ck), 1)
    lower = ci <= ri
    upper = ci >= ri
    eye = (ci == ri).astype(BF16)
    dt_row = _softplus(dt_ref[0, 0] + bias_ref[0])
    a_row = dt_row * a_ref[0]
    cum_f = dot_hi(a_row, upper.astype(F32))
    cum_b = dot_hi(a_row, lower.astype(F32))
    tot = dot_hi(a_row, jnp.ones((ck, ck), F32))
    d_row = jnp.where((ri % nhd) < nh, cum_f, cum_b)
    wrow[...] = jnp.exp(tot - d_row) * dt_row
    texp = jnp.exp(tot)
    row4 = lax.broadcasted_iota(jnp.int32, (ck, 2 * gw), 0)
    lane4 = lax.broadcasted_iota(jnp.int32, (ck, 2 * gw), 1) // SSM_HEAD_DIM
    tmask = jnp.where(row4 % nhd == lane4, jnp.concatenate([texp] * (2 * gw // ck), axis=1), 0.0)
    pick = (lax.broadcasted_iota(jnp.int32, (dec.shape[0], ck), 1) // nhd
            == lax.broadcasted_iota(jnp.int32, (dec.shape[0], ck), 0)).astype(F32)
    dec[...] = dot_hi(pick, tmask)
    if with_y:
        drow[...] = d_row
        dtrow[...] = dt_row
        for k, part in enumerate(_split3(d_row)):
            dsplit[k] = part.astype(F32)
        for k, part in enumerate(_split3(jnp.exp(d_row))[:2]):
            esplit[k] = part.astype(F32)

    def spread(src, r0, width):
        rows = jnp.concatenate([jnp.broadcast_to(src[pl.ds(r0 + hd, 1), :], (width, ck)) for hd in range(nhd)], axis=0)
        return _dot_nt(eye, rows.astype(BF16))

    def chunk_states(c, carry):
        xc, b_c = conv_silu_chunk(c)
        wx = spread(wrow, c * nhd, SSM_HEAD_DIM)
        xw = jnp.concatenate([(xc * wx[:, :gw]).astype(BF16), (xc * wx[:, gw:]).astype(BF16)], axis=1)
        contrib[c] = _dot_tn(b_c, xw)
        return carry

    lax.fori_loop(0, nc, chunk_states, 0, unroll=2)

    def fwd_chain(c, s):
        if with_y:
            sfa[c] = s
        return s * dec[pl.ds(c, 1), :gw] + contrib[c, :, :gw]

    def bwd_chain(i, s):
        c = nc - 1 - i
        if with_y:
            sba[c] = s
        return s * dec[pl.ds(c, 1), gw:] + contrib[c, :, gw:]

    if with_y:
        s_f0 = sf0_ref[0, 0]
        s_b0 = sb0_ref[0, 0]
    else:
        s_f0 = jnp.zeros((SSM_STATE, gw), F32)
        s_b0 = s_f0
    s_f = lax.fori_loop(0, nc, fwd_chain, s_f0)
    s_b = lax.fori_loop(0, nc, bwd_chain, s_b0)
    if not with_y:
        sf_out[0, 0] = s_f
        sb_out[0, 0] = s_b
        return

    lane_head = lax.broadcasted_iota(jnp.int32, (ck, gw), 1) // SSM_HEAD_DIM

    def out_step(c, carry):
        l0 = pl.multiple_of(c * ck, ck)
        r0 = c * nhd
        d_b = spread(dsplit.at[0], r0, ck) + spread(dsplit.at[1], r0, ck) + spread(dsplit.at[2], r0, ck)
        e_b = spread(esplit.at[0], r0, SSM_HEAD_DIM) + spread(esplit.at[1], r0, SSM_HEAD_DIM)
        b_c = bs[pl.ds(l0, ck), :]
        c_c = cs[pl.ds(l0, ck), :]
        x_c = xs[pl.ds(l0, ck), :]
        g = _dot_nt(c_c, b_c)
        acc = jnp.zeros((ck, gw), F32)
        for j in range(nh):
            jb = nh + j
            lf = jnp.exp(jnp.where(lower, d_b[:, j * ck:(j + 1) * ck] - drow[pl.ds(r0 + j, 1), :], -1e30)) * dtrow[pl.ds(r0 + j, 1), :]
            lb = jnp.exp(jnp.where(upper, d_b[:, jb * ck:(jb + 1) * ck] - drow[pl.ds(r0 + jb, 1), :], -1e30)) * dtrow[pl.ds(r0 + jb, 1), :]
            m = (g * (lf + lb)).astype(BF16)
            acc = acc + _dot(m, jnp.where(lane_head == j, x_c, jnp.zeros_like(x_c)))
        acc = acc + _dot(c_c, sfa[c].astype(BF16)) * e_b[:, :gw]
        acc = acc + _dot(c_c, sba[c].astype(BF16)) * e_b[:, gw:]
        y = acc + xf[pl.ds(l0, ck), :] * dsk_ref[...]
        y = y * _silu(z_ref[0, pl.ds(l0, ck), :].astype(F32))
        ms = jnp.mean(y * y, axis=-1, keepdims=True)
        y_ref[0, pl.ds(l0, ck), :] = (y * lax.rsqrt(ms + EPS) * gn_ref[...]).astype(y_ref.dtype)
        return carry

    lax.fori_loop(0, nc, out_step, 0, unroll=2)


def _ssd(proj, x_blk0, z_blk0, dt_raw, conv_w, conv_b, dt_bias, a_log, d_skip, gate_norm, init):
    b, length, _ = proj.shape
    nc = length // SSM_CHUNK
    ng, nh, gw, st = SSM_GROUPS, HEADS_PER_GROUP, GROUP_WIDTH, SSM_STATE
    with_y = init is not None
    b_blk0 = x_blk0 * (gw // st) + SSM_D_INNER // st
    c_blk0 = b_blk0 + ng
    ck = SSM_CHUNK
    nhd = 2 * nh
    assert nc * nhd <= ck
    dtg = dt_raw.reshape(b, nc, ck, 2, ng, nh).transpose(0, 4, 1, 3, 5, 2).reshape(b, ng, nc * nhd, ck)
    dtg = jnp.pad(dtg, ((0, 0), (0, 0), (0, ck - nc * nhd), (0, 0)))
    per_row = lambda t: jnp.tile(t.reshape(2, ng, nh).transpose(1, 0, 2).reshape(ng, nhd), (1, ck // nhd)).reshape(ng, ck, 1)
    bias_r = per_row(dt_bias)
    a_r = per_row(-jnp.exp(a_log.astype(F32)))
    cw_x, cw_b, cw_c = conv_w[:, :SSM_D_INNER], conv_w[:, SSM_D_INNER:SSM_D_INNER + ng * st], conv_w[:, SSM_D_INNER + ng * st:]
    cb = conv_b.reshape(1, -1)
    cb_x, cb_b, cb_c = cb[:, :SSM_D_INNER], cb[:, SSM_D_INNER:SSM_D_INNER + ng * st], cb[:, SSM_D_INNER + ng * st:]
    in_specs = [
        pl.BlockSpec((1, length, gw), lambda bi, g: (bi, 0, x_blk0 + g)),
        pl.BlockSpec((1, length, st), lambda bi, g: (bi, 0, b_blk0 + g)),
        pl.BlockSpec((1, length, st), lambda bi, g: (bi, 0, c_blk0 + g)),
        pl.BlockSpec((3, gw), lambda bi, g: (0, g)),
        pl.BlockSpec((1, gw), lambda bi, g: (0, g)),
        pl.BlockSpec((3, st), lambda bi, g: (0, g)),
        pl.BlockSpec((1, st), lambda bi, g: (0, g)),
        pl.BlockSpec((3, st), lambda bi, g: (0, g)),
        pl.BlockSpec((1, st), lambda bi, g: (0, g)),
        pl.BlockSpec((1, 1, ck, ck), lambda bi, g: (bi, g, 0, 0)),
        pl.BlockSpec((1, ck, 1), lambda bi, g: (g, 0, 0)),
        pl.BlockSpec((1, ck, 1), lambda bi, g: (g, 0, 0)),
    ]
    args = [proj, proj, proj, cw_x, cb_x, cw_b, cb_b, cw_c, cb_c, dtg, bias_r, a_r]
    scratch = [pltpu.VMEM((length, gw), BF16), pltpu.VMEM((length, st), BF16), pltpu.VMEM((length, st), BF16),
               pltpu.VMEM((nc, st, 2 * gw), F32), pltpu.VMEM((ck // nhd, 2 * gw), F32), pltpu.VMEM((ck, ck), F32)]
    state_spec = pl.BlockSpec((1, 1, st, gw), lambda bi, g: (bi, g, 0, 0))
    state_shape = jax.ShapeDtypeStruct((b, ng, st, gw), F32)
    if with_y:
        in_specs += [
            pl.BlockSpec((1, length, gw), lambda bi, g: (bi, 0, z_blk0 + g)),
            pl.BlockSpec((1, gw), lambda bi, g: (0, g)),
            pl.BlockSpec((1, gw), lambda bi, g: (0, g)),
            state_spec, state_spec,
        ]
        args += [proj, jnp.repeat(d_skip, SSM_HEAD_DIM).reshape(1, -1), gate_norm.reshape(1, -1), init[0], init[1]]
        out_specs = pl.BlockSpec((1, length, gw), lambda bi, g: (bi, 0, g))
        out_shape = jax.ShapeDtypeStruct((b, length, SSM_D_INNER), BF16)
        scratch += [pltpu.VMEM((length, gw), F32), pltpu.VMEM((nc, st, gw), F32), pltpu.VMEM((nc, st, gw), F32),
                    pltpu.VMEM((ck, ck), F32), pltpu.VMEM((ck, ck), F32), pltpu.VMEM((3, ck, ck), F32), pltpu.VMEM((2, ck, ck), F32)]
    else:
        out_specs = [state_spec, state_spec]
        out_shape = [state_shape, state_shape]
    return pl.pallas_call(
        functools.partial(_ssd_kernel, with_y=with_y, nc=nc),
        grid=(b, ng),
        in_specs=in_specs,
        out_specs=out_specs,
        out_shape=out_shape,
        scratch_shapes=scratch,
        compiler_params=_cparams(("parallel", "parallel")),
        name="ssd_main" if with_y else "ssd_ctx",
    )(*args)


def _proj_res_kernel(*refs, n_lhs):
    a_refs, w_refs = refs[:n_lhs], refs[n_lhs:2 * n_lhs]
    b_ref, gate_ref, res_ref, o_ref = refs[2 * n_lhs:]
    acc = _dot(a_refs[0][0], w_refs[0][...])
    for a_ref, w_ref in zip(a_refs[1:], w_refs[1:]):
        acc = acc + _dot(a_ref[0], w_ref[...])
    o_ref[0] = res_ref[0] + gate_ref[0] * (acc + b_ref[...])


def _proj_residual(lhs, w, bias, gate, resid, tm=1024, tn=1024):
    b, s, k = lhs[0].shape
    n = w.shape[1]
    n_lhs = len(lhs)
    in_specs = [pl.BlockSpec((1, tm, k), lambda bi, i, j: (bi, i, 0)) for _ in lhs]
    in_specs += [pl.BlockSpec((k, tn), functools.partial(lambda bi, i, j, t: (t, j), t=t)) for t in range(n_lhs)]
    in_specs += [
        pl.BlockSpec((1, tn), lambda bi, i, j: (0, j)),
        pl.BlockSpec((1, 1, tn), lambda bi, i, j: (bi, 0, j)),
        pl.BlockSpec((1, tm, tn), lambda bi, i, j: (bi, i, j)),
    ]
    return pl.pallas_call(
        functools.partial(_proj_res_kernel, n_lhs=n_lhs),
        grid=(b, s // tm, n // tn),
        in_specs=in_specs,
        out_specs=pl.BlockSpec((1, tm, tn), lambda bi, i, j: (bi, i, j)),
        out_shape=jax.ShapeDtypeStruct((b, s, n), F32),
        compiler_params=_cparams(("parallel", "parallel", "parallel")),
        name="proj_residual",
    )(*lhs, *([w] * n_lhs), bias.reshape(1, n), gate.reshape(b, 1, n), resid)


def _hy_filter_kernel(tw_ref, f_ref, w1t, w1c, w1s, b1, w2, b2, w3, b3, fr, w4_ref, dl_ref, o_ref, h_scr):
    dot_hi = lambda a, b: jnp.dot(a, b, precision=HI, preferred_element_type=F32)

    @pl.when(pl.program_id(0) == 0)
    def _():
        t = tw_ref[:, 0:1]
        ang = tw_ref[:, 1:2] * f_ref[...]
        pre = t * w1t[...] + dot_hi(jnp.cos(ang), w1c[...]) + dot_hi(-jnp.sin(ang), w1s[...]) + b1[...]
        h = jnp.sin(fr[...] * pre)
        h = jnp.sin(fr[...] * (dot_hi(h, w2[...]) + b2[...]))
        h_scr[...] = jnp.sin(fr[...] * (dot_hi(h, w3[...]) + b3[...]))

    o_ref[...] = dot_hi(h_scr[...], w4_ref[...]) * jnp.exp(-tw_ref[:, 0:1] * dl_ref[...])


def _hyena_filters(length, w1, b1, w2, b2, w3, b3, w4, freq):
    bands = (HY_EMB - 1) // 2
    fw = HY_FILTER_WIDTH
    t = jnp.linspace(0.0, 1.0, length, dtype=F32)
    w = 2 * math.pi * jnp.arange(length, dtype=F32) / length
    tw = jnp.stack([t, w], axis=1)
    f = jnp.linspace(1e-4, bands - 1, bands, dtype=F32).reshape(1, bands)
    min_decay = math.log(HY_TARGET) / HY_SLOW_PCT
    max_decay = math.log(HY_TARGET) / HY_FAST_PCT
    deltas = jnp.abs(jnp.linspace(min_decay, max_decay, D_MODEL, dtype=F32))
    dl = jnp.concatenate([deltas, deltas]).reshape(1, 2 * D_MODEL)
    tn = 1024
    small = lambda shape: pl.BlockSpec(shape, lambda j: (0, 0))
    row = lambda v: v.reshape(1, -1)
    return pl.pallas_call(
        _hy_filter_kernel,
        grid=(2 * D_MODEL // tn,),
        in_specs=[small((length, 2)), small((1, bands)), small((1, fw)), small((bands, fw)), small((bands, fw)), small((1, fw)),
                  small((fw, fw)), small((1, fw)), small((fw, fw)), small((1, fw)), small((1, fw)),
                  pl.BlockSpec((fw, tn), lambda j: (0, j)), pl.BlockSpec((1, tn), lambda j: (0, j))],
        out_specs=pl.BlockSpec((length, tn), lambda j: (0, j)),
        out_shape=jax.ShapeDtypeStruct((length, 2 * D_MODEL), F32),
        scratch_shapes=[pltpu.VMEM((length, fw), F32)],
        compiler_params=_cparams(("arbitrary",)),
        name="hyena_filters",
    )(tw, f, w1[0:1], w1[1:1 + bands], w1[1 + bands:], row(b1), w2, row(b2), w3, row(b3), row(freq), w4, dl)


def _hy_conv_kernel(x0_ref, x1_ref, v_ref, w0, b0, w1, b1, w2, b2, x0_out, u_out):
    length = x0_ref.shape[1]

    def conv(raw_ref, w_ref, b_ref):
        x = raw_ref[0].astype(F32)
        rows = lax.broadcasted_iota(jnp.int32, x.shape, 0)
        xm1 = jnp.where(rows == 0, 0.0, pltpu.roll(x, 1, 0))
        xp1 = jnp.where(rows == length - 1, 0.0, pltpu.roll(x, length - 1, 0))
        return b_ref[...] + xm1 * w_ref[0:1, :] + x * w_ref[1:2, :] + xp1 * w_ref[2:3, :]

    x0_out[0] = conv(x0_ref, w0, b0).astype(x0_out.dtype)
    u_out[0] = (conv(v_ref, w2, b2) * conv(x1_ref, w1, b1)).astype(u_out.dtype)


def _hyena_conv(proj, conv_w, conv_b):
    b, length, _ = proj.shape
    d = D_MODEL
    tn = 512
    nb = d // tn
    cb = conv_b.reshape(1, -1)
    seg = lambda k: pl.BlockSpec((1, length, tn), lambda bi, j: (bi, 0, k * nb + j))
    wseg = lambda k: pl.BlockSpec((3, tn), lambda bi, j: (0, k * nb + j))
    bseg = lambda k: pl.BlockSpec((1, tn), lambda bi, j: (0, k * nb + j))
    out = pl.BlockSpec((1, length, tn), lambda bi, j: (bi, 0, j))
    return pl.pallas_call(
        _hy_conv_kernel,
        grid=(b, nb),
        in_specs=[seg(0), seg(1), seg(2), wseg(0), bseg(0), wseg(1), bseg(1), wseg(2), bseg(2)],
        out_specs=[out, out],
        out_shape=[jax.ShapeDtypeStruct((b, length, d), BF16)] * 2,
        compiler_params=_cparams(("parallel", "parallel")),
        name="hyena_conv",
    )(proj, proj, proj, conv_w, cb, conv_w, cb, conv_w, cb)


def _dft_matrices(length):
    n = 2 * length
    f = lax.broadcasted_iota(jnp.int32, (length, length), 0)
    t = lax.broadcasted_iota(jnp.int32, (length, length), 1)
    ang = ((f * t) % n).astype(F32) * (2 * math.pi / n)
    sign = (1 - 2 * (t % 2)).astype(F32)
    fwd_c = jnp.cos(ang)
    fwd_s = jnp.where(f == 0, sign, -jnp.sin(ang))
    fwd = jnp.concatenate([fwd_c, fwd_s], axis=0).astype(BF16)
    wgt = jnp.where(f == 0, 1.0, 2.0) / n
    inv = jnp.concatenate([(fwd_c * wgt).T, (fwd_s * wgt).T], axis=1).astype(BF16)
    return fwd, inv


def _mm_kernel(a_ref, b_ref, o_ref):
    o_ref[...] = _dot(a_ref[...], b_ref[...]).astype(o_ref.dtype)


def _matmul(a, b, out_dtype=F32, tm=1024, tn=1024):
    m, k = a.shape
    n = b.shape[1]
    return pl.pallas_call(
        _mm_kernel,
        grid=(n // tn, m // tm),
        in_specs=[pl.BlockSpec((tm, k), lambda j, i: (i, 0)), pl.BlockSpec((k, tn), lambda j, i: (0, j))],
        out_specs=pl.BlockSpec((tm, tn), lambda j, i: (i, j)),
        out_shape=jax.ShapeDtypeStruct((m, n), out_dtype),
        compiler_params=_cparams(("parallel", "parallel")),
        name="matmul",
    )(a, b)


def _filter_spectrum(kk, fwd):
    length = kk.shape[0]
    d = D_MODEL
    k_f, k_b = kk[:, :d], kk[:, d:]
    k_fp = k_f.at[0].add(k_b[0])
    k_bp = k_b.at[0].set(0.0)
    spec = _matmul(fwd, jnp.concatenate([k_fp, k_bp], axis=1).astype(BF16))
    a_re, a_im, b_re, b_im = spec[:length, :d], spec[length:, :d], spec[:length, d:], spec[length:, d:]
    nz = (jnp.arange(length) != 0)[:, None]
    k_re = a_re + b_re
    k_im = jnp.where(nz, a_im - b_im, a_im + b_im)
    return jnp.stack([k_re, jnp.where(nz, k_im, 0.0), jnp.where(nz, k_re, k_im)])


def _dft_fwd_kernel(fc_ref, fs_ref, u_ref, k_ref, y_ref):
    u = u_ref[0]
    re = _dot(fc_ref[...], u)
    im = _dot(fs_ref[...], u)
    y_ref[0, 0] = (re * k_ref[0] - im * k_ref[1]).astype(y_ref.dtype)
    y_ref[0, 1] = (re * k_ref[1] + im * k_ref[2]).astype(y_ref.dtype)


def _dft_forward(u, fwd, coef, tm=1024, tn=512):
    b, length, d = u.shape
    ni = length // tm
    return pl.pallas_call(
        _dft_fwd_kernel,
        grid=(d // tn, ni, b),
        in_specs=[
            pl.BlockSpec((tm, length), lambda j, i, bi: (i, 0)),
            pl.BlockSpec((tm, length), lambda j, i, bi: (ni + i, 0)),
            pl.BlockSpec((1, length, tn), lambda j, i, bi: (bi, 0, j)),
            pl.BlockSpec((3, tm, tn), lambda j, i, bi: (0, i, j)),
        ],
        out_specs=pl.BlockSpec((1, 2, tm, tn), lambda j, i, bi: (bi, 0, i, j)),
        out_shape=jax.ShapeDtypeStruct((b, 2, length, d), BF16),
        compiler_params=_cparams(("parallel", "parallel", "parallel")),
        name="dft_forward",
    )(fwd, fwd, u, coef)


def _dft_inv_kernel(g_ref, y_ref, x0_ref, u_ref, fb_ref, o_ref):
    conv = _dot(g_ref[...], y_ref[0])
    o_ref[0] = (x0_ref[0].astype(F32) * (conv + u_ref[0].astype(F32) * fb_ref[...])).astype(o_ref.dtype)


def _dft_inverse(y, inv, x0, u, filt_bias, tm=512, tn=512):
    b, n2, d = y.shape
    length = n2 // 2
    return pl.pallas_call(
        _dft_inv_kernel,
        grid=(d // tn, length // tm, b),
        in_specs=[
            pl.BlockSpec((tm, n2), lambda j, i, bi: (i, 0)),
            pl.BlockSpec((1, n2, tn), lambda j, i, bi: (bi, 0, j)),
            pl.BlockSpec((1, tm, tn), lambda j, i, bi: (bi, i, j)),
            pl.BlockSpec((1, tm, tn), lambda j, i, bi: (bi, i, j)),
            pl.BlockSpec((1, tn), lambda j, i, bi: (0, j)),
        ],
        out_specs=pl.BlockSpec((1, tm, tn), lambda j, i, bi: (bi, i, j)),
        out_shape=jax.ShapeDtypeStruct((b, length, d), BF16),
        compiler_params=_cparams(("parallel", "parallel", "parallel")),
        name="dft_inverse",
    )(inv, y, x0, u, filt_bias.reshape(1, d))


PACK_HI = 0xFFFF0000
TILE_SUBLANES = 8


def _pack_pairs(lo, hi):
    lo_b = pltpu.bitcast(lo.astype(BF16).astype(F32), jnp.uint32)
    hi_b = pltpu.bitcast(hi.astype(BF16).astype(F32), jnp.uint32)
    return (lo_b >> 16) | (hi_b & jnp.uint32(PACK_HI))


def _unpack_pairs(w):
    return pltpu.bitcast(w << 16, F32), pltpu.bitcast(w & jnp.uint32(PACK_HI), F32)


def _router_kernel(x_ref, g_ref, sh_ref, sc_ref, wr_ref, br_ref, hp_ref, idx_ref, gate_ref, rank_ref, cnt_ref, run):
    tm = x_ref.shape[1]
    half = x_ref.shape[2] // 2
    first = (pl.program_id(0) == 0) & (pl.program_id(1) == 0)

    @pl.when(first)
    def _():
        run[...] = jnp.zeros_like(run)

    x = x_ref[0]
    ms = jnp.mean(x * x, axis=-1, keepdims=True)
    h = x * lax.rsqrt(ms + EPS) * g_ref[...] * (1.0 + sc_ref[0]) + sh_ref[0]
    hp_ref[0] = _pack_pairs(h[:, :half], h[:, half:])

    logits = jnp.dot(h, wr_ref[...], precision=HI, preferred_element_type=F32) + br_ref[...]
    lane_i = lax.broadcasted_iota(jnp.int32, logits.shape, 1)
    lane = lane_i.astype(F32)
    work = jnp.where(lane_i < N_EXPERTS, logits, -jnp.inf)
    li = lax.broadcasted_iota(jnp.int32, (tm, tm), 0)
    ki = lax.broadcasted_iota(jnp.int32, (tm, tm), 1)
    strict_lower = (ki < li).astype(BF16)
    sels, tops = [], []
    for _ in range(TOP_K):
        m = jnp.max(work, axis=-1, keepdims=True)
        first_idx = jnp.min(jnp.where(work == m, lane, float(LANES)), axis=-1, keepdims=True)
        sel = lane == first_idx
        sels.append(sel)
        tops.append((m, first_idx))
        work = jnp.where(sel, -jnp.inf, work)
    chosen = sels[0] | sels[1] | sels[2] | sels[3]
    before = _dot(strict_lower, chosen.astype(BF16)) + run[...]
    run[...] = run[...] + jnp.sum(chosen.astype(F32), axis=0, keepdims=True)
    den = sum(jnp.exp(m - tops[0][0]) for m, _ in tops)
    idx_o = jnp.zeros(logits.shape, jnp.int32)
    gate_o = jnp.zeros(logits.shape, F32)
    rank_o = jnp.zeros(logits.shape, jnp.int32)
    for k, (sel, (m, first_idx)) in enumerate(zip(sels, tops)):
        rank = jnp.sum(jnp.where(sel, before, 0.0), axis=-1, keepdims=True).astype(jnp.int32)
        idx_o = jnp.where(lane_i == k, first_idx.astype(jnp.int32), idx_o)
        gate_o = jnp.where(lane_i == k, jnp.exp(m - tops[0][0]) / den, gate_o)
        rank_o = jnp.where(lane_i == k, rank, rank_o)
    idx_ref[0] = idx_o
    gate_ref[0] = gate_o
    rank_ref[0] = rank_o
    cnt_ref[...] = run[...]


def _moe_route(x, g, shift, scale, w_r, b_r, tm=512):
    b, s, d = x.shape
    wr = jnp.zeros((d, LANES), F32).at[:, :N_EXPERTS].set(w_r)
    br = jnp.zeros((1, LANES), F32).at[0, :N_EXPERTS].set(b_r)
    tok = lambda n: pl.BlockSpec((1, tm, n), lambda bi, i: (bi, i, 0))
    return pl.pallas_call(
        _router_kernel,
        grid=(b, s // tm),
        in_specs=[
            tok(d),
            pl.BlockSpec((1, d), lambda bi, i: (0, 0)),
            pl.BlockSpec((1, 1, d), lambda bi, i: (bi, 0, 0)),
            pl.BlockSpec((1, 1, d), lambda bi, i: (bi, 0, 0)),
            pl.BlockSpec((d, LANES), lambda bi, i: (0, 0)),
            pl.BlockSpec((1, LANES), lambda bi, i: (0, 0)),
        ],
        out_specs=[tok(d // 2), tok(LANES), tok(LANES), tok(LANES), pl.BlockSpec((1, LANES), lambda bi, i: (0, 0))],
        out_shape=[
            jax.ShapeDtypeStruct((b, s, d // 2), jnp.uint32),
            jax.ShapeDtypeStruct((b, s, LANES), jnp.int32),
            jax.ShapeDtypeStruct((b, s, LANES), F32),
            jax.ShapeDtypeStruct((b, s, LANES), jnp.int32),
            jax.ShapeDtypeStruct((1, LANES), F32),
        ],
        scratch_shapes=[pltpu.VMEM((1, LANES), F32)],
        compiler_params=_cparams(("arbitrary", "arbitrary")),
        name="moe_route",
    )(x, g.reshape(1, d), shift.reshape(b, 1, d), scale.reshape(b, 1, d), wr, br)


ROW_COPY_PRIORITY = 1
MOE_PITCH = TILE_SUBLANES + 1


def _ffn_kernel(be_ref, na_ref, dst_prv, tok_cur, tok_nxt, hp_ref, w1g_ref, w1u_ref, b1g_ref, b1u_ref, w2_ref, b2_ref, y_ref,
                xg, stg, xb, acc, gu, zeros, gsem, usem, zsem):
    i = pl.program_id(0)
    f = pl.program_id(1)
    nb = pl.num_programs(0)
    nf = pl.num_programs(1)
    tm = MOE_ROWS
    sub = hp_ref.shape[1]
    half = sub * LANES
    per = tm // MOE_N_FF
    slot = i % 2
    active = i < na_ref[0]

    def gather(tok_ref, to_slot, r, shift=0):
        return pltpu.make_async_copy(hp_ref.at[tok_ref[0, 0, r] + shift], xg.at[to_slot, pl.ds(r * MOE_PITCH, sub), :], gsem.at[to_slot])

    def scatter(dst_ref, from_slot, r, shift=0):
        return pltpu.make_async_copy(stg.at[from_slot, pl.ds(r * MOE_PITCH, sub), :], y_ref.at[dst_ref[0, 0, r] + shift], usem.at[from_slot])

    def wait_rows(buf, sem, s):
        pltpu.make_async_copy(buf.at[1 - s, pl.ds(0, tm * sub), :], buf.at[s, pl.ds(0, tm * sub), :], sem.at[s]).wait()

    @pl.when((i == 0) & (f == 0))
    def _():
        stg[...] = jnp.zeros_like(stg)
        zrows = zeros.shape[0]
        zeros[...] = jnp.zeros_like(zeros)
        dump0 = y_ref.shape[0] - 2 * tm
        fills = [pltpu.make_async_copy(zeros, y_ref.at[pl.ds(dump0 + c * zrows, zrows)], zsem) for c in range(2 * tm // zrows)]
        for cp in fills:
            cp.start()
        for cp in fills:
            cp.wait()
        for r in range(tm):
            gather(tok_cur, 0, r).start(priority=ROW_COPY_PRIORITY)

    @pl.when(f == 0)
    def _():
        wait_rows(xg, gsem, slot)
        for j in range(sub):
            lo, hi = _unpack_pairs(xg[slot, pl.ds(j, tm, stride=MOE_PITCH), :])
            xb[:, j * LANES:(j + 1) * LANES] = lo.astype(BF16)
            xb[:, half + j * LANES:half + (j + 1) * LANES] = hi.astype(BF16)
        acc[...] = jnp.broadcast_to(b2_ref[0, 0], acc.shape)

    def move_rows(group=0, n_groups=1, shift=0):
        for rr in range(per * group // n_groups, per * (group + 1) // n_groups):
            r = f * per + rr
            gather(tok_nxt, 1 - slot, r, shift).start(priority=ROW_COPY_PRIORITY)
            scatter(dst_prv, 1 - slot, r, shift).start(priority=ROW_COPY_PRIORITY)

    def after(v):
        bits = pltpu.bitcast(v[0:1, 0:1], jnp.int32)[0, 0]
        return (bits & 1) >> 1

    @pl.when(active)
    def _():
        pieces = acc.shape[1] // MXU_COLS
        n_groups = pieces + 2
        x = xb[...]
        g = _dot(x, w1g_ref[0, 0]) + b1g_ref[0, 0]
        move_rows(0, n_groups, after(g))
        gu[0] = g
        u = _dot(x, w1u_ref[0, 0]) + b1u_ref[0, 0]
        move_rows(1, n_groups, after(u))
        gu[1] = u
        g = jnp.minimum(gu[0], SWIGLU_LIMIT)
        u = jnp.clip(gu[1], -SWIGLU_LIMIT, SWIGLU_LIMIT)
        a = ((u + 1.0) * g / (1.0 + jnp.exp(-SWIGLU_ALPHA * g))).astype(BF16)
        w2 = w2_ref[0, 0]
        for c in range(pieces):
            cols = slice(c * MXU_COLS, (c + 1) * MXU_COLS)
            y = _dot(a, w2[:, cols])
            acc[:, cols] += y
            move_rows(c + 2, n_groups, after(y))

    @pl.when(jnp.logical_not(active))
    def _():
        move_rows()

    @pl.when(f == nf - 1)
    def _():
        @pl.when(i >= 1)
        def _():
            wait_rows(stg, usem, slot)
        for j in range(sub):
            stg[slot, pl.ds(j, tm, stride=MOE_PITCH), :] = _pack_pairs(acc[:, j * LANES:(j + 1) * LANES],
                                                                      acc[:, half + j * LANES:half + (j + 1) * LANES])

    @pl.when((i == nb - 1) & (f == nf - 1))
    def _():
        wait_rows(xg, gsem, 1 - slot)
        wait_rows(stg, usem, 1 - slot)


def _moe_experts(layer, hp_tiles, slot_src, block_e, n_active, w1, b1, w2, b2):
    n_tok, sub, _ = hp_tiles.shape
    assert w1.dtype == BF16 and w2.dtype == BF16
    d = 2 * sub * LANES
    tm, tf = MOE_ROWS, MOE_FF_TILE
    nf = MOE_N_FF
    n_real = slot_src.shape[0] // tm
    nb = n_real + 2
    n_rows = TOP_K * n_tok + 2 * tm
    ext = jnp.full(((nb + 2) * tm,), -1, jnp.int32).at[tm:(n_real + 1) * tm].set(slot_src)
    slot = jnp.arange(ext.shape[0], dtype=jnp.int32)
    tok_ext = (jnp.maximum(ext, 0) >> 2).reshape(nb + 2, 1, tm)
    dump = TOP_K * n_tok + ((slot // tm) % 2) * tm + slot % tm
    dst_ext = jnp.where(ext < 0, dump, (ext & (TOP_K - 1)) * n_tok + (ext >> 2)).reshape(nb + 2, 1, tm)

    def blk(i, na):
        return jnp.minimum(i, na[0] - 1)

    def ff(i, f, na):
        return jnp.where(i < na[0], f, nf - 1)

    smem = lambda off: pl.BlockSpec((1, 1, tm), lambda i, f, be, na: (i + off, 0, 0), memory_space=pltpu.SMEM)
    grid_spec = pltpu.PrefetchScalarGridSpec(
        num_scalar_prefetch=2,
        grid=(nb, nf),
        in_specs=[
            smem(0), smem(1), smem(2),
            pl.BlockSpec(memory_space=pl.ANY),
            pl.BlockSpec((1, 1, d, tf), lambda i, f, be, na: (layer, be[blk(i, na)], 0, ff(i, f, na))),
            pl.BlockSpec((1, 1, d, tf), lambda i, f, be, na: (layer, be[blk(i, na)], 0, nf + ff(i, f, na))),
            pl.BlockSpec((1, 1, 1, tf), lambda i, f, be, na: (layer, be[blk(i, na)], 0, ff(i, f, na))),
            pl.BlockSpec((1, 1, 1, tf), lambda i, f, be, na: (layer, be[blk(i, na)], 0, nf + ff(i, f, na))),
            pl.BlockSpec((1, 1, tf, d), lambda i, f, be, na: (layer, be[blk(i, na)], ff(i, f, na), 0)),
            pl.BlockSpec((1, 1, 1, d), lambda i, f, be, na: (layer, be[blk(i, na)], 0, 0)),
        ],
        out_specs=pl.BlockSpec(memory_space=pl.ANY),
        scratch_shapes=[pltpu.VMEM((2, tm * MOE_PITCH, LANES), jnp.uint32), pltpu.VMEM((2, tm * MOE_PITCH, LANES), jnp.uint32),
                        pltpu.VMEM((tm, d), BF16), pltpu.VMEM((tm, d), F32), pltpu.VMEM((2, tm, tf), F32),
                        pltpu.VMEM((64, sub, LANES), jnp.uint32),
                        pltpu.SemaphoreType.DMA((2,)), pltpu.SemaphoreType.DMA((2,)), pltpu.SemaphoreType.DMA(())],
    )
    depth, ne = w1.shape[:2]
    be_ext = jnp.concatenate([block_e, jnp.full((2,), N_EXPERTS - 1, jnp.int32)])
    return pl.pallas_call(
        _ffn_kernel,
        grid_spec=grid_spec,
        out_shape=jax.ShapeDtypeStruct((n_rows, sub, LANES), jnp.uint32),
        compiler_params=_cparams(("arbitrary", "arbitrary")),
        name="moe_experts",
    )(be_ext, n_active, dst_ext, tok_ext, tok_ext, hp_tiles, w1, w1, b1.reshape(depth, ne, 1, -1), b1.reshape(depth, ne, 1, -1), w2,
      b2.reshape(depth, ne, 1, -1))


def _combine_kernel(*refs):
    y_refs, (gate_ref, x_ref, gm_ref, o_ref) = refs[:TOP_K], refs[TOP_K:]
    half = y_refs[0].shape[1]
    gates = gate_ref[0]
    lo_acc = jnp.zeros((x_ref.shape[1], half), F32)
    hi_acc = lo_acc
    for k in range(TOP_K):
        lo, hi = _unpack_pairs(y_refs[k][...])
        lo_acc = lo_acc + gates[:, k:k + 1] * lo
        hi_acc = hi_acc + gates[:, k:k + 1] * hi
    o_ref[0, :, :half] = x_ref[0, :, :half] + gm_ref[0, :, :half] * lo_acc
    o_ref[0, :, half:] = x_ref[0, :, half:] + gm_ref[0, :, half:] * hi_acc


def _moe_combine(y_tok, gates, x, gmod, tc=256):
    b, s, d = x.shape
    nt = s // tc
    plane = lambda k: pl.BlockSpec((tc, d // 2), lambda bi, i: (k * b * nt + bi * nt + i, 0))
    return pl.pallas_call(
        _combine_kernel,
        grid=(b, nt),
        in_specs=[plane(k) for k in range(TOP_K)] + [
            pl.BlockSpec((1, tc, LANES), lambda bi, i: (bi, i, 0)),
            pl.BlockSpec((1, tc, d), lambda bi, i: (bi, i, 0)),
            pl.BlockSpec((1, 1, d), lambda bi, i: (bi, 0, 0)),
        ],
        out_specs=pl.BlockSpec((1, tc, d), lambda bi, i: (bi, i, 0)),
        out_shape=jax.ShapeDtypeStruct((b, s, d), F32),
        compiler_params=_cparams(("parallel", "parallel")),
        name="moe_combine",
    )(*([y_tok] * TOP_K), gates, x, gmod.reshape(b, 1, d))


def _moe_ffn(layer, x, g, shift, scale, gmod, w_r, b_r, w1, b1, w2, b2):
    b, s, d = x.shape
    n = b * s
    hp, idx, gates, rank, counts = _moe_route(x, g, shift, scale, w_r, b_r)
    counts = counts[0, :N_EXPERTS].astype(jnp.int32)
    padded = (counts + MOE_ROWS - 1) // MOE_ROWS * MOE_ROWS
    pad_end = jnp.cumsum(padded)
    pad_start = pad_end - padded
    idx4 = idx.reshape(n, LANES)[:, :TOP_K]
    onehot = idx4[:, :, None] == jnp.arange(N_EXPERTS, dtype=jnp.int32)
    pos = jnp.sum(jnp.where(onehot, pad_start, 0), axis=-1) + rank.reshape(n, LANES)[:, :TOP_K]
    n_blocks = n * TOP_K // MOE_ROWS + N_EXPERTS
    slot_ids = jnp.arange(n * TOP_K, dtype=jnp.int32)
    slot_src = jnp.full((n_blocks * MOE_ROWS,), -1, jnp.int32).at[pos.reshape(-1)].set(slot_ids)
    block_start = jnp.arange(n_blocks, dtype=jnp.int32) * MOE_ROWS
    block_e = jnp.minimum(jnp.sum(pad_end[None, :] <= block_start[:, None], axis=1), N_EXPERTS - 1).astype(jnp.int32)
    n_active = (pad_end[-1:] // MOE_ROWS).astype(jnp.int32)
    hp_tiles = hp.reshape(n, d // 2 // LANES, LANES)
    y_tok = _moe_experts(layer, hp_tiles, slot_src, block_e, n_active, w1, b1, w2, b2)
    return _moe_combine(y_tok.reshape(y_tok.shape[0], d // 2), gates, x, gmod)


def kernel(x, c, ctx, c_ctx, ada_w, ada_b, norm_mix, norm_ffn, ev_w_in, ev_conv_w, ev_conv_b, ev_q_norm, ev_k_norm, ev_rpb, ev_a_log, ev_dt_bias, ev_d_skip, ev_gate_norm, ev_w_out, od_w_in, od_b_in, od_conv_w, od_conv_b, od_filt_w1, od_filt_b1, od_filt_w2, od_filt_b2, od_filt_w3, od_filt_b3, od_filt_w4, od_filt_freq, od_filt_bias, od_w_out, od_b_out, moe_router_w, moe_router_b, moe_w1, moe_b1, moe_w2, moe_b2):
    b, s, d = x.shape
    assert ada_w.shape[0] == 2 and ev_w_in.shape[0] == 1 and od_w_in.shape[0] == 1 and d == D_MODEL
    cc = jnp.zeros((16, d), F32).at[:b].set(c).at[b].set(c_ctx)
    mod = _ada_mod(cc, ada_w, ada_b)
    chunk = lambda m, k: m[:, k * d:(k + 1) * d]
    moe = (moe_w1.astype(BF16), moe_b1, moe_w2.astype(BF16), moe_b2)

    m0 = mod[0, :b]
    mc0 = jnp.broadcast_to(mod[0, b:b + 1], (b, N_MOD * d))
    w_in = ev_w_in[0]
    n_main = 2 * NA_WIDTH + SSM_D_INNER + NA_WIDTH + SSM_XBC
    kv0 = NA_WIDTH + SSM_D_INNER
    w_main = w_in[:, :n_main].astype(BF16)
    w_ctx = w_in[:, kv0:n_main].astype(BF16)
    w_dt = jnp.zeros((d, LANES), BF16).at[:, :2 * SSM_HEADS].set(w_in[:, n_main:].astype(BF16))
    zb = lambda n: jnp.zeros((n,), F32)
    g0 = norm_mix[0]
    proj = _norm_mod_matmul(x, g0, chunk(m0, 0), chunk(m0, 1), w_main, zb(n_main), BF16, 1024, 1024)
    dt_raw = _norm_mod_matmul(x, g0, chunk(m0, 0), chunk(m0, 1), w_dt, zb(LANES), F32, 1024, LANES)[..., :2 * SSM_HEADS]
    proj_c = _norm_mod_matmul(ctx, g0, chunk(mc0, 0), chunk(mc0, 1), w_ctx, zb(n_main - kv0), BF16, 256, 1024)
    dt_raw_c = _norm_mod_matmul(ctx, g0, chunk(mc0, 0), chunk(mc0, 1), w_dt, zb(LANES), F32, 256, LANES)[..., :2 * SSM_HEADS]
    attn = _attention(proj, proj_c, ev_q_norm[0], ev_k_norm[0], _attn_bias_table(ev_rpb[0], s // GRID_W))
    ssm_args = (ev_conv_w[0], ev_conv_b[0], ev_dt_bias[0], ev_a_log[0])
    x_blk_c = 2 * NA_WIDTH // GROUP_WIDTH
    states = _ssd(proj_c, x_blk_c, None, dt_raw_c, *ssm_args, None, None, None)
    x_blk = (kv0 + 2 * NA_WIDTH) // GROUP_WIDTH
    z_blk = NA_WIDTH // GROUP_WIDTH
    y_ssm = _ssd(proj, x_blk, z_blk, dt_raw, *ssm_args, ev_d_skip[0], ev_gate_norm[0], states)
    x = _proj_residual([attn, y_ssm], ev_w_out[0].astype(BF16), zb(d), chunk(m0, 2), x)
    x = _moe_ffn(0, x, norm_ffn[0], chunk(m0, 3), chunk(m0, 4), chunk(m0, 5), moe_router_w[0], moe_router_b[0], *moe)

    m1 = mod[1, :b]
    proj_h = _norm_mod_matmul(x, norm_mix[1], chunk(m1, 0), chunk(m1, 1), od_w_in[0].astype(BF16), od_b_in[0], BF16, 1024, 1024)
    filt = _hyena_filters(s, od_filt_w1[0], od_filt_b1[0], od_filt_w2[0], od_filt_b2[0], od_filt_w3[0], od_filt_b3[0],
                          od_filt_w4[0], od_filt_freq[0])
    fwd, inv = _dft_matrices(s)
    coef = _filter_spectrum(filt, fwd)
    x0, u = _hyena_conv(proj_h, od_conv_w[0], od_conv_b[0])
    spec = _dft_forward(u, fwd, coef)
    y_h = _dft_inverse(spec.reshape(b, 2 * s, d), inv, x0, u, od_filt_bias[0])
    x = _proj_residual([y_h], od_w_out[0].astype(BF16), od_b_out[0], chunk(m1, 2), x)
    x = _moe_ffn(1, x, norm_ffn[1], chunk(m1, 3), chunk(m1, 4), chunk(m1, 5), moe_router_w[1], moe_router_b[1], *moe)
    return x
```
